```python
import math, functools
import jax, jax.numpy as jnp
from jax import lax
import numpy as np

D_MODEL = 1024
BATCH = 2
SEQ = 8192
DEPTH = 4
DEC_BATCH = 32
DEC_SEQ = 8
PAST_LEN = 8192
PAGE_SIZE = 128

S5_GROUP = 16
S5_GROUPS = 16
D_S5 = S5_GROUP * S5_GROUPS
S5_STATE = 64
D_SGU = 256
SGU_HEADS = 4
SGU_HEAD_DIM = D_SGU // SGU_HEADS
CHUNK = 128
D_CONV = 256
CONV_W = 31
ATT_HEADS = 4
ATT_HD = 64
ATT_VD = 2 * ATT_HD
D_ATT = ATT_HEADS * ATT_VD
ROT_DIM = ATT_HD // 4
ROPE_THETA = 500000.0
Q_BLOCK = 128
ATT_SCALE = ATT_HD ** -0.5
NEG_INF = -1e30
SPLITS = (D_S5, D_S5 + D_SGU, D_S5 + 2 * D_SGU, D_S5 + 2 * D_SGU + 2 * D_CONV,
          D_S5 + 2 * D_SGU + 2 * D_CONV + D_ATT, D_S5 + 2 * D_SGU + 2 * D_CONV + 2 * D_ATT)
D_IN = D_S5 + 2 * D_SGU + 2 * D_CONV + 3 * D_ATT
N_BRANCH = 4
D_FF = 2816
N_EXPERTS = 8
TOP_K = 2
D_FF_EXPERT = 1408
N_DENSE = (DEPTH + 1) // 2
N_MOE = DEPTH // 2
ALPHA = (2 * DEPTH) ** 0.25
BETA = (8 * DEPTH) ** -0.25
LN_EPS = 1e-5
RMS_EPS = 1e-5

kernel_name = 'hybrid_s5_sgu_conformer_diffattn_decoder_step'


def layer_norm(x, g, b):
    xf = x.astype(jnp.float32)
    mu = jnp.mean(xf, axis=-1, keepdims=True)
    var = jnp.mean(jnp.square(xf - mu), axis=-1, keepdims=True)
    return ((xf - mu) * lax.rsqrt(var + LN_EPS) * g.astype(jnp.float32) + b.astype(jnp.float32)).astype(x.dtype)


def partial_rope(x, pos):
    half = ROT_DIM // 2
    inv = ROPE_THETA ** (-jnp.arange(half, dtype=jnp.float32) * (2.0 / ROT_DIM))
    ang = pos.astype(jnp.float32)[:, None] * inv[None, :]
    cos = jnp.cos(ang)[:, None, None, :]
    sin = jnp.sin(ang)[:, None, None, :]
    xf = x.astype(jnp.float32)
    x1 = xf[..., :half]
    x2 = xf[..., half:ROT_DIM]
    out = jnp.concatenate([x1 * cos - x2 * sin, x2 * cos + x1 * sin, xf[..., ROT_DIM:]], axis=-1)
    return out.astype(x.dtype)


def _linear_combine(e1, e2):
    a1, b1 = e1
    a2, b2 = e2
    return a2 * a1, a2 * b1 + b2


def s5_branch(u, s0_re, s0_im, lam_re, lam_im, log_dt, b_re, b_im, c_re, c_im, d_skip, w_glu):
    f32 = jnp.float32
    bt, t, _ = u.shape
    uf = u.astype(f32)
    ug = uf.reshape(bt, t, S5_GROUPS, S5_GROUP)
    lam = lax.complex(lam_re.astype(f32), lam_im.astype(f32))
    dt = jnp.exp(log_dt.astype(f32))[:, None]
    lam_bar = jnp.exp(lam * dt)
    b_bar = ((lam_bar - 1.0) / lam)[..., None] * lax.complex(b_re.astype(f32), b_im.astype(f32))
    bu = lax.complex(jnp.einsum('gpc,btgc->btgp', jnp.real(b_bar), ug),
                     jnp.einsum('gpc,btgc->btgp', jnp.imag(b_bar), ug))
    s0 = lax.complex(s0_re.astype(f32), s0_im.astype(f32))
    bu = bu.at[:, 0].add(lam_bar * s0)
    a = jnp.broadcast_to(lam_bar, bu.shape)
    _, s = lax.associative_scan(_linear_combine, (a, bu), axis=1)
    c = lax.complex(c_re.astype(f32), c_im.astype(f32))
    y = jnp.real(jnp.einsum('gcp,btgp->btgc', c, s)).reshape(bt, t, D_S5) + d_skip.astype(f32) * uf
    y = jax.nn.gelu(y)
    hg = y @ w_glu.astype(f32)
    out = hg[..., :D_S5] * jax.nn.sigmoid(hg[..., D_S5:])
    s_last = s[:, -1]
    return out.astype(u.dtype), jnp.real(s_last).astype(s0_re.dtype), jnp.imag(s_last).astype(s0_re.dtype)


def sgu_mix(v, w_s, b_s):
    bt, t, _ = v.shape
    nc = -(-t // CHUNK)
    vp = jnp.pad(v, ((0, 0), (0, nc * CHUNK - t), (0, 0))).reshape(bt, nc, CHUNK, SGU_HEADS, SGU_HEAD_DIM)
    ws = jnp.where(jnp.tril(jnp.ones((CHUNK, CHUNK), dtype=bool)), w_s, 0)
    z = jnp.einsum('gts,bcsgd->bctgd', ws, vp) + b_s.T[:, :, None]
    return z.reshape(bt, nc * CHUNK, D_SGU)[:, :t]


def conv_branch(g, prefix, conv_w, conv_b, ln_g, ln_b):
    full = jnp.concatenate([prefix.astype(g.dtype), g], axis=1)
    y = lax.conv_general_dilated(full, conv_w[:, None, :].astype(g.dtype), window_strides=(1,), padding='VALID',
                                 dimension_numbers=('NWC', 'WIO', 'NWC'), feature_group_count=D_CONV)
    y = jax.nn.silu(layer_norm(y + conv_b, ln_g, ln_b))
    return y, full[:, -(CONV_W - 1):]


def diff_attn_core(q, k, v, mask, lam, lam_init, subln_g):
    f32 = jnp.float32
    s = jnp.einsum('bqhmd,bkhmd->bhmqk', q.astype(f32), k.astype(f32)) * ATT_SCALE
    p = jax.nn.softmax(jnp.where(mask, s, NEG_INF), axis=-1)
    a = p[:, :, 0] - lam * p[:, :, 1]
    o = jnp.einsum('bhqk,bkhe->bqhe', a, v.astype(f32))
    o = o * lax.rsqrt(jnp.mean(jnp.square(o), axis=-1, keepdims=True) + RMS_EPS)
    return o * subln_g * (1.0 - lam_init)


def attend_prompt(q, k, v, lam, lam_init, subln_g):
    bt, t = q.shape[0], q.shape[1]
    nqb = t // Q_BLOCK
    qb = q.reshape(bt, nqb, Q_BLOCK, ATT_HEADS, 2, ATT_HD).swapaxes(0, 1)
    kpos = jnp.arange(t)

    def one_block(args):
        qi, i = args
        qpos = i * Q_BLOCK + jnp.arange(Q_BLOCK)
        return diff_attn_core(qi, k, v, kpos[None, :] <= qpos[:, None], lam, lam_init, subln_g)

    o = lax.map(one_block, (qb, jnp.arange(nqb)))
    return o.swapaxes(0, 1).reshape(bt, t, D_ATT)


def attend_sample(q, k, v, k_past, v_past, lam, lam_init, subln_g):
    db, tn = q.shape[0], q.shape[1]
    past = k_past.shape[1] * k_past.shape[2]
    k_all = jnp.concatenate([k_past.reshape(db, past, ATT_HEADS, 2, ATT_HD).astype(k.dtype), k], axis=1)
    v_all = jnp.concatenate([v_past.reshape(db, past, ATT_HEADS, ATT_VD).astype(v.dtype), v], axis=1)
    mask = jnp.concatenate([jnp.ones((tn, past), dtype=bool), jnp.tril(jnp.ones((tn, tn), dtype=bool))], axis=1)
    return diff_attn_core(q, k_all, v_all, mask, lam, lam_init, subln_g).reshape(db, tn, D_ATT)


def token_mixer(x, pos, s0_re, s0_im, conv_prefix, attend, w):
    bt, t, _ = x.shape
    h = x @ w['w_in']
    u_a, u_b, v_b, g_c, q, k, v = jnp.split(h, SPLITS, axis=-1)
    y_a, s_re, s_im = s5_branch(u_a, s0_re, s0_im, w['s5_lam_re'], w['s5_lam_im'], w['s5_log_dt'],
                                w['s5_b_re'], w['s5_b_im'], w['s5_c_re'], w['s5_c_im'], w['s5_d'], w['s5_w_glu'])
    vn = layer_norm(jax.nn.gelu(v_b), w['sgu_ln_g'], w['sgu_ln_b'])
    y_b = jax.nn.gelu(u_b) * sgu_mix(vn, w['sgu_w_s'], w['sgu_b_s'])
    glu = g_c[..., :D_CONV] * jax.nn.sigmoid(g_c[..., D_CONV:])
    y_c, conv_buf = conv_branch(glu, conv_prefix, w['conv_w'], w['conv_b'], w['conv_ln_g'], w['conv_ln_b'])
    q = partial_rope(q.reshape(bt, t, ATT_HEADS, 2, ATT_HD), pos)
    k = partial_rope(k.reshape(bt, t, ATT_HEADS, 2, ATT_HD), pos)
    v = v.reshape(bt, t, ATT_HEADS, ATT_VD)
    y_d = attend(q, k, v).astype(x.dtype)
    gates = jax.nn.sigmoid(x @ w['w_gate'] + w['b_gate']).reshape(bt, t, N_BRANCH, D_MODEL)
    merged = (gates[:, :, 0] * (y_a @ w['w_br_a']) + gates[:, :, 1] * (y_b @ w['w_br_b'])
              + gates[:, :, 2] * (y_c @ w['w_br_c']) + gates[:, :, 3] * (y_d @ w['w_br_d']))
    return merged @ w['w_o'], (k, v, s_re, s_im, conv_buf, vn)


def swiglu(x, wg, wu, wd):
    return (jax.nn.silu(x @ wg) * (x @ wu)) @ wd


def moe_ffn(x, w_router, b_router, wg, wu, wd):
    logits = (x @ w_router).astype(jnp.float32) + b_router.astype(jnp.float32)
    top_v, top_i = lax.top_k(logits, TOP_K)
    top_w = jax.nn.softmax(top_v, axis=-1)
    combine = jnp.sum(jax.nn.one_hot(top_i, N_EXPERTS, dtype=jnp.float32) * top_w[..., None], axis=-2)
    y = jnp.zeros_like(x)
    for e in range(N_EXPERTS):
        y = y + combine[..., e:e + 1].astype(x.dtype) * swiglu(x, wg[e], wu[e], wd[e])
    return y


def setup_inputs(seed: int = 0) -> dict:
    key = jax.random.key(seed)
    ks = iter(jax.random.split(key, 96))
    f32 = jnp.float32

    def nrm(shape, scale=1.0):
        return jax.random.normal(next(ks), shape, f32) * scale

    n_pages = PAST_LEN // PAGE_SIZE
    n_used = DEC_BATCH * n_pages
    n_pool = n_used + max(1, n_used // 4)
    page_table = jax.random.permutation(next(ks), n_pool)[:n_used].reshape(DEC_BATCH, n_pages).astype(jnp.int32)
    lam_im0 = jnp.pi * jnp.arange(S5_STATE, dtype=f32)
    return {
        'x_prompt': nrm((BATCH, SEQ, D_MODEL)),
        'x_sample': nrm((DEC_BATCH, DEC_SEQ, D_MODEL)),
        'cache_k': nrm((DEPTH, n_pool, PAGE_SIZE, ATT_HEADS, 2, ATT_HD)),
        'cache_v': nrm((DEPTH, n_pool, PAGE_SIZE, ATT_HEADS, ATT_VD)),
        'state_ssm_re': nrm((DEPTH, DEC_BATCH, S5_GROUPS, S5_STATE), 0.1),
        'state_ssm_im': nrm((DEPTH, DEC_BATCH, S5_GROUPS, S5_STATE), 0.1),
        'state_conv': nrm((DEPTH, DEC_BATCH, CONV_W - 1, D_CONV), 0.5),
        'page_table': page_table,
        'ln_in_g': 1.0 + nrm((D_MODEL,), 0.02),
        'ln_in_b': nrm((D_MODEL,), 0.02),
        'w_in': nrm((DEPTH, D_MODEL, D_IN), D_MODEL ** -0.5),
        'w_gate': nrm((DEPTH, D_MODEL, N_BRANCH * D_MODEL), D_MODEL ** -0.5),
        'b_gate': nrm((DEPTH, N_BRANCH * D_MODEL), 0.02),
        's5_lam_re': -0.5 + nrm((DEPTH, S5_GROUPS, S5_STATE), 0.01),
        's5_lam_im': lam_im0 + nrm((DEPTH, S5_GROUPS, S5_STATE), 0.01),
        's5_log_dt': jax.random.uniform(next(ks), (DEPTH, S5_GROUPS), f32, math.log(0.001), math.log(0.1)),
        's5_b_re': nrm((DEPTH, S5_GROUPS, S5_STATE, S5_GROUP), (2 * S5_GROUP) ** -0.5),
        's5_b_im': nrm((DEPTH, S5_GROUPS, S5_STATE, S5_GROUP), (2 * S5_GROUP) ** -0.5),
        's5_c_re': nrm((DEPTH, S5_GROUPS, S5_GROUP, S5_STATE), S5_STATE ** -0.5),
        's5_c_im': nrm((DEPTH, S5_GROUPS, S5_GROUP, S5_STATE), S5_STATE ** -0.5),
        's5_d': nrm((DEPTH, D_S5)),
        's5_w_glu': nrm((DEPTH, D_S5, 2 * D_S5), D_S5 ** -0.5),
        'sgu_ln_g': 1.0 + nrm((DEPTH, D_SGU), 0.02),
        'sgu_ln_b': nrm((DEPTH, D_SGU), 0.02),
        'sgu_w_s': nrm((DEPTH, SGU_HEADS, CHUNK, CHUNK), 0.5 * CHUNK ** -0.5),
        'sgu_b_s': 1.0 + nrm((DEPTH, SGU_HEADS, CHUNK), 0.02),
        'conv_w': nrm((DEPTH, CONV_W, D_CONV), CONV_W ** -0.5),
        'conv_b': nrm((DEPTH, D_CONV), 0.02),
        'conv_ln_g': 1.0 + nrm((DEPTH, D_CONV), 0.02),
        'conv_ln_b': nrm((DEPTH, D_CONV), 0.02),
        'att_lam_q1': nrm((DEPTH, ATT_HD), 0.1),
        'att_lam_k1': nrm((DEPTH, ATT_HD), 0.1),
        'att_lam_q2': nrm((DEPTH, ATT_HD), 0.1),
        'att_lam_k2': nrm((DEPTH, ATT_HD), 0.1),
        'att_subln_g': 1.0 + nrm((DEPTH, ATT_VD), 0.02),
        'w_br_a': nrm((DEPTH, D_S5, D_MODEL), BETA * D_S5 ** -0.5),
        'w_br_b': nrm((DEPTH, D_SGU, D_MODEL), BETA * D_SGU ** -0.5),
        'w_br_c': nrm((DEPTH, D_CONV, D_MODEL), BETA * D_CONV ** -0.5),
        'w_br_d': nrm((DEPTH, D_ATT, D_MODEL), BETA * D_ATT ** -0.5),
        'w_o': nrm((DEPTH, D_MODEL, D_MODEL), BETA * D_MODEL ** -0.5),
        'ln1_g': 1.0 + nrm((DEPTH, D_MODEL), 0.02),
        'ln1_b': nrm((DEPTH, D_MODEL), 0.02),
        'ffn_wg': nrm((N_DENSE, D_MODEL, D_FF), D_MODEL ** -0.5),
        'ffn_wu': nrm((N_DENSE, D_MODEL, D_FF), D_MODEL ** -0.5),
        'ffn_wd': nrm((N_DENSE, D_FF, D_MODEL), BETA * D_FF ** -0.5),
        'moe_router': nrm((N_MOE, D_MODEL, N_EXPERTS), D_MODEL ** -0.5),
        'moe_router_b': nrm((N_MOE, N_EXPERTS), 0.01),
        'moe_wg': nrm((N_MOE, N_EXPERTS, D_MODEL, D_FF_EXPERT), D_MODEL ** -0.5),
        'moe_wu': nrm((N_MOE, N_EXPERTS, D_MODEL, D_FF_EXPERT), D_MODEL ** -0.5),
        'moe_wd': nrm((N_MOE, N_EXPERTS, D_FF_EXPERT, D_MODEL), BETA * D_FF_EXPERT ** -0.5),
        'ln2_g': 1.0 + nrm((DEPTH, D_MODEL), 0.02),
        'ln2_b': nrm((DEPTH, D_MODEL), 0.02),
    }


def reference(x_prompt, x_sample, cache_k, cache_v, state_ssm_re, state_ssm_im, state_conv, page_table,
              ln_in_g, ln_in_b, w_in, w_gate, b_gate, s5_lam_re, s5_lam_im, s5_log_dt, s5_b_re, s5_b_im,
              s5_c_re, s5_c_im, s5_d, s5_w_glu, sgu_ln_g, sgu_ln_b, sgu_w_s, sgu_b_s, conv_w, conv_b,
              conv_ln_g, conv_ln_b, att_lam_q1, att_lam_k1, att_lam_q2, att_lam_k2, att_subln_g,
              w_br_a, w_br_b, w_br_c, w_br_d, w_o, ln1_g, ln1_b, ffn_wg, ffn_wu, ffn_wd,
              moe_router, moe_router_b, moe_wg, moe_wu, moe_wd, ln2_g, ln2_b):
    f32 = jnp.float32
    bp, tp = x_prompt.shape[0], x_prompt.shape[1]
    past = page_table.shape[1] * PAGE_SIZE
    pos_p = jnp.arange(tp, dtype=jnp.int32)
    pos_s = past + jnp.arange(x_sample.shape[1], dtype=jnp.int32)
    zero_ssm = jnp.zeros((bp, S5_GROUPS, S5_STATE), x_prompt.dtype)
    zero_conv = jnp.zeros((bp, CONV_W - 1, D_CONV), x_prompt.dtype)
    xp = layer_norm(x_prompt, ln_in_g, ln_in_b)
    xs = layer_norm(x_sample, ln_in_g, ln_in_b)
    kp_l, vp_l, srp_l, sip_l, cp_l = [], [], [], [], []
    ks_l, vs_l, srs_l, sis_l, cs_l, chs_l = [], [], [], [], [], []
    for l in range(DEPTH):
        w = {'w_in': w_in[l], 'w_gate': w_gate[l], 'b_gate': b_gate[l],
             's5_lam_re': s5_lam_re[l], 's5_lam_im': s5_lam_im[l], 's5_log_dt': s5_log_dt[l],
             's5_b_re': s5_b_re[l], 's5_b_im': s5_b_im[l], 's5_c_re': s5_c_re[l], 's5_c_im': s5_c_im[l],
             's5_d': s5_d[l], 's5_w_glu': s5_w_glu[l], 'sgu_ln_g': sgu_ln_g[l], 'sgu_ln_b': sgu_ln_b[l],
             'sgu_w_s': sgu_w_s[l], 'sgu_b_s': sgu_b_s[l], 'conv_w': conv_w[l], 'conv_b': conv_b[l],
             'conv_ln_g': conv_ln_g[l], 'conv_ln_b': conv_ln_b[l], 'w_br_a': w_br_a[l], 'w_br_b': w_br_b[l],
             'w_br_c': w_br_c[l], 'w_br_d': w_br_d[l], 'w_o': w_o[l]}
        lam_init = 0.8 - 0.6 * math.exp(-0.3 * l)
        lam = (jnp.exp(jnp.sum(att_lam_q1[l].astype(f32) * att_lam_k1[l].astype(f32)))
               - jnp.exp(jnp.sum(att_lam_q2[l].astype(f32) * att_lam_k2[l].astype(f32))) + lam_init)
        g_sub = att_subln_g[l].astype(f32)
        attend_p = functools.partial(attend_prompt, lam=lam, lam_init=lam_init, subln_g=g_sub)
        attend_s = functools.partial(attend_sample, k_past=cache_k[l, page_table], v_past=cache_v[l, page_table],
                                     lam=lam, lam_init=lam_init, subln_g=g_sub)
        if l % 2 == 0:
            i = l // 2
            ffn = functools.partial(swiglu, wg=ffn_wg[i], wu=ffn_wu[i], wd=ffn_wd[i])
        else:
            i = l // 2
            ffn = functools.partial(moe_ffn, w_router=moe_router[i], b_router=moe_router_b[i],
                                    wg=moe_wg[i], wu=moe_wu[i], wd=moe_wd[i])
        mp, (k_p, v_p, sr_p, si_p, c_p, _) = token_mixer(xp, pos_p, zero_ssm, zero_ssm, zero_conv, attend_p, w)
        xp = layer_norm(ALPHA * xp + mp, ln1_g[l], ln1_b[l])
        xp = layer_norm(ALPHA * xp + ffn(xp), ln2_g[l], ln2_b[l])
        ms, (k_s, v_s, sr_s, si_s, c_s, ch_s) = token_mixer(xs, pos_s, state_ssm_re[l], state_ssm_im[l],
                                                              state_conv[l], attend_s, w)
        xs = layer_norm(ALPHA * xs + ms, ln1_g[l], ln1_b[l])
        xs = layer_norm(ALPHA * xs + ffn(xs), ln2_g[l], ln2_b[l])
        kp_l.append(k_p); vp_l.append(v_p); srp_l.append(sr_p); sip_l.append(si_p); cp_l.append(c_p)
        ks_l.append(k_s); vs_l.append(v_s); srs_l.append(sr_s); sis_l.append(si_s); cs_l.append(c_s)
        chs_l.append(ch_s)
    return (xp, xs,
            jnp.stack(kp_l), jnp.stack(vp_l), jnp.stack(srp_l), jnp.stack(sip_l), jnp.stack(cp_l),
            jnp.stack(ks_l), jnp.stack(vs_l), jnp.stack(srs_l), jnp.stack(sis_l), jnp.stack(cs_l),
            jnp.stack(chs_l))
```

```python
import functools
import math

import jax
import jax.numpy as jnp
from jax import lax
from jax.experimental import pallas as pl
from jax.experimental.pallas import tpu as pltpu

F32 = jnp.float32
BF16 = jnp.bfloat16

D_MODEL = 1024
PAGE_SIZE = 128
S5_GROUP = 16
S5_GROUPS = 16
D_S5 = 256
S5_STATE = 64
D_SGU = 256
SGU_HEADS = 4
CHUNK = 128
D_CONV = 256
CONV_W = 31
ATT_HEADS = 4
ATT_HD = 64
ATT_VD = 128
D_ATT = 512
ROT_DIM = 16
ROPE_THETA = 500000.0
ATT_SCALE = ATT_HD ** -0.5
NEG_INF = -1e30
D_IN = 2816
N_BRANCH = 4
N_EXPERTS = 8
LN_EPS = 1e-5
RMS_EPS = 1e-5

LANES = 128
SUBLANES = 8
HALO = 32
HALO_OFF = HALO - (CONV_W - 1)
VMEM_LIMIT = 56 * 1024 * 1024


def _cparams(sem):
    return pltpu.CompilerParams(dimension_semantics=sem, vmem_limit_bytes=VMEM_LIMIT)


def _const_spec(shape):
    nd = len(shape)
    return pl.BlockSpec(shape, lambda *_: (0,) * nd)


def _ln(x, g, b):
    mu = jnp.mean(x, axis=-1, keepdims=True)
    xc = x - mu
    var = jnp.mean(xc * xc, axis=-1, keepdims=True)
    return xc * lax.rsqrt(var + LN_EPS) * g + b


def _sigmoid(x):
    return 1.0 / (1.0 + jnp.exp(-x))


def _silu(x):
    return x * _sigmoid(x)


def _gelu(x):
    return jax.nn.gelu(x, approximate=True)


def _mm(a, b):
    return jnp.dot(a.astype(BF16), b.astype(BF16), preferred_element_type=F32)


def _mm_nt(a, b):
    return lax.dot_general(a.astype(BF16), b.astype(BF16), (((1,), (1,)), ((), ())),
                           preferred_element_type=F32)


def _rope(x, tab):
    n_rep = x.shape[1] // LANES
    c = jnp.concatenate([tab[:, 0:LANES]] * n_rep, axis=1)
    s_hi = jnp.concatenate([tab[:, LANES:2 * LANES]] * n_rep, axis=1)
    s_lo = jnp.concatenate([tab[:, 2 * LANES:3 * LANES]] * n_rep, axis=1)
    half = ROT_DIM // 2
    x_up = pltpu.roll(x, x.shape[1] - half, 1)
    x_dn = pltpu.roll(x, half, 1)
    return x * c + x_up * s_hi + x_dn * s_lo


def _ln_kernel(x_ref, g_ref, b_ref, o_ref):
    o_ref[...] = _ln(x_ref[...], g_ref[...], b_ref[...])


def _layernorm(x, g, b, tm):
    n, d = x.shape
    return pl.pallas_call(
        _ln_kernel,
        grid=(n // tm,),
        in_specs=[pl.BlockSpec((tm, d), lambda i: (i, 0)), _const_spec((1, d)), _const_spec((1, d))],
        out_specs=pl.BlockSpec((tm, d), lambda i: (i, 0)),
        out_shape=jax.ShapeDtypeStruct((n, d), F32),
        compiler_params=_cparams(("parallel",)),
    )(x, g.reshape(1, d), b.reshape(1, d))


def _proj_split(x, w_ref, tab, sgu_g, sgu_b):
    h = _mm(x, w_ref[...])
    ua = h[:, 0:256]
    gub = _gelu(h[:, 256:512])
    vn = _ln(_gelu(h[:, 512:768]), sgu_g, sgu_b)
    glu = h[:, 768:1024] * _sigmoid(h[:, 1024:1280])
    q = _rope(h[:, 1280:1792], tab) * ATT_SCALE
    k = _rope(h[:, 1792:2304], tab)
    v = h[:, 2304:2816]
    return ua, gub, vn, glu, q, k, v


def _sgu(gub, vn, ws_ref, sb_ref):
    tm = vn.shape[0]
    head_of_lane = lax.broadcasted_iota(jnp.int32, (CHUNK, D_SGU), 1) // (D_SGU // SGU_HEADS)
    outs = []
    for c in range(tm // CHUNK):
        vc = vn[c * CHUNK:(c + 1) * CHUNK]
        z = sb_ref[...]
        for g in range(SGU_HEADS):
            z = z + _mm(ws_ref[g], jnp.where(head_of_lane == g, vc, 0.0))
        outs.append(gub[c * CHUNK:(c + 1) * CHUNK] * z)
    return outs[0] if len(outs) == 1 else jnp.concatenate(outs, axis=0)


def _conv_post(acc, cb, g, b):
    return _silu(_ln(acc + cb, g, b))


def _proj_prompt_kernel(x_ref, w_ref, tab_ref, sg_ref, sbn_ref, ws_ref, sb_ref, cw_ref, cb_ref, cg_ref, cbn_ref,
                        ua_ref, yb_ref, yc_ref, glu_ref, q_ref, k_ref, v_ref, kb_ref, vb_ref,
                        ext_ref, *, tm, tiles_per_seq, rb):
    i = pl.program_id(0)
    ua, gub, vn, glu, q, k, v = _proj_split(x_ref[...], w_ref, tab_ref[...], sg_ref[...], sbn_ref[...])
    ua_ref[...] = ua
    glu_ref[...] = glu
    q_ref[...] = q.astype(BF16)
    k_ref[...] = k
    v_ref[...] = v
    kb_ref[...] = k.astype(BF16)
    vb_ref[...] = v.astype(BF16)
    yb_ref[...] = _sgu(gub, vn, ws_ref, sb_ref)

    @pl.when(i % tiles_per_seq == 0)
    def _():
        ext_ref[0:HALO, :] = jnp.zeros((HALO, D_CONV), F32)

    ext_ref[HALO:HALO + tm, :] = glu
    for r in range(tm // rb):
        acc = jnp.zeros((rb, D_CONV), F32)
        for w in range(CONV_W):
            acc = acc + ext_ref[pl.ds(r * rb + HALO_OFF + w, rb), :] * cw_ref[w:w + 1, :]
        yc_ref[r * rb:(r + 1) * rb, :] = _conv_post(acc, cb_ref[...], cg_ref[...], cbn_ref[...])
    ext_ref[0:HALO, :] = ext_ref[tm:tm + HALO, :]


def _proj_prompt(x, w_in, tab, sgu_g, sgu_b, ws, sb, cw, cb, cg, cbn, *, seq, tm):
    n = x.shape[0]
    tiles_per_seq = seq // tm
    row = lambda width: pl.BlockSpec((tm, width), lambda i: (i, 0))
    f = lambda width, dt=F32: jax.ShapeDtypeStruct((n, width), dt)
    kern = functools.partial(_proj_prompt_kernel, tm=tm, tiles_per_seq=tiles_per_seq, rb=64)
    return pl.pallas_call(
        kern,
        grid=(n // tm,),
        in_specs=[row(D_MODEL), _const_spec((D_MODEL, D_IN)),
                  pl.BlockSpec((tm, 3 * LANES), lambda i: (i % tiles_per_seq, 0)),
                  _const_spec((1, D_SGU)), _const_spec((1, D_SGU)),
                  _const_spec((SGU_HEADS, CHUNK, CHUNK)), _const_spec((CHUNK, D_SGU)),
                  _const_spec((CONV_W, D_CONV)), _const_spec((1, D_CONV)), _const_spec((1, D_CONV)),
                  _const_spec((1, D_CONV))],
        out_specs=[row(D_S5), row(D_SGU), row(D_CONV), row(D_CONV), row(D_ATT), row(D_ATT), row(D_ATT),
                   row(D_ATT), row(D_ATT)],
        out_shape=[f(D_S5), f(D_SGU), f(D_CONV), f(D_CONV), f(D_ATT, BF16), f(D_ATT), f(D_ATT),
                   f(D_ATT, BF16), f(D_ATT, BF16)],
        scratch_shapes=[pltpu.VMEM((tm + HALO, D_CONV), F32)],
        compiler_params=_cparams(("arbitrary",)),
    )(x, w_in, tab, sgu_g, sgu_b, ws, sb, cw, cb, cg, cbn)


def _proj_sample_kernel(x_ref, w_ref, tab_ref, sg_ref, sbn_ref, ws_ref, sb_ref, cw_ref, cb_ref, cg_ref, cbn_ref,
                        pfx_ref,
                        ua_ref, yb_ref, yc_ref, glu_ref, q_ref, k_ref, v_ref, vn_ref,
                        ext_ref, *, nb, tn):
    ua, gub, vn, glu, q, k, v = _proj_split(x_ref[...], w_ref, tab_ref[...], sg_ref[...], sbn_ref[...])
    ua_ref[...] = ua
    glu_ref[...] = glu
    q_ref[...] = q
    k_ref[...] = k
    v_ref[...] = v
    vn_ref[...] = vn
    yb_ref[...] = _sgu(gub, vn, ws_ref, sb_ref)

    ext_ref[:, 0:HALO, :] = pfx_ref[...]
    ext_ref[:, HALO:HALO + tn, :] = glu.reshape(nb, tn, D_CONV)
    acc = jnp.zeros((nb, tn, D_CONV), F32)
    for w in range(CONV_W):
        acc = acc + ext_ref[:, pl.ds(HALO_OFF + w, tn), :] * cw_ref[w:w + 1, :]
    yc_ref[...] = _conv_post(acc.reshape(nb * tn, D_CONV), cb_ref[...], cg_ref[...], cbn_ref[...])


def _proj_sample(x, w_in, tab, sgu_g, sgu_b, ws, sb, cw, cb, cg, cbn, pfx, *, nb, tn):
    n = x.shape[0]
    full = lambda width: _const_spec((n, width))
    f = lambda width: jax.ShapeDtypeStruct((n, width), F32)
    kern = functools.partial(_proj_sample_kernel, nb=nb, tn=tn)
    return pl.pallas_call(
        kern,
        grid=(1,),
        in_specs=[full(D_MODEL), _const_spec((D_MODEL, D_IN)), full(3 * LANES),
                  _const_spec((1, D_SGU)), _const_spec((1, D_SGU)),
                  _const_spec((SGU_HEADS, CHUNK, CHUNK)), _const_spec((CHUNK, D_SGU)),
                  _const_spec((CONV_W, D_CONV)), _const_spec((1, D_CONV)), _const_spec((1, D_CONV)),
                  _const_spec((1, D_CONV)), _const_spec((nb, HALO, D_CONV))],
        out_specs=[full(D_S5), full(D_SGU), full(D_CONV), full(D_CONV), full(D_ATT), full(D_ATT), full(D_ATT),
                   full(D_SGU)],
        out_shape=[f(D_S5), f(D_SGU), f(D_CONV), f(D_CONV), f(D_ATT), f(D_ATT), f(D_ATT), f(D_SGU)],
        scratch_shapes=[pltpu.VMEM((nb, HALO + tn, D_CONV), F32)],
        compiler_params=_cparams(("arbitrary",)),
    )(x, w_in, tab, sgu_g, sgu_b, ws, sb, cw, cb, cg, cbn, pfx)


def _s5_mats(lam_re, lam_im, log_dt, b_re, b_im, c_re, c_im, L):
    G, P, C = S5_GROUPS, S5_STATE, S5_GROUP
    lam = lax.complex(lam_re.astype(F32), lam_im.astype(F32))
    dt = jnp.exp(log_dt.astype(F32))[:, None]
    lam_bar = jnp.exp(lam * dt)
    b_bar = ((lam_bar - 1.0) / lam)[..., None] * lax.complex(b_re.astype(F32), b_im.astype(F32))
    cc = lax.complex(c_re.astype(F32), c_im.astype(F32))
    steps = jnp.arange(L + 1, dtype=F32)
    pw = jnp.exp((lam * dt)[None] * steps[:, None, None])
    kr = (jnp.einsum('gcp,tgp,gpd->tgcd', jnp.real(cc), jnp.real(pw[:L]), jnp.real(b_bar))
          - jnp.einsum('gcp,tgp,gpd->tgcd', jnp.real(cc), jnp.imag(pw[:L]), jnp.imag(b_bar))
          - jnp.einsum('gcp,tgp,gpd->tgcd', jnp.imag(cc), jnp.real(pw[:L]), jnp.imag(b_bar))
          - jnp.einsum('gcp,tgp,gpd->tgcd', jnp.imag(cc), jnp.imag(pw[:L]), jnp.real(b_bar)))
    jj = jnp.arange(L)[:, None]
    ii = jnp.arange(L)[None, :]
    tau = ii - jj
    kt = kr[jnp.clip(tau, 0, L - 1)] * (tau >= 0)[:, :, None, None, None].astype(F32)
    t_mat = kt.transpose(2, 0, 4, 1, 3).reshape(G, L * C, L * C)
    w = pw[L - 1 - jnp.arange(L)][:, :, :, None] * b_bar[None]
    w = w.transpose(1, 0, 3, 2).reshape(G, L * C, P)
    m1 = jnp.concatenate([jnp.real(w), jnp.imag(w)], axis=-1)
    m1w = jnp.concatenate([jnp.imag(w), jnp.real(w)], axis=-1)
    m1cat = jnp.concatenate([m1, m1w], axis=-1)
    vv = cc[None] * pw[1:L + 1][:, :, None, :]
    vv = vv.transpose(1, 3, 0, 2).reshape(G, P, L * C)
    m2 = jnp.concatenate([jnp.real(vv), -jnp.imag(vv)], axis=1)
    a = pw[L]
    a1 = jnp.concatenate([jnp.real(a), jnp.real(a)], axis=-1)
    a2 = jnp.concatenate([-jnp.imag(a), jnp.imag(a)], axis=-1)
    a2w = jnp.concatenate([jnp.imag(a), -jnp.imag(a)], axis=-1)
    return t_mat.astype(BF16), m1cat.astype(BF16), m2.astype(BF16), a1, a2, a2w


def _s5_local_kernel(u_ref, m1_ref, o_ref):
    o_ref[...] = _mm(u_ref[...], m1_ref[...])


def _s5_scan_kernel(sl_ref, s0_ref, s0w_ref, a1_ref, a2_ref, a2w_ref, st_ref, fin_ref, *, cpb):
    sp = 2 * S5_STATE
    a1 = a1_ref[...]
    a2 = a2_ref[...]
    a2w = a2w_ref[...]

    def body(j, carry):
        s, sw = carry
        st_ref[j] = s
        loc = sl_ref[j]
        return a1 * s + a2 * sw + loc[:, :sp], a1 * sw + a2w * s + loc[:, sp:]

    s, _ = lax.fori_loop(0, cpb, body, (s0_ref[...], s0w_ref[...]))
    fin_ref[...] = s


def _s5_out_kernel(u_ref, t_ref, st_ref, m2_ref, d_ref, o_ref):
    u = u_ref[...]
    y = _mm(u, t_ref[...]) + _mm(st_ref[...], m2_ref[...]) + d_ref[...] * u
    o_ref[...] = _gelu(y)


def _s5_branch(ua, s0_re, s0_im, mats, d_skip, *, nb, t, L):
    t_mat, m1cat, m2, a1, a2, a2w = mats
    G, C, P = S5_GROUPS, S5_GROUP, S5_STATE
    cpb = t // L
    ch = nb * cpb
    lc = L * C
    sp = 2 * P
    ug = ua.reshape(nb, cpb, L, G, C).transpose(3, 0, 1, 2, 4).reshape(G, ch, lc)
    sloc = pl.pallas_call(
        _s5_local_kernel,
        grid=(G,),
        in_specs=[pl.BlockSpec((None, ch, lc), lambda g: (g, 0, 0)),
                  pl.BlockSpec((None, lc, 2 * sp), lambda g: (g, 0, 0))],
        out_specs=pl.BlockSpec((ch, 2 * sp), lambda g: (0, g)),
        out_shape=jax.ShapeDtypeStruct((ch, G * 2 * sp), F32),
        compiler_params=_cparams(("parallel",)),
    )(ug, m1cat)
    s0 = jnp.concatenate([s0_re, s0_im], axis=-1).astype(F32)
    s0w = jnp.concatenate([s0_im, s0_re], axis=-1).astype(F32)
    st, fin = pl.pallas_call(
        functools.partial(_s5_scan_kernel, cpb=cpb),
        grid=(nb,),
        in_specs=[pl.BlockSpec((None, cpb, G, 2 * sp), lambda b: (b, 0, 0, 0)),
                  pl.BlockSpec((None, G, sp), lambda b: (b, 0, 0)),
                  pl.BlockSpec((None, G, sp), lambda b: (b, 0, 0)),
                  _const_spec((G, sp)), _const_spec((G, sp)), _const_spec((G, sp))],
        out_specs=[pl.BlockSpec((None, cpb, G, sp), lambda b: (b, 0, 0, 0)),
                   pl.BlockSpec((None, G, sp), lambda b: (b, 0, 0))],
        out_shape=[jax.ShapeDtypeStruct((nb, cpb, G, sp), F32), jax.ShapeDtypeStruct((nb, G, sp), F32)],
        compiler_params=_cparams(("parallel",)),
    )(sloc.reshape(nb, cpb, G, 2 * sp), s0, s0w, a1, a2, a2w)
    d_t = jnp.tile(d_skip.astype(F32).reshape(G, 1, C), (1, L, 1)).reshape(G, 1, lc)
    yg = pl.pallas_call(
        _s5_out_kernel,
        grid=(G,),
        in_specs=[pl.BlockSpec((None, ch, lc), lambda g: (g, 0, 0)),
                  pl.BlockSpec((None, lc, lc), lambda g: (g, 0, 0)),
                  pl.BlockSpec((ch, sp), lambda g: (0, g)),
                  pl.BlockSpec((None, sp, lc), lambda g: (g, 0, 0)),
                  pl.BlockSpec((None, 1, lc), lambda g: (g, 0, 0))],
        out_specs=pl.BlockSpec((None, ch, lc), lambda g: (g, 0, 0)),
        out_shape=jax.ShapeDtypeStruct((G, ch, lc), F32),
        compiler_params=_cparams(("parallel",)),
    )(ug, t_mat, st.reshape(ch, G * sp), m2, d_t)
    y = yg.reshape(G, nb, cpb, L, C).transpose(1, 2, 3, 0, 4).reshape(nb * t, G * C)
    return y, fin[..., :P], fin[..., P:]


def _diff_finish(o1, o2, lam, gs):
    o = o1 - lam * o2
    return o * lax.rsqrt(jnp.mean(o * o, axis=-1, keepdims=True) + RMS_EPS) * gs


def _attn_prompt_kernel(q_ref, k_ref, v_ref, lam_ref, gs_ref, o_ref, q2_ref, m_ref, l_ref, acc_ref, *, tq):
    i = pl.program_id(2)
    q = q_ref[...]
    lane = lax.broadcasted_iota(jnp.int32, (tq, ATT_VD), 1)
    zero = jnp.zeros_like(q)
    q2_ref[0:tq, :] = jnp.where(lane < ATT_HD, q, zero)
    q2_ref[tq:2 * tq, :] = jnp.where(lane >= ATT_HD, q, zero)
    m_ref[...] = jnp.full((2 * tq, 1), NEG_INF, F32)
    l_ref[...] = jnp.zeros((2 * tq, 1), F32)
    acc_ref[...] = jnp.zeros((2 * tq, ATT_VD), F32)

    def step(j, diagonal):
        off = pl.multiple_of(j * tq, tq)
        s = _mm_nt(q2_ref[...], k_ref[pl.ds(off, tq), :])
        if diagonal:
            row = lax.broadcasted_iota(jnp.int32, (2 * tq, tq), 0) % tq
            col = lax.broadcasted_iota(jnp.int32, (2 * tq, tq), 1)
            s = jnp.where(col <= row, s, NEG_INF)
        m_prev = m_ref[...]
        m_new = jnp.maximum(m_prev, jnp.max(s, axis=-1, keepdims=True))
        p = jnp.exp(s - m_new)
        alpha = jnp.exp(m_prev - m_new)
        l_ref[...] = alpha * l_ref[...] + jnp.sum(p, axis=-1, keepdims=True)
        acc_ref[...] = alpha * acc_ref[...] + _mm(p, v_ref[pl.ds(off, tq), :])
        m_ref[...] = m_new

    def body(j, carry):
        step(j, False)
        return carry

    lax.fori_loop(0, i, body, 0)
    step(i, True)
    o = acc_ref[...] / l_ref[...]
    o_ref[...] = _diff_finish(o[0:tq], o[tq:2 * tq], lam_ref[...], gs_ref[...])


def _attn_prompt(qb, kb, vb, lamv, gsv, *, nb, t, tq):
    n = nb * t
    nq = t // tq
    return pl.pallas_call(
        functools.partial(_attn_prompt_kernel, tq=tq),
        grid=(nb, ATT_HEADS, nq),
        in_specs=[pl.BlockSpec((tq, ATT_VD), lambda b, h, i: (b * nq + i, h)),
                  pl.BlockSpec((t, ATT_VD), lambda b, h, i: (b, h)),
                  pl.BlockSpec((t, ATT_VD), lambda b, h, i: (b, h)),
                  _const_spec((1, ATT_VD)), _const_spec((1, ATT_VD))],
        out_specs=pl.BlockSpec((tq, ATT_VD), lambda b, h, i: (b * nq + i, h)),
        out_shape=jax.ShapeDtypeStruct((n, D_ATT), F32),
        scratch_shapes=[pltpu.VMEM((2 * tq, ATT_VD), BF16), pltpu.VMEM((2 * tq, 1), F32),
                        pltpu.VMEM((2 * tq, 1), F32), pltpu.VMEM((2 * tq, ATT_VD), F32)],
        compiler_params=_cparams(("parallel", "parallel", "arbitrary")),
    )(qb, kb, vb, lamv, gsv)


def _attn_sample_kernel(pt_ref, q_ref, kn_ref, vn_ref, lam_ref, gs_ref, *rest, tn, pg):
    k_refs = rest[0:pg]
    v_refs = rest[pg:2 * pg]
    o_ref = rest[2 * pg]
    qbd_ref, m_ref, l_ref, acc_ref = rest[2 * pg + 1:]
    p_idx = pl.program_id(1)
    n_rows = 2 * ATT_HEADS * tn

    def update(s, v):
        m_prev = m_ref[...]
        m_new = jnp.maximum(m_prev, jnp.max(s, axis=-1, keepdims=True))
        p = jnp.exp(s - m_new)
        alpha = jnp.exp(m_prev - m_new)
        l_ref[...] = alpha * l_ref[...] + jnp.sum(p, axis=-1, keepdims=True)
        acc_ref[...] = alpha * acc_ref[...] + _mm(p, v)
        m_ref[...] = m_new

    @pl.when(p_idx == 0)
    def _():
        q = q_ref[...]
        lane_grp = lax.broadcasted_iota(jnp.int32, (tn, D_ATT), 1) // ATT_HD
        pieces = []
        for mp in range(2):
            for h in range(ATT_HEADS):
                pieces.append(jnp.where(lane_grp == 2 * h + mp, q, 0.0))
        qbd_ref[...] = jnp.concatenate(pieces, axis=0).astype(BF16)
        m_ref[...] = jnp.full((n_rows, 1), NEG_INF, F32)
        l_ref[...] = jnp.zeros((n_rows, 1), F32)
        acc_ref[...] = jnp.zeros((n_rows, D_ATT), F32)
        s = _mm_nt(qbd_ref[...], kn_ref[...])
        t_of_row = lax.broadcasted_iota(jnp.int32, (n_rows, PAGE_SIZE), 0) % tn
        col = lax.broadcasted_iota(jnp.int32, (n_rows, PAGE_SIZE), 1)
        update(jnp.where(col <= t_of_row, s, NEG_INF), vn_ref[...])

    kcat = jnp.concatenate([r[...].astype(BF16) for r in k_refs], axis=0)
    vcat = jnp.concatenate([r[...].astype(BF16) for r in v_refs], axis=0)
    update(_mm_nt(qbd_ref[...], kcat), vcat)

    @pl.when(p_idx == pl.num_programs(1) - 1)
    def _():
        half = n_rows // 2
        o = acc_ref[...] / l_ref[...]
        o1 = o[0:half]
        o2 = o[half:n_rows]
        outs = []
        for h in range(ATT_HEADS):
            sl = (slice(h * tn, (h + 1) * tn), slice(h * ATT_VD, (h + 1) * ATT_VD))
            outs.append(_diff_finish(o1[sl], o2[sl], lam_ref[...], gs_ref[...]))
        o_ref[...] = jnp.concatenate(outs, axis=1)


def _attn_sample(q, k_new, v_new, cache_k, cache_v, page_table, layer, lamv, gsv, *, nb, tn, pg):
    n_pages = page_table.shape[1]
    depth, n_pool = cache_k.shape[0], cache_k.shape[1]
    ck = cache_k.reshape(depth, n_pool, PAGE_SIZE, D_ATT)
    cv = cache_v.reshape(depth, n_pool, PAGE_SIZE, D_ATT)
    pad = ((0, 0), (0, PAGE_SIZE - tn), (0, 0))
    kn = jnp.pad(k_new.reshape(nb, tn, D_ATT), pad)
    vn = jnp.pad(v_new.reshape(nb, tn, D_ATT), pad)
    n_rows = 2 * ATT_HEADS * tn

    def page_spec(r):
        return pl.BlockSpec((None, None, PAGE_SIZE, D_ATT),
                            lambda b, p, pt: (layer, pt[b, p * pg + r], 0, 0))

    grid_spec = pltpu.PrefetchScalarGridSpec(
        num_scalar_prefetch=1,
        grid=(nb, n_pages // pg),
        in_specs=[pl.BlockSpec((tn, D_ATT), lambda b, p, pt: (b, 0)),
                  pl.BlockSpec((None, PAGE_SIZE, D_ATT), lambda b, p, pt: (b, 0, 0)),
                  pl.BlockSpec((None, PAGE_SIZE, D_ATT), lambda b, p, pt: (b, 0, 0)),
                  pl.BlockSpec((1, ATT_VD), lambda b, p, pt: (0, 0)),
                  pl.BlockSpec((1, ATT_VD), lambda b, p, pt: (0, 0))]
                 + [page_spec(r) for r in range(pg)] + [page_spec(r) for r in range(pg)],
        out_specs=pl.BlockSpec((tn, D_ATT), lambda b, p, pt: (b, 0)),
        scratch_shapes=[pltpu.VMEM((n_rows, D_ATT), BF16), pltpu.VMEM((n_rows, 1), F32),
                        pltpu.VMEM((n_rows, 1), F32), pltpu.VMEM((n_rows, D_ATT), F32)],
    )
    return pl.pallas_call(
        functools.partial(_attn_sample_kernel, tn=tn, pg=pg),
        grid_spec=grid_spec,
        out_shape=jax.ShapeDtypeStruct((nb * tn, D_ATT), F32),
        compiler_params=_cparams(("parallel", "arbitrary")),
    )(page_table, q, kn, vn, lamv, gsv, *([ck] * pg), *([cv] * pg))


def _merge_kernel(x_ref, ya_ref, yb_ref, yc_ref, yd_ref, wglu_ref, wgate_ref, bgate_ref,
                  wa_ref, wb_ref, wc_ref, wd_ref, wo_ref, g_ref, b_ref, o_ref, *, alpha):
    x = x_ref[...]
    xb = x.astype(BF16)
    hg = _mm(ya_ref[...], wglu_ref[...])
    ya = hg[:, :D_S5] * _sigmoid(hg[:, D_S5:])
    merged = None
    for idx, (y, w_ref) in enumerate(((ya, wa_ref), (yb_ref[...], wb_ref), (yc_ref[...], wc_ref),
                                      (yd_ref[...], wd_ref))):
        cols = slice(idx * D_MODEL, (idx + 1) * D_MODEL)
        gate = _sigmoid(_mm(xb, wgate_ref[:, cols]) + bgate_ref[:, cols])
        term = gate * _mm(y, w_ref[...])
        merged = term if merged is None else merged + term
    out = _mm(merged, wo_ref[...])
    o_ref[...] = _ln(alpha * x + out, g_ref[...], b_ref[...])


def _merge(x, ya, yb, yc, yd, w, *, tm, alpha):
    n = x.shape[0]
    row = lambda width: pl.BlockSpec((tm, width), lambda i: (i, 0))
    return pl.pallas_call(
        functools.partial(_merge_kernel, alpha=alpha),
        grid=(n // tm,),
        in_specs=[row(D_MODEL), row(D_S5), row(D_SGU), row(D_CONV), row(D_ATT),
                  _const_spec((D_S5, 2 * D_S5)), _const_spec((D_MODEL, N_BRANCH * D_MODEL)),
                  _const_spec((1, N_BRANCH * D_MODEL)),
                  _const_spec((D_S5, D_MODEL)), _const_spec((D_SGU, D_MODEL)), _const_spec((D_CONV, D_MODEL)),
                  _const_spec((D_ATT, D_MODEL)), _const_spec((D_MODEL, D_MODEL)),
                  _const_spec((1, D_MODEL)), _const_spec((1, D_MODEL))],
        out_specs=row(D_MODEL),
        out_shape=jax.ShapeDtypeStruct((n, D_MODEL), F32),
        compiler_params=_cparams(("parallel",)),
    )(x, ya, yb, yc, yd, w['w_glu'], w['w_gate'], w['b_gate'], w['w_br_a'], w['w_br_b'], w['w_br_c'],
      w['w_br_d'], w['w_o'], w['ln1_g'], w['ln1_b'])


def _ffn_kernel(x_ref, wg_ref, wu_ref, wd_ref, g_ref, b_ref, o_ref, *, alpha):
    x = x_ref[...]
    xb = x.astype(BF16)
    hid = _silu(_mm(xb, wg_ref[...])) * _mm(xb, wu_ref[...])
    o_ref[...] = _ln(alpha * x + _mm(hid, wd_ref[...]), g_ref[...], b_ref[...])


def _ffn(x, wg, wu, wd, g, b, *, tm, alpha):
    n = x.shape[0]
    dff = wg.shape[1]
    row = pl.BlockSpec((tm, D_MODEL), lambda i: (i, 0))
    return pl.pallas_call(
        functools.partial(_ffn_kernel, alpha=alpha),
        grid=(n // tm,),
        in_specs=[row, _const_spec((D_MODEL, dff)), _const_spec((D_MODEL, dff)), _const_spec((dff, D_MODEL)),
                  _const_spec((1, D_MODEL)), _const_spec((1, D_MODEL))],
        out_specs=row,
        out_shape=jax.ShapeDtypeStruct((n, D_MODEL), F32),
        compiler_params=_cparams(("parallel",)),
    )(x, wg, wu, wd, g, b)


def _split_bf16(a):
    hi = a.astype(BF16)
    lo = (a - hi.astype(F32)).astype(BF16)
    return hi, lo


def _moe_kernel(x_ref, wr_hi_ref, wr_lo_ref, br_ref, wg_ref, wu_ref, wd_ref, g_ref, b_ref, o_ref,
                comb_ref, acc_ref, *, alpha):
    e = pl.program_id(1)
    x = x_ref[...]
    tm = x.shape[0]
    lane = lax.broadcasted_iota(jnp.int32, (tm, LANES), 1)

    @pl.when(e == 0)
    def _():
        x_hi, x_lo = _split_bf16(x)
        logits = (jnp.dot(x_hi, wr_hi_ref[...], preferred_element_type=F32)
                  + jnp.dot(x_hi, wr_lo_ref[...], preferred_element_type=F32)
                  + jnp.dot(x_lo, wr_hi_ref[...], preferred_element_type=F32)) + br_ref[...]
        logits = jnp.where(lane < N_EXPERTS, logits, -jnp.inf)
        lane_f = lane.astype(F32)
        v1 = jnp.max(logits, axis=-1, keepdims=True)
        i1 = jnp.min(jnp.where(logits == v1, lane_f, float(LANES)), axis=-1, keepdims=True)
        rest = jnp.where(lane_f == i1, -jnp.inf, logits)
        v2 = jnp.max(rest, axis=-1, keepdims=True)
        i2 = jnp.min(jnp.where(rest == v2, lane_f, float(LANES)), axis=-1, keepdims=True)
        e2 = jnp.exp(v2 - v1)
        w1 = 1.0 / (1.0 + e2)
        w2 = e2 / (1.0 + e2)
        comb_ref[...] = jnp.where(lane_f == i1, w1, 0.0) + jnp.where(lane_f == i2, w2, 0.0)
        acc_ref[...] = jnp.zeros_like(acc_ref)

    xb = x.astype(BF16)
    hid = _silu(_mm(xb, wg_ref[...])) * _mm(xb, wu_ref[...])
    cw = jnp.sum(jnp.where(lane == e, comb_ref[...], 0.0), axis=-1, keepdims=True)
    acc_ref[...] += cw * _mm(hid, wd_ref[...])

    @pl.when(e == pl.num_programs(1) - 1)
    def _():
        o_ref[...] = _ln(alpha * x + acc_ref[...], g_ref[...], b_ref[...])


def _moe(x, wr_hi, wr_lo, br, wg, wu, wd, g, b, *, tm, alpha):
    n = x.shape[0]
    dfe = wg.shape[2]
    row = pl.BlockSpec((tm, D_MODEL), lambda i, e: (i, 0))
    cst = lambda shape: pl.BlockSpec(shape, lambda i, e: (0,) * len(shape))
    return pl.pallas_call(
        functools.partial(_moe_kernel, alpha=alpha),
        grid=(n // tm, N_EXPERTS),
        in_specs=[row, cst((D_MODEL, LANES)), cst((D_MODEL, LANES)), cst((1, LANES)),
                  pl.BlockSpec((None, D_MODEL, dfe), lambda i, e: (e, 0, 0)),
                  pl.BlockSpec((None, D_MODEL, dfe), lambda i, e: (e, 0, 0)),
                  pl.BlockSpec((None, dfe, D_MODEL), lambda i, e: (e, 0, 0)),
                  cst((1, D_MODEL)), cst((1, D_MODEL))],
        out_specs=row,
        out_shape=jax.ShapeDtypeStruct((n, D_MODEL), F32),
        scratch_shapes=[pltpu.VMEM((tm, LANES), F32), pltpu.VMEM((tm, D_MODEL), F32)],
        compiler_params=_cparams(("parallel", "arbitrary")),
    )(x, wr_hi, wr_lo, br, wg, wu, wd, g, b)


def _rope_table(pos):
    half = ROT_DIM // 2
    inv = ROPE_THETA ** (-jnp.arange(half, dtype=F32) * (2.0 / ROT_DIM))
    ang = pos.astype(F32)[:, None] * inv[None, :]
    cos, sin = jnp.cos(ang), jnp.sin(ang)
    n = pos.shape[0]
    one = jnp.ones((n, ATT_HD - ROT_DIM), F32)
    zero = jnp.zeros((n, ATT_HD - ROT_DIM), F32)
    z8 = jnp.zeros((n, half), F32)
    c = jnp.concatenate([cos, cos, one], axis=1)
    s_hi = jnp.concatenate([-sin, z8, zero], axis=1)
    s_lo = jnp.concatenate([z8, sin, zero], axis=1)
    rep = LANES // ATT_HD
    return jnp.concatenate([jnp.tile(c, (1, rep)), jnp.tile(s_hi, (1, rep)), jnp.tile(s_lo, (1, rep))], axis=1)


def _sgu_tables(w_s, b_s, tn):
    tril = jnp.tril(jnp.ones((CHUNK, CHUNK), dtype=bool))
    ws = jnp.where(tril, w_s, 0)
    hd = D_SGU // SGU_HEADS
    bias_p = jnp.repeat(b_s.T, hd, axis=1)
    rep = CHUNK // tn
    eye = jnp.eye(rep, dtype=w_s.dtype)
    ws_s = jnp.einsum('ab,gts->gatbs', eye, ws[:, :tn, :tn]).reshape(SGU_HEADS, CHUNK, CHUNK)
    bias_s = jnp.tile(bias_p[:tn], (rep, 1))
    return ws.astype(BF16), bias_p.astype(F32), ws_s.astype(BF16), bias_s.astype(F32)


def kernel(x_prompt, x_sample, cache_k, cache_v, state_ssm_re, state_ssm_im, state_conv, page_table, ln_in_g, ln_in_b, w_in, w_gate, b_gate, s5_lam_re, s5_lam_im, s5_log_dt, s5_b_re, s5_b_im, s5_c_re, s5_c_im, s5_d, s5_w_glu, sgu_ln_g, sgu_ln_b, sgu_w_s, sgu_b_s, conv_w, conv_b, conv_ln_g, conv_ln_b, att_lam_q1, att_lam_k1, att_lam_q2, att_lam_k2, att_subln_g, w_br_a, w_br_b, w_br_c, w_br_d, w_o, ln1_g, ln1_b, ffn_wg, ffn_wu, ffn_wd, moe_router, moe_router_b, moe_wg, moe_wu, moe_wd, ln2_g, ln2_b):
    bp, tp, d = x_prompt.shape
    bs, ts, _ = x_sample.shape
    depth = w_in.shape[0]
    n_pages = page_table.shape[1]
    past = n_pages * PAGE_SIZE
    alpha = (2 * depth) ** 0.25
    n_p, n_s = bp * tp, bs * ts
    tm_p = min(512, tp)
    tq = min(256, tp)
    l_p = 16
    pg = min(8, n_pages)
    assert tp % tm_p == 0 and tp % CHUNK == 0 and tp % tq == 0 and tp % l_p == 0 and tp >= CONV_W - 1
    assert n_s % CHUNK == 0 and CHUNK % ts == 0 and ts % SUBLANES == 0 and n_pages % pg == 0

    row1 = lambda a: a.reshape(1, -1).astype(F32)
    tab_p = _rope_table(jnp.arange(tp, dtype=jnp.int32))
    tab_s = jnp.tile(_rope_table(past + jnp.arange(ts, dtype=jnp.int32)), (bs, 1))
    zero_state = jnp.zeros((bp, S5_GROUPS, S5_STATE), F32)

    xp = _layernorm(x_prompt.reshape(n_p, d), ln_in_g, ln_in_b, tm_p)
    xs = _layernorm(x_sample.reshape(n_s, d), ln_in_g, ln_in_b, n_s)

    outs = [[] for _ in range(11)]
    for l in range(depth):
        lam_init = 0.8 - 0.6 * math.exp(-0.3 * l)
        lam = (jnp.exp(jnp.sum(att_lam_q1[l].astype(F32) * att_lam_k1[l].astype(F32)))
               - jnp.exp(jnp.sum(att_lam_q2[l].astype(F32) * att_lam_k2[l].astype(F32))) + lam_init)
        lamv = jnp.full((1, ATT_VD), lam, F32)
        gsv = row1(att_subln_g[l]) * (1.0 - lam_init)
        w_in_b = w_in[l].astype(BF16)
        ws_p, sb_p, ws_s, sb_s = _sgu_tables(sgu_w_s[l], sgu_b_s[l], ts)
        conv_args = (conv_w[l].astype(F32), row1(conv_b[l]), row1(conv_ln_g[l]), row1(conv_ln_b[l]))
        sgu_ln = (row1(sgu_ln_g[l]), row1(sgu_ln_b[l]))
        s5_par = (s5_lam_re[l], s5_lam_im[l], s5_log_dt[l], s5_b_re[l], s5_b_im[l], s5_c_re[l], s5_c_im[l])
        mw = {'w_glu': s5_w_glu[l].astype(BF16), 'w_gate': w_gate[l].astype(BF16), 'b_gate': row1(b_gate[l]),
              'w_br_a': w_br_a[l].astype(BF16), 'w_br_b': w_br_b[l].astype(BF16),
              'w_br_c': w_br_c[l].astype(BF16), 'w_br_d': w_br_d[l].astype(BF16), 'w_o': w_o[l].astype(BF16),
              'ln1_g': row1(ln1_g[l]), 'ln1_b': row1(ln1_b[l])}
        ln2 = (row1(ln2_g[l]), row1(ln2_b[l]))
        i = l // 2
        if l % 2 == 0:
            fw = (ffn_wg[i].astype(BF16), ffn_wu[i].astype(BF16), ffn_wd[i].astype(BF16))
            ffn = lambda x, tm: _ffn(x, *fw, *ln2, tm=tm, alpha=alpha)
        else:
            wr = jnp.pad(moe_router[i].astype(F32), ((0, 0), (0, LANES - N_EXPERTS)))
            wr_hi, wr_lo = _split_bf16(wr)
            br = jnp.pad(moe_router_b[i].astype(F32), (0, LANES - N_EXPERTS)).reshape(1, LANES)
            fw = (moe_wg[i].astype(BF16), moe_wu[i].astype(BF16), moe_wd[i].astype(BF16))
            ffn = lambda x, tm: _moe(x, wr_hi, wr_lo, br, *fw, *ln2, tm=tm, alpha=alpha)

        ua, yb, yc, glu, qb, k, v, kb, vb = _proj_prompt(xp, w_in_b, tab_p, *sgu_ln, ws_p, sb_p, *conv_args,
                                                         seq=tp, tm=tm_p)
        ya, sr_p, si_p = _s5_branch(ua, zero_state, zero_state, _s5_mats(*s5_par, l_p), s5_d[l],
                                    nb=bp, t=tp, L=l_p)
        yd = _attn_prompt(qb, kb, vb, lamv, gsv, nb=bp, t=tp, tq=tq)
        xp = _merge(xp, ya, yb, yc, yd, mw, tm=tm_p, alpha=alpha)
        xp = ffn(xp, tm_p)
        outs[0].append(k.reshape(bp, tp, ATT_HEADS, 2, ATT_HD))
        outs[1].append(v.reshape(bp, tp, ATT_HEADS, ATT_VD))
        outs[2].append(sr_p)
        outs[3].append(si_p)
        outs[4].append(glu.reshape(bp, tp, D_CONV)[:, tp - (CONV_W - 1):])

        pfx = jnp.pad(state_conv[l].astype(F32), ((0, 0), (HALO_OFF, 0), (0, 0)))
        ua, yb, yc, glu, q, k, v, vn = _proj_sample(xs, w_in_b, tab_s, *sgu_ln, ws_s, sb_s, *conv_args, pfx,
                                                    nb=bs, tn=ts)
        ya, sr_s, si_s = _s5_branch(ua, state_ssm_re[l], state_ssm_im[l], _s5_mats(*s5_par, ts), s5_d[l],
                                    nb=bs, t=ts, L=ts)
        yd = _attn_sample(q, k, v, cache_k, cache_v, page_table, l, lamv, gsv, nb=bs, tn=ts, pg=pg)
        xs = _merge(xs, ya, yb, yc, yd, mw, tm=n_s, alpha=alpha)
        xs = ffn(xs, n_s)
        outs[5].append(k.reshape(bs, ts, ATT_HEADS, 2, ATT_HD))
        outs[6].append(v.reshape(bs, ts, ATT_HEADS, ATT_VD))
        outs[7].append(sr_s)
        outs[8].append(si_s)
        full = jnp.concatenate([state_conv[l].astype(F32), glu.reshape(bs, ts, D_CONV)], axis=1)
        outs[9].append(full[:, ts:])
        outs[10].append(vn.reshape(bs, ts, D_SGU))

    return (xp.reshape(bp, tp, d), xs.reshape(bs, ts, d), *[jnp.stack(o) for o in outs])
```

```python
import functools
import math

import jax
import jax.numpy as jnp
from jax import lax
from jax.experimental import pallas as pl
from jax.experimental.pallas import tpu as pltpu

F32 = jnp.float32
BF16 = jnp.bfloat16

D_MODEL = 1024
PAGE_SIZE = 128
S5_GROUP = 16
S5_GROUPS = 16
D_S5 = 256
S5_STATE = 64
D_SGU = 256
SGU_HEADS = 4
CHUNK = 128
D_CONV = 256
CONV_W = 31
ATT_HEADS = 4
ATT_HD = 64
ATT_VD = 128
D_ATT = 512
ROT_DIM = 16
ROPE_THETA = 500000.0
ATT_SCALE = ATT_HD ** -0.5
LOG2E = 1.4426950408889634
NEG_INF = -1e30
D_IN = 2816
N_BRANCH = 4
N_EXPERTS = 8
LN_EPS = 1e-5
RMS_EPS = 1e-5

LANES = 128
SUBLANES = 8
HALO = 32
HALO_OFF = HALO - (CONV_W - 1)
VMEM_LIMIT = 56 * 1024 * 1024


def _cparams(sem):
    return pltpu.CompilerParams(dimension_semantics=sem, vmem_limit_bytes=VMEM_LIMIT)


def _const_spec(shape):
    nd = len(shape)
    return pl.BlockSpec(shape, lambda *_: (0,) * nd)


def _ln(x, g, b):
    mu = jnp.mean(x, axis=-1, keepdims=True)
    xc = x - mu
    var = jnp.mean(xc * xc, axis=-1, keepdims=True)
    return xc * lax.rsqrt(var + LN_EPS) * g + b


def _sigmoid(x):
    return 1.0 / (1.0 + jnp.exp(-x))


def _silu(x):
    return x * _sigmoid(x)


def _gelu(x):
    return jax.nn.gelu(x, approximate=True)


def _mm(a, b):
    return jnp.dot(a.astype(BF16), b.astype(BF16), preferred_element_type=F32)


def _mm_nt(a, b):
    return lax.dot_general(a.astype(BF16), b.astype(BF16), (((1,), (1,)), ((), ())),
                           preferred_element_type=F32)


def _rope(x, tab):
    n_rep = x.shape[1] // LANES
    c = jnp.concatenate([tab[:, 0:LANES]] * n_rep, axis=1)
    s_hi = jnp.concatenate([tab[:, LANES:2 * LANES]] * n_rep, axis=1)
    s_lo = jnp.concatenate([tab[:, 2 * LANES:3 * LANES]] * n_rep, axis=1)
    half = ROT_DIM // 2
    x_up = pltpu.roll(x, x.shape[1] - half, 1)
    x_dn = pltpu.roll(x, half, 1)
    return x * c + x_up * s_hi + x_dn * s_lo


def _ln_kernel(x_ref, g_ref, b_ref, o_ref):
    o_ref[...] = _ln(x_ref[...], g_ref[...], b_ref[...])


def _layernorm(x, g, b, tm):
    n, d = x.shape
    return pl.pallas_call(
        _ln_kernel,
        grid=(n // tm,),
        in_specs=[pl.BlockSpec((tm, d), lambda i: (i, 0)), _const_spec((1, d)), _const_spec((1, d))],
        out_specs=pl.BlockSpec((tm, d), lambda i: (i, 0)),
        out_shape=jax.ShapeDtypeStruct((n, d), F32),
        compiler_params=_cparams(("parallel",)),
    )(x, g.reshape(1, d), b.reshape(1, d))


def _proj_split(x, w_ref, tab, sgu_g, sgu_b, q_scale):
    h = _mm(x, w_ref[...])
    ua = h[:, 0:256]
    gub = _gelu(h[:, 256:512])
    vn = _ln(_gelu(h[:, 512:768]), sgu_g, sgu_b)
    glu = h[:, 768:1024] * _sigmoid(h[:, 1024:1280])
    q = _rope(h[:, 1280:1792], tab) * q_scale
    k = _rope(h[:, 1792:2304], tab)
    v = h[:, 2304:2816]
    return ua, gub, vn, glu, q, k, v


def _sgu(gub, vn, ws_ref, sb_ref):
    tm = vn.shape[0]
    head_of_lane = lax.broadcasted_iota(jnp.int32, (CHUNK, D_SGU), 1) // (D_SGU // SGU_HEADS)
    outs = []
    for c in range(tm // CHUNK):
        vc = vn[c * CHUNK:(c + 1) * CHUNK]
        z = sb_ref[...]
        for g in range(SGU_HEADS):
            z = z + _mm(ws_ref[g], jnp.where(head_of_lane == g, vc, 0.0))
        outs.append(gub[c * CHUNK:(c + 1) * CHUNK] * z)
    return outs[0] if len(outs) == 1 else jnp.concatenate(outs, axis=0)


def _conv_post(acc, cb, g, b):
    return _silu(_ln(acc + cb, g, b))


def _to_planes(rows, tmp_ref, planes_ref):
    n_planes, per_plane = planes_ref.shape[0], planes_ref.shape[1]
    for h in range(tmp_ref.shape[0]):
        tmp_ref[h] = rows[:, h * LANES:(h + 1) * LANES]
    for i in range(n_planes):
        for h in range(tmp_ref.shape[0]):
            planes_ref[i, :, h * LANES:(h + 1) * LANES] = tmp_ref[h, pl.ds(i, per_plane, stride=n_planes), :]


def _from_planes(planes_ref, tmp_ref):
    n_planes, per_plane = planes_ref.shape[0], planes_ref.shape[1]
    for i in range(n_planes):
        for h in range(tmp_ref.shape[0]):
            tmp_ref[h, pl.ds(i, per_plane, stride=n_planes), :] = planes_ref[i, :, h * LANES:(h + 1) * LANES]
    return jnp.concatenate([tmp_ref[h] for h in range(tmp_ref.shape[0])], axis=1)


def _proj_prompt_kernel(x_ref, w_ref, tab_ref, sg_ref, sbn_ref, ws_ref, sb_ref, cw_ref, cb_ref, cg_ref, cbn_ref,
                        up_ref, yb_ref, yc_ref, glu_ref, q_ref, k_ref, v_ref, kb_ref, vx_ref,
                        ext_ref, ua_ref, *, tm, tiles_per_seq, rb):
    i = pl.program_id(0)
    ua, gub, vn, glu, q, k, v = _proj_split(x_ref[...], w_ref, tab_ref[...], sg_ref[...], sbn_ref[...],
                                            ATT_SCALE * LOG2E)
    _to_planes(ua, ua_ref, up_ref)
    glu_ref[...] = glu
    q_ref[...] = q.astype(BF16)
    k_ref[...] = k
    v_ref[...] = v
    kb_ref[...] = k.astype(BF16)
    vb = v.astype(BF16)
    ones = jnp.ones((tm, ATT_VD), BF16)
    vx_ref[...] = jnp.concatenate(
        [piece for h in range(ATT_HEADS) for piece in (vb[:, h * ATT_VD:(h + 1) * ATT_VD], ones)], axis=1)
    yb_ref[...] = _sgu(gub, vn, ws_ref, sb_ref)

    @pl.when(i % tiles_per_seq == 0)
    def _():
        ext_ref[0:HALO, :] = jnp.zeros((HALO, D_CONV), F32)

    ext_ref[HALO:HALO + tm, :] = glu
    for r in range(tm // rb):
        acc = jnp.zeros((rb, D_CONV), F32)
        for w in range(CONV_W):
            acc = acc + ext_ref[pl.ds(r * rb + HALO_OFF + w, rb), :] * cw_ref[w:w + 1, :]
        yc_ref[r * rb:(r + 1) * rb, :] = _conv_post(acc, cb_ref[...], cg_ref[...], cbn_ref[...])
    ext_ref[0:HALO, :] = ext_ref[tm:tm + HALO, :]


def _proj_prompt(x, w_in, tab, sgu_g, sgu_b, ws, sb, cw, cb, cg, cbn, *, seq, tm, chunk):
    n = x.shape[0]
    tiles_per_seq = seq // tm
    row = lambda width: pl.BlockSpec((tm, width), lambda i: (i, 0))
    f = lambda width, dt=F32: jax.ShapeDtypeStruct((n, width), dt)
    planes_spec = pl.BlockSpec((chunk, tm // chunk, D_S5), lambda i: (0, i, 0))
    planes_shape = jax.ShapeDtypeStruct((chunk, n // chunk, D_S5), F32)
    kern = functools.partial(_proj_prompt_kernel, tm=tm, tiles_per_seq=tiles_per_seq, rb=64)
    return pl.pallas_call(
        kern,
        grid=(n // tm,),
        in_specs=[row(D_MODEL), _const_spec((D_MODEL, D_IN)),
                  pl.BlockSpec((tm, 3 * LANES), lambda i: (i % tiles_per_seq, 0)),
                  _const_spec((1, D_SGU)), _const_spec((1, D_SGU)),
                  _const_spec((SGU_HEADS, CHUNK, CHUNK)), _const_spec((CHUNK, D_SGU)),
                  _const_spec((CONV_W, D_CONV)), _const_spec((1, D_CONV)), _const_spec((1, D_CONV)),
                  _const_spec((1, D_CONV))],
        out_specs=[planes_spec, row(D_SGU), row(D_CONV), row(D_CONV), row(D_ATT), row(D_ATT), row(D_ATT),
                   row(D_ATT), row(2 * D_ATT)],
        out_shape=[planes_shape, f(D_SGU), f(D_CONV), f(D_CONV), f(D_ATT, BF16), f(D_ATT), f(D_ATT),
                   f(D_ATT, BF16), f(2 * D_ATT, BF16)],
        scratch_shapes=[pltpu.VMEM((tm + HALO, D_CONV), F32), pltpu.VMEM((D_S5 // LANES, tm, LANES), F32)],
        compiler_params=_cparams(("arbitrary",)),
    )(x, w_in, tab, sgu_g, sgu_b, ws, sb, cw, cb, cg, cbn)


def _proj_sample_kernel(x_ref, w_ref, tab_ref, sg_ref, sbn_ref, ws_ref, sb_ref, cw_ref, cb_ref, cg_ref, cbn_ref,
                        pfx_ref,
                        up_ref, yb_ref, yc_ref, glu_ref, q_ref, k_ref, v_ref, vn_ref,
                        ext_ref, ua_ref, *, nb, tn):
    ua, gub, vn, glu, q, k, v = _proj_split(x_ref[...], w_ref, tab_ref[...], sg_ref[...], sbn_ref[...], ATT_SCALE)
    _to_planes(ua, ua_ref, up_ref)
    glu_ref[...] = glu
    q_ref[...] = q
    k_ref[...] = k
    v_ref[...] = v
    vn_ref[...] = vn
    yb_ref[...] = _sgu(gub, vn, ws_ref, sb_ref)

    ext_ref[:, 0:HALO, :] = pfx_ref[...]
    ext_ref[:, HALO:HALO + tn, :] = glu.reshape(nb, tn, D_CONV)
    acc = jnp.zeros((nb, tn, D_CONV), F32)
    for w in range(CONV_W):
        acc = acc + ext_ref[:, pl.ds(HALO_OFF + w, tn), :] * cw_ref[w:w + 1, :]
    yc_ref[...] = _conv_post(acc.reshape(nb * tn, D_CONV), cb_ref[...], cg_ref[...], cbn_ref[...])


def _proj_sample(x, w_in, tab, sgu_g, sgu_b, ws, sb, cw, cb, cg, cbn, pfx, *, nb, tn):
    n = x.shape[0]
    full = lambda width: _const_spec((n, width))
    f = lambda width: jax.ShapeDtypeStruct((n, width), F32)
    kern = functools.partial(_proj_sample_kernel, nb=nb, tn=tn)
    return pl.pallas_call(
        kern,
        grid=(1,),
        in_specs=[full(D_MODEL), _const_spec((D_MODEL, D_IN)), full(3 * LANES),
                  _const_spec((1, D_SGU)), _const_spec((1, D_SGU)),
                  _const_spec((SGU_HEADS, CHUNK, CHUNK)), _const_spec((CHUNK, D_SGU)),
                  _const_spec((CONV_W, D_CONV)), _const_spec((1, D_CONV)), _const_spec((1, D_CONV)),
                  _const_spec((1, D_CONV)), _const_spec((nb, HALO, D_CONV))],
        out_specs=[_const_spec((tn, nb, D_S5)), full(D_SGU), full(D_CONV), full(D_CONV), full(D_ATT), full(D_ATT),
                   full(D_ATT), full(D_SGU)],
        out_shape=[jax.ShapeDtypeStruct((tn, nb, D_S5), F32), f(D_SGU), f(D_CONV), f(D_CONV), f(D_ATT), f(D_ATT),
                   f(D_ATT), f(D_SGU)],
        scratch_shapes=[pltpu.VMEM((nb, HALO + tn, D_CONV), F32), pltpu.VMEM((D_S5 // LANES, n, LANES), F32)],
        compiler_params=_cparams(("arbitrary",)),
    )(x, w_in, tab, sgu_g, sgu_b, ws, sb, cw, cb, cg, cbn, pfx)


def _s5_mats(lam_re, lam_im, log_dt, b_re, b_im, c_re, c_im, L):
    G, P, C = S5_GROUPS, S5_STATE, S5_GROUP
    lr, li = lam_re.astype(F32), lam_im.astype(F32)
    dt = jnp.exp(log_dt.astype(F32))[:, None]
    steps = jnp.arange(L + 1, dtype=F32)[:, None, None]
    mag = jnp.exp((lr * dt)[None] * steps)
    pr = mag * jnp.cos((li * dt)[None] * steps)
    pi = mag * jnp.sin((li * dt)[None] * steps)
    nr, ni = pr[1] - 1.0, pi[1]
    den = lr * lr + li * li
    fr = (nr * lr + ni * li) / den
    fi = (ni * lr - nr * li) / den
    bre, bim = b_re.astype(F32), b_im.astype(F32)
    br = fr[..., None] * bre - fi[..., None] * bim
    bi = fr[..., None] * bim + fi[..., None] * bre
    cr, ci = c_re.astype(F32), c_im.astype(F32)
    kr = (jnp.einsum('gcp,tgp,gpd->tgcd', cr, pr[:L], br) - jnp.einsum('gcp,tgp,gpd->tgcd', cr, pi[:L], bi)
          - jnp.einsum('gcp,tgp,gpd->tgcd', ci, pr[:L], bi) - jnp.einsum('gcp,tgp,gpd->tgcd', ci, pi[:L], br))
    eye = jnp.eye(G, dtype=F32)
    kbd = jnp.einsum('tgod,gh->tgdho', kr, eye).reshape(L, G * C, G * C)
    rev = L - 1 - jnp.arange(L)
    wr = pr[rev][..., None] * br[None] - pi[rev][..., None] * bi[None]
    wi = pr[rev][..., None] * bi[None] + pi[rev][..., None] * br[None]
    m1re = jnp.einsum('jgpd,gh->jgdhp', wr, eye).reshape(L, G * C, G * P)
    m1im = jnp.einsum('jgpd,gh->jgdhp', wi, eye).reshape(L, G * C, G * P)
    vr = cr[None] * pr[1:L + 1][:, :, None, :] - ci[None] * pi[1:L + 1][:, :, None, :]
    vi = cr[None] * pi[1:L + 1][:, :, None, :] + ci[None] * pr[1:L + 1][:, :, None, :]
    m2re = jnp.einsum('igcp,gh->igphc', vr, eye).reshape(L, G * P, G * C)
    m2im = -jnp.einsum('igcp,gh->igphc', vi, eye).reshape(L, G * P, G * C)
    a_re = pr[L].reshape(1, G * P)
    a_im = pi[L].reshape(1, G * P)
    return (kbd.astype(BF16), m1re.astype(BF16), m1im.astype(BF16), m2re.astype(BF16), m2im.astype(BF16),
            a_re, a_im)


def _s5_local_kernel(up_ref, m1re_ref, m1im_ref, slre_ref, slim_ref):
    @pl.when(pl.program_id(0) == 0)
    def _():
        slre_ref[...] = jnp.zeros_like(slre_ref)
        slim_ref[...] = jnp.zeros_like(slim_ref)

    u = up_ref[...].astype(BF16)
    slre_ref[...] += jnp.dot(u, m1re_ref[...], preferred_element_type=F32)
    slim_ref[...] += jnp.dot(u, m1im_ref[...], preferred_element_type=F32)


def _s5_scan_kernel(slre_ref, slim_ref, s0re_ref, s0im_ref, are_ref, aim_ref,
                    stre_ref, stim_ref, fre_ref, fim_ref, *, nb, cpb):
    ar = are_ref[...]
    ai = aim_ref[...]

    def body(c, carry):
        out = []
        for b in range(nb):
            sr, si = carry[b]
            row = pl.ds(b * cpb + c, 1)
            stre_ref[row, :] = sr
            stim_ref[row, :] = si
            out.append((ar * sr - ai * si + slre_ref[row, :], ar * si + ai * sr + slim_ref[row, :]))
        return tuple(out)

    init = tuple((s0re_ref[b:b + 1, :], s0im_ref[b:b + 1, :]) for b in range(nb))
    fin = lax.fori_loop(0, cpb, body, init)
    for b in range(nb):
        fre_ref[b:b + 1, :] = fin[b][0]
        fim_ref[b:b + 1, :] = fin[b][1]


def _s5_out_kernel(up_ref, kbd_ref, stre_ref, stim_ref, m2re_ref, m2im_ref, d_ref, o_ref):
    i = pl.program_id(0)
    o_ref[...] = (_mm(stre_ref[...], m2re_ref[...]) + _mm(stim_ref[...], m2im_ref[...])
                  + d_ref[...] * up_ref[i])

    def body(tau, carry):
        o_ref[...] += _mm(up_ref[i - tau], kbd_ref[tau])
        return carry

    lax.fori_loop(0, i + 1, body, 0)
    o_ref[...] = _gelu(o_ref[...])


def _s5_branch(up, s0_re, s0_im, mats, d_skip, *, nb, t):
    kbd, m1re, m1im, m2re, m2im, a_re, a_im = mats
    L, ch, dc = up.shape
    G, P = S5_GROUPS, S5_STATE
    gp = G * P
    cpb = t // L
    acc_spec = _const_spec((ch, gp))
    acc_shape = jax.ShapeDtypeStruct((ch, gp), F32)
    slre, slim = pl.pallas_call(
        _s5_local_kernel,
        grid=(L,),
        in_specs=[pl.BlockSpec((None, ch, dc), lambda j: (j, 0, 0)),
                  pl.BlockSpec((None, dc, gp), lambda j: (j, 0, 0)),
                  pl.BlockSpec((None, dc, gp), lambda j: (j, 0, 0))],
        out_specs=[acc_spec, acc_spec],
        out_shape=[acc_shape, acc_shape],
        compiler_params=_cparams(("arbitrary",)),
    )(up, m1re, m1im)
    fin_shape = jax.ShapeDtypeStruct((nb, gp), F32)
    stre, stim, fre, fim = pl.pallas_call(
        functools.partial(_s5_scan_kernel, nb=nb, cpb=cpb),
        out_shape=[acc_shape, acc_shape, fin_shape, fin_shape],
        compiler_params=pltpu.CompilerParams(vmem_limit_bytes=VMEM_LIMIT),
    )(slre, slim, s0_re.reshape(nb, gp).astype(F32), s0_im.reshape(nb, gp).astype(F32), a_re, a_im)
    yp = pl.pallas_call(
        _s5_out_kernel,
        grid=(L,),
        in_specs=[_const_spec((L, ch, dc)), _const_spec((L, dc, dc)), acc_spec, acc_spec,
                  pl.BlockSpec((None, gp, dc), lambda i: (i, 0, 0)),
                  pl.BlockSpec((None, gp, dc), lambda i: (i, 0, 0)),
                  _const_spec((1, dc))],
        out_specs=pl.BlockSpec((None, ch, dc), lambda i: (i, 0, 0)),
        out_shape=jax.ShapeDtypeStruct((L, ch, dc), F32),
        compiler_params=_cparams(("parallel",)),
    )(up, kbd, stre, stim, m2re, m2im, d_skip.astype(F32).reshape(1, dc))
    return yp, fre.reshape(nb, G, P), fim.reshape(nb, G, P)


def _diff_finish(o1, o2, lam, gs):
    o = o1 - lam * o2
    return o * lax.rsqrt(jnp.mean(o * o, axis=-1, keepdims=True) + RMS_EPS) * gs


def _attn_prompt_kernel(q_ref, k_ref, v_ref, lam_ref, gs_ref, o_ref, q2_ref, m_ref, acc_ref, *, tq, rb, unroll):
    i = pl.program_id(2)
    r2 = 2 * tq
    q = q_ref[...]
    lane = lax.broadcasted_iota(jnp.int32, (tq, ATT_VD), 1)
    zero = jnp.zeros_like(q)
    q2_ref[0:tq, :] = jnp.where(lane < ATT_HD, q, zero)
    q2_ref[tq:r2, :] = jnp.where(lane >= ATT_HD, q, zero)
    m_ref[...] = jnp.full((r2, LANES), NEG_INF, F32)
    acc_ref[...] = jnp.zeros((r2, 2 * ATT_VD), F32)
    n_rep = tq // LANES

    def step(j, diagonal):
        off = pl.multiple_of(j * tq, tq)
        kt = k_ref[pl.ds(off, tq), :]
        vt = v_ref[pl.ds(off, tq), :]
        for r in range(r2 // rb):
            rows = slice(r * rb, (r + 1) * rb)
            s = _mm_nt(q2_ref[rows, :], kt)
            if diagonal:
                row = (lax.broadcasted_iota(jnp.int32, (rb, tq), 0) + r * rb) % tq
                col = lax.broadcasted_iota(jnp.int32, (rb, tq), 1)
                s = jnp.where(col <= row, s, NEG_INF)
            m_prev = m_ref[rows, :]
            m_new = jnp.maximum(m_prev, jnp.max(s, axis=-1, keepdims=True))
            alpha = jnp.exp2(m_prev - m_new)
            p = jnp.exp2(s - jnp.concatenate([m_new] * n_rep, axis=1))
            acc_ref[rows, :] = jnp.concatenate([alpha, alpha], axis=1) * acc_ref[rows, :] + _mm(p, vt)
            m_ref[rows, :] = m_new

    def group(g, carry):
        for u in range(unroll):
            step(g * unroll + u, False)
        return carry

    def single(j, carry):
        step(j, False)
        return carry

    n_grp = i // unroll
    lax.fori_loop(0, n_grp, group, 0)
    lax.fori_loop(n_grp * unroll, i, single, 0)
    step(i, True)
    acc = acc_ref[...]
    o = acc[:, 0:ATT_VD] / acc[:, ATT_VD:]
    o_ref[...] = _diff_finish(o[0:tq], o[tq:r2], lam_ref[...], gs_ref[...])


def _attn_prompt(qb, kb, vx, lamv, gsv, *, nb, t, tq):
    n = nb * t
    nq = t // tq
    return pl.pallas_call(
        functools.partial(_attn_prompt_kernel, tq=tq, rb=min(128, tq), unroll=4),
        grid=(nb, ATT_HEADS, nq),
        in_specs=[pl.BlockSpec((tq, ATT_VD), lambda b, h, i: (b * nq + i, h)),
                  pl.BlockSpec((t, ATT_VD), lambda b, h, i: (b, h)),
                  pl.BlockSpec((t, 2 * ATT_VD), lambda b, h, i: (b, h)),
                  _const_spec((1, ATT_VD)), _const_spec((1, ATT_VD))],
        out_specs=pl.BlockSpec((tq, ATT_VD), lambda b, h, i: (b * nq + i, h)),
        out_shape=jax.ShapeDtypeStruct((n, D_ATT), F32),
        scratch_shapes=[pltpu.VMEM((2 * tq, ATT_VD), BF16), pltpu.VMEM((2 * tq, LANES), F32),
                        pltpu.VMEM((2 * tq, 2 * ATT_VD), F32)],
        compiler_params=_cparams(("parallel", "parallel", "arbitrary")),
    )(qb, kb, vx, lamv, gsv)


def _attn_sample_kernel(pt_ref, q_ref, kn_ref, vn_ref, lam_ref, gs_ref, *rest, tn, pg):
    k_refs = rest[0:pg]
    v_refs = rest[pg:2 * pg]
    o_ref = rest[2 * pg]
    qbd_ref, m_ref, l_ref, acc_ref = rest[2 * pg + 1:]
    p_idx = pl.program_id(1)
    rows_h = 2 * tn
    n_rows = ATT_HEADS * rows_h

    def update(s, v_pages):
        n_rep = s.shape[1] // LANES
        m_prev = m_ref[...]
        m_new = jnp.maximum(m_prev, jnp.max(s, axis=-1, keepdims=True))
        p = jnp.exp(s - jnp.concatenate([m_new] * n_rep, axis=1))
        alpha = jnp.exp(m_prev - m_new)
        l_ref[...] = alpha * l_ref[...] + jnp.sum(p, axis=-1, keepdims=True)
        for h in range(ATT_HEADS):
            rows = slice(h * rows_h, (h + 1) * rows_h)
            v_h = [r[pl.ds(h, PAGE_SIZE, stride=ATT_HEADS), :].astype(BF16) for r in v_pages]
            v_h = v_h[0] if len(v_h) == 1 else jnp.concatenate(v_h, axis=0)
            acc_ref[rows, :] = alpha[rows] * acc_ref[rows, :] + _mm(p[rows], v_h)
        m_ref[...] = m_new

    @pl.when(p_idx == 0)
    def _():
        q = q_ref[...]
        lane_grp = lax.broadcasted_iota(jnp.int32, (tn, D_ATT), 1) // ATT_HD
        pieces = [jnp.where(lane_grp == g, q, 0.0) for g in range(2 * ATT_HEADS)]
        qbd_ref[...] = jnp.concatenate(pieces, axis=0).astype(BF16)
        m_ref[...] = jnp.full((n_rows, LANES), NEG_INF, F32)
        l_ref[...] = jnp.zeros((n_rows, LANES), F32)
        acc_ref[...] = jnp.zeros((n_rows, ATT_VD), F32)
        s = _mm(qbd_ref[...], kn_ref[...])
        t_of_row = lax.broadcasted_iota(jnp.int32, (n_rows, PAGE_SIZE), 0) % tn
        col = lax.broadcasted_iota(jnp.int32, (n_rows, PAGE_SIZE), 1)
        update(jnp.where(col <= t_of_row, s, NEG_INF), [vn_ref])

    kcat = jnp.concatenate([r[...].astype(BF16) for r in k_refs], axis=1)
    update(_mm(qbd_ref[...], kcat), list(v_refs))

    @pl.when(p_idx == pl.num_programs(1) - 1)
    def _():
        o = acc_ref[...] / l_ref[...]
        outs = []
        for h in range(ATT_HEADS):
            o1 = o[h * rows_h:h * rows_h + tn]
            o2 = o[h * rows_h + tn:(h + 1) * rows_h]
            outs.append(_diff_finish(o1, o2, lam_ref[...], gs_ref[...]))
        o_ref[...] = jnp.concatenate(outs, axis=1)


def _attn_sample(q, k_new, v_new, cache_k, cache_v, page_table, layer, lamv, gsv, *, nb, tn, pg):
    n_pages = page_table.shape[1]
    depth, n_pool = cache_k.shape[0], cache_k.shape[1]
    rows_v = PAGE_SIZE * ATT_HEADS
    ck = jnp.transpose(cache_k, (0, 1, 3, 4, 5, 2)).reshape(depth, n_pool, D_ATT, PAGE_SIZE)
    cv = cache_v.reshape(depth, n_pool, rows_v, ATT_VD)
    kn = jnp.pad(k_new.reshape(nb, tn, D_ATT).transpose(0, 2, 1), ((0, 0), (0, 0), (0, PAGE_SIZE - tn)))
    vn = jnp.pad(v_new.reshape(nb, tn * ATT_HEADS, ATT_VD), ((0, 0), (0, rows_v - tn * ATT_HEADS), (0, 0)))
    n_rows = 2 * ATT_HEADS * tn

    def page_spec(r, rows, cols):
        return pl.BlockSpec((None, None, rows, cols), lambda b, p, pt: (layer, pt[b, p * pg + r], 0, 0))

    grid_spec = pltpu.PrefetchScalarGridSpec(
        num_scalar_prefetch=1,
        grid=(nb, n_pages // pg),
        in_specs=[pl.BlockSpec((tn, D_ATT), lambda b, p, pt: (b, 0)),
                  pl.BlockSpec((None, D_ATT, PAGE_SIZE), lambda b, p, pt: (b, 0, 0)),
                  pl.BlockSpec((None, rows_v, ATT_VD), lambda b, p, pt: (b, 0, 0)),
                  pl.BlockSpec((1, ATT_VD), lambda b, p, pt: (0, 0)),
                  pl.BlockSpec((1, ATT_VD), lambda b, p, pt: (0, 0))]
                 + [page_spec(r, D_ATT, PAGE_SIZE) for r in range(pg)]
                 + [page_spec(r, rows_v, ATT_VD) for r in range(pg)],
        out_specs=pl.BlockSpec((tn, D_ATT), lambda b, p, pt: (b, 0)),
        scratch_shapes=[pltpu.VMEM((n_rows, D_ATT), BF16), pltpu.VMEM((n_rows, LANES), F32),
                        pltpu.VMEM((n_rows, LANES), F32), pltpu.VMEM((n_rows, ATT_VD), F32)],
    )
    return pl.pallas_call(
        functools.partial(_attn_sample_kernel, tn=tn, pg=pg),
        grid_spec=grid_spec,
        out_shape=jax.ShapeDtypeStruct((nb * tn, D_ATT), F32),
        compiler_params=_cparams(("parallel", "arbitrary")),
    )(page_table, q, kn, vn, lamv, gsv, *([ck] * pg), *([cv] * pg))


def _merge_kernel(x_ref, ya_ref, yb_ref, yc_ref, yd_ref, wglu_ref, wgate_ref, bgate_ref,
                  wa_ref, wb_ref, wc_ref, wd_ref, wo_ref, g_ref, b_ref, o_ref, ya_rows_ref, *, alpha):
    x = x_ref[...]
    xb = x.astype(BF16)
    hg = _mm(_from_planes(ya_ref, ya_rows_ref), wglu_ref[...])
    ya = hg[:, :D_S5] * _sigmoid(hg[:, D_S5:])
    merged = None
    for idx, (y, w_ref) in enumerate(((ya, wa_ref), (yb_ref[...], wb_ref), (yc_ref[...], wc_ref),
                                      (yd_ref[...], wd_ref))):
        cols = slice(idx * D_MODEL, (idx + 1) * D_MODEL)
        gate = _sigmoid(_mm(xb, wgate_ref[:, cols]) + bgate_ref[:, cols])
        term = gate * _mm(y, w_ref[...])
        merged = term if merged is None else merged + term
    out = _mm(merged, wo_ref[...])
    o_ref[...] = _ln(alpha * x + out, g_ref[...], b_ref[...])


def _merge(x, ya, yb, yc, yd, w, *, tm, alpha):
    n = x.shape[0]
    chunk = ya.shape[0]
    row = lambda width: pl.BlockSpec((tm, width), lambda i: (i, 0))
    return pl.pallas_call(
        functools.partial(_merge_kernel, alpha=alpha),
        grid=(n // tm,),
        in_specs=[row(D_MODEL), pl.BlockSpec((chunk, tm // chunk, D_S5), lambda i: (0, i, 0)),
                  row(D_SGU), row(D_CONV), row(D_ATT),
                  _const_spec((D_S5, 2 * D_S5)), _const_spec((D_MODEL, N_BRANCH * D_MODEL)),
                  _const_spec((1, N_BRANCH * D_MODEL)),
                  _const_spec((D_S5, D_MODEL)), _const_spec((D_SGU, D_MODEL)), _const_spec((D_CONV, D_MODEL)),
                  _const_spec((D_ATT, D_MODEL)), _const_spec((D_MODEL, D_MODEL)),
                  _const_spec((1, D_MODEL)), _const_spec((1, D_MODEL))],
        out_specs=row(D_MODEL),
        out_shape=jax.ShapeDtypeStruct((n, D_MODEL), F32),
        scratch_shapes=[pltpu.VMEM((D_S5 // LANES, tm, LANES), F32)],
        compiler_params=_cparams(("parallel",)),
    )(x, ya, yb, yc, yd, w['w_glu'], w['w_gate'], w['b_gate'], w['w_br_a'], w['w_br_b'], w['w_br_c'],
      w['w_br_d'], w['w_o'], w['ln1_g'], w['ln1_b'])


def _ffn_kernel(x_ref, wg_ref, wu_ref, wd_ref, g_ref, b_ref, o_ref, *, alpha):
    x = x_ref[...]
    xb = x.astype(BF16)
    hid = _silu(_mm(xb, wg_ref[...])) * _mm(xb, wu_ref[...])
    o_ref[...] = _ln(alpha * x + _mm(hid, wd_ref[...]), g_ref[...], b_ref[...])


def _ffn(x, wg, wu, wd, g, b, *, tm, alpha):
    n = x.shape[0]
    dff = wg.shape[1]
    row = pl.BlockSpec((tm, D_MODEL), lambda i: (i, 0))
    return pl.pallas_call(
        functools.partial(_ffn_kernel, alpha=alpha),
        grid=(n // tm,),
        in_specs=[row, _const_spec((D_MODEL, dff)), _const_spec((D_MODEL, dff)), _const_spec((dff, D_MODEL)),
                  _const_spec((1, D_MODEL)), _const_spec((1, D_MODEL))],
        out_specs=row,
        out_shape=jax.ShapeDtypeStruct((n, D_MODEL), F32),
        compiler_params=_cparams(("parallel",)),
    )(x, wg, wu, wd, g, b)


def _split_bf16(a):
    hi = a.astype(BF16)
    lo = (a - hi.astype(F32)).astype(BF16)
    return hi, lo


def _moe_kernel(x_ref, wr_hi_ref, wr_lo_ref, br_ref, wg_ref, wu_ref, wd_ref, g_ref, b_ref, o_ref,
                comb_ref, acc_ref, *, alpha):
    e = pl.program_id(1)
    x = x_ref[...]
    tm = x.shape[0]
    lane = lax.broadcasted_iota(jnp.int32, (tm, LANES), 1)

    @pl.when(e == 0)
    def _():
        x_hi, x_lo = _split_bf16(x)
        logits = (jnp.dot(x_hi, wr_hi_ref[...], preferred_element_type=F32)
                  + jnp.dot(x_hi, wr_lo_ref[...], preferred_element_type=F32)
                  + jnp.dot(x_lo, wr_hi_ref[...], preferred_element_type=F32)) + br_ref[...]
        logits = jnp.where(lane < N_EXPERTS, logits, -jnp.inf)
        lane_f = lane.astype(F32)
        v1 = jnp.max(logits, axis=-1, keepdims=True)
        i1 = jnp.min(jnp.where(logits == v1, lane_f, float(LANES)), axis=-1, keepdims=True)
        rest = jnp.where(lane_f == i1, -jnp.inf, logits)
        v2 = jnp.max(rest, axis=-1, keepdims=True)
        i2 = jnp.min(jnp.where(rest == v2, lane_f, float(LANES)), axis=-1, keepdims=True)
        e2 = jnp.exp(v2 - v1)
        w1 = 1.0 / (1.0 + e2)
        w2 = e2 / (1.0 + e2)
        comb_ref[...] = jnp.where(lane_f == i1, w1, 0.0) + jnp.where(lane_f == i2, w2, 0.0)
        acc_ref[...] = jnp.zeros_like(acc_ref)

    xb = x.astype(BF16)
    hid = _silu(_mm(xb, wg_ref[...])) * _mm(xb, wu_ref[...])
    cw = jnp.sum(jnp.where(lane == e, comb_ref[...], 0.0), axis=-1, keepdims=True)
    acc_ref[...] += cw * _mm(hid, wd_ref[...])

    @pl.when(e == pl.num_programs(1) - 1)
    def _():
        o_ref[...] = _ln(alpha * x + acc_ref[...], g_ref[...], b_ref[...])


def _moe(x, wr_hi, wr_lo, br, wg, wu, wd, g, b, *, tm, alpha):
    n = x.shape[0]
    dfe = wg.shape[2]
    row = pl.BlockSpec((tm, D_MODEL), lambda i, e: (i, 0))
    cst = lambda shape: pl.BlockSpec(shape, lambda i, e: (0,) * len(shape))
    return pl.pallas_call(
        functools.partial(_moe_kernel, alpha=alpha),
        grid=(n // tm, N_EXPERTS),
        in_specs=[row, cst((D_MODEL, LANES)), cst((D_MODEL, LANES)), cst((1, LANES)),
                  pl.BlockSpec((None, D_MODEL, dfe), lambda i, e: (e, 0, 0)),
                  pl.BlockSpec((None, D_MODEL, dfe), lambda i, e: (e, 0, 0)),
                  pl.BlockSpec((None, dfe, D_MODEL), lambda i, e: (e, 0, 0)),
                  cst((1, D_MODEL)), cst((1, D_MODEL))],
        out_specs=row,
        out_shape=jax.ShapeDtypeStruct((n, D_MODEL), F32),
        scratch_shapes=[pltpu.VMEM((tm, LANES), F32), pltpu.VMEM((tm, D_MODEL), F32)],
        compiler_params=_cparams(("parallel", "arbitrary")),
    )(x, wr_hi, wr_lo, br, wg, wu, wd, g, b)


def _rope_table(pos):
    half = ROT_DIM // 2
    inv = ROPE_THETA ** (-jnp.arange(half, dtype=F32) * (2.0 / ROT_DIM))
    ang = pos.astype(F32)[:, None] * inv[None, :]
    cos, sin = jnp.cos(ang), jnp.sin(ang)
    n = pos.shape[0]
    one = jnp.ones((n, ATT_HD - ROT_DIM), F32)
    zero = jnp.zeros((n, ATT_HD - ROT_DIM), F32)
    z8 = jnp.zeros((n, half), F32)
    c = jnp.concatenate([cos, cos, one], axis=1)
    s_hi = jnp.concatenate([-sin, z8, zero], axis=1)
    s_lo = jnp.concatenate([z8, sin, zero], axis=1)
    rep = LANES // ATT_HD
    return jnp.concatenate([jnp.tile(c, (1, rep)), jnp.tile(s_hi, (1, rep)), jnp.tile(s_lo, (1, rep))], axis=1)


def _sgu_tables(w_s, b_s, tn):
    tril = jnp.tril(jnp.ones((CHUNK, CHUNK), dtype=bool))
    ws = jnp.where(tril, w_s, 0)
    hd = D_SGU // SGU_HEADS
    bias_p = jnp.repeat(b_s.T, hd, axis=1)
    rep = CHUNK // tn
    eye = jnp.eye(rep, dtype=w_s.dtype)
    ws_s = jnp.einsum('ab,gts->gatbs', eye, ws[:, :tn, :tn]).reshape(SGU_HEADS, CHUNK, CHUNK)
    bias_s = jnp.tile(bias_p[:tn], (rep, 1))
    return ws.astype(BF16), bias_p.astype(F32), ws_s.astype(BF16), bias_s.astype(F32)


def kernel(x_prompt, x_sample, cache_k, cache_v, state_ssm_re, state_ssm_im, state_conv, page_table, ln_in_g, ln_in_b, w_in, w_gate, b_gate, s5_lam_re, s5_lam_im, s5_log_dt, s5_b_re, s5_b_im, s5_c_re, s5_c_im, s5_d, s5_w_glu, sgu_ln_g, sgu_ln_b, sgu_w_s, sgu_b_s, conv_w, conv_b, conv_ln_g, conv_ln_b, att_lam_q1, att_lam_k1, att_lam_q2, att_lam_k2, att_subln_g, w_br_a, w_br_b, w_br_c, w_br_d, w_o, ln1_g, ln1_b, ffn_wg, ffn_wu, ffn_wd, moe_router, moe_router_b, moe_wg, moe_wu, moe_wd, ln2_g, ln2_b):
    bp, tp, d = x_prompt.shape
    bs, ts, _ = x_sample.shape
    depth = w_in.shape[0]
    n_pages = page_table.shape[1]
    past = n_pages * PAGE_SIZE
    alpha = (2 * depth) ** 0.25
    n_p, n_s = bp * tp, bs * ts
    tm_p = min(512, tp)
    tq = min(512, tp)
    l_p = 16
    pg = min(8, n_pages)
    assert tp % tm_p == 0 and tp % CHUNK == 0 and tp % tq == 0 and tp % l_p == 0 and tp >= CONV_W - 1
    assert n_s % CHUNK == 0 and CHUNK % ts == 0 and ts % SUBLANES == 0 and n_pages % pg == 0

    row1 = lambda a: a.reshape(1, -1).astype(F32)
    tab_p = _rope_table(jnp.arange(tp, dtype=jnp.int32))
    tab_s = jnp.tile(_rope_table(past + jnp.arange(ts, dtype=jnp.int32)), (bs, 1))
    zero_state = jnp.zeros((bp, S5_GROUPS, S5_STATE), F32)

    xp = _layernorm(x_prompt.reshape(n_p, d), ln_in_g, ln_in_b, tm_p)
    xs = _layernorm(x_sample.reshape(n_s, d), ln_in_g, ln_in_b, n_s)

    outs = [[] for _ in range(11)]
    for l in range(depth):
        lam_init = 0.8 - 0.6 * math.exp(-0.3 * l)
        lam = (jnp.exp(jnp.sum(att_lam_q1[l].astype(F32) * att_lam_k1[l].astype(F32)))
               - jnp.exp(jnp.sum(att_lam_q2[l].astype(F32) * att_lam_k2[l].astype(F32))) + lam_init)
        lamv = jnp.full((1, ATT_VD), lam, F32)
        gsv = row1(att_subln_g[l]) * (1.0 - lam_init)
        w_in_b = w_in[l].astype(BF16)
        ws_p, sb_p, ws_s, sb_s = _sgu_tables(sgu_w_s[l], sgu_b_s[l], ts)
        conv_args = (conv_w[l].astype(F32), row1(conv_b[l]), row1(conv_ln_g[l]), row1(conv_ln_b[l]))
        sgu_ln = (row1(sgu_ln_g[l]), row1(sgu_ln_b[l]))
        s5_par = (s5_lam_re[l], s5_lam_im[l], s5_log_dt[l], s5_b_re[l], s5_b_im[l], s5_c_re[l], s5_c_im[l])
        mw = {'w_glu': s5_w_glu[l].astype(BF16), 'w_gate': w_gate[l].astype(BF16), 'b_gate': row1(b_gate[l]),
              'w_br_a': w_br_a[l].astype(BF16), 'w_br_b': w_br_b[l].astype(BF16),
              'w_br_c': w_br_c[l].astype(BF16), 'w_br_d': w_br_d[l].astype(BF16), 'w_o': w_o[l].astype(BF16),
              'ln1_g': row1(ln1_g[l]), 'ln1_b': row1(ln1_b[l])}
        ln2 = (row1(ln2_g[l]), row1(ln2_b[l]))
        i = l // 2
        if l % 2 == 0:
            fw = (ffn_wg[i].astype(BF16), ffn_wu[i].astype(BF16), ffn_wd[i].astype(BF16))
            ffn = lambda x, tm: _ffn(x, *fw, *ln2, tm=tm, alpha=alpha)
        else:
            wr = jnp.pad(moe_router[i].astype(F32), ((0, 0), (0, LANES - N_EXPERTS)))
            wr_hi, wr_lo = _split_bf16(wr)
            br = jnp.pad(moe_router_b[i].astype(F32), (0, LANES - N_EXPERTS)).reshape(1, LANES)
            fw = (moe_wg[i].astype(BF16), moe_wu[i].astype(BF16), moe_wd[i].astype(BF16))
            ffn = lambda x, tm: _moe(x, wr_hi, wr_lo, br, *fw, *ln2, tm=tm, alpha=alpha)

        ua, yb, yc, glu, qb, k, v, kb, vx = _proj_prompt(xp, w_in_b, tab_p, *sgu_ln, ws_p, sb_p, *conv_args,
                                                         seq=tp, tm=tm_p, chunk=l_p)
        ya, sr_p, si_p = _s5_branch(ua, zero_state, zero_state, _s5_mats(*s5_par, l_p), s5_d[l], nb=bp, t=tp)
        yd = _attn_prompt(qb, kb, vx, lamv, gsv, nb=bp, t=tp, tq=tq)
        xp = _merge(xp, ya, yb, yc, yd, mw, tm=tm_p, alpha=alpha)
        xp = ffn(xp, tm_p)
        outs[0].append(k.reshape(bp, tp, ATT_HEADS, 2, ATT_HD))
        outs[1].append(v.reshape(bp, tp, ATT_HEADS, ATT_VD))
        outs[2].append(sr_p)
        outs[3].append(si_p)
        outs[4].append(glu.reshape(bp, tp, D_CONV)[:, tp - (CONV_W - 1):])

        pfx = jnp.pad(state_conv[l].astype(F32), ((0, 0), (HALO_OFF, 0), (0, 0)))
        ua, yb, yc, glu, q, k, v, vn = _proj_sample(xs, w_in_b, tab_s, *sgu_ln, ws_s, sb_s, *conv_args, pfx,
                                                    nb=bs, tn=ts)
        ya, sr_s, si_s = _s5_branch(ua, state_ssm_re[l], state_ssm_im[l], _s5_mats(*s5_par, ts), s5_d[l],
                                    nb=bs, t=ts)
        yd = _attn_sample(q, k, v, cache_k, cache_v, page_table, l, lamv, gsv, nb=bs, tn=ts, pg=pg)
        xs = _merge(xs, ya, yb, yc, yd, mw, tm=n_s, alpha=alpha)
        xs = ffn(xs, n_s)
        outs[5].append(k.reshape(bs, ts, ATT_HEADS, 2, ATT_HD))
        outs[6].append(v.reshape(bs, ts, ATT_HEADS, ATT_VD))
        outs[7].append(sr_s)
        outs[8].append(si_s)
        full = jnp.concatenate([state_conv[l].astype(F32), glu.reshape(bs, ts, D_CONV)], axis=1)
        outs[9].append(full[:, ts:])
        outs[10].append(vn.reshape(bs, ts, D_SGU))

    return (xp.reshape(bp, tp, d), xs.reshape(bs, ts, d), *[jnp.stack(o) for o in outs])
```

```python
import functools
import math

import jax
import jax.numpy as jnp
from jax import lax
from jax.experimental import pallas as pl
from jax.experimental.pallas import tpu as pltpu

F32 = jnp.float32
BF16 = jnp.bfloat16

D_MODEL = 1024
PAGE_SIZE = 128
S5_GROUP = 16
S5_GROUPS = 16
D_S5 = 256
S5_STATE = 64
D_SGU = 256
SGU_HEADS = 4
CHUNK = 128
D_CONV = 256
CONV_W = 31
ATT_HEADS = 4
ATT_HD = 64
ATT_VD = 128
D_ATT = 512
ROT_DIM = 16
ROPE_THETA = 500000.0
ATT_SCALE = ATT_HD ** -0.5
LOG2E = 1.4426950408889634
NEG_INF = -1e30
D_IN = 2816
N_BRANCH = 4
N_EXPERTS = 8
TOP_K = 2
LN_EPS = 1e-5
RMS_EPS = 1e-5

LANES = 128
SUBLANES = 8
HALO = 32
HALO_OFF = HALO - (CONV_W - 1)
VMEM_LIMIT = 56 * 1024 * 1024


def _cparams(sem):
    return pltpu.CompilerParams(dimension_semantics=sem, vmem_limit_bytes=VMEM_LIMIT)


def _const_spec(shape):
    nd = len(shape)
    return pl.BlockSpec(shape, lambda *_: (0,) * nd)


def _ln(x, g, b):
    mu = jnp.mean(x, axis=-1, keepdims=True)
    xc = x - mu
    var = jnp.mean(xc * xc, axis=-1, keepdims=True)
    return xc * lax.rsqrt(var + LN_EPS) * g + b


def _sigmoid(x):
    return 1.0 / (1.0 + jnp.exp(-x))


def _silu(x):
    return x * _sigmoid(x)


def _gelu(x):
    return jax.nn.gelu(x, approximate=True)


def _mm(a, b):
    return jnp.dot(a.astype(BF16), b.astype(BF16), preferred_element_type=F32)


def _mm_nt(a, b):
    return lax.dot_general(a.astype(BF16), b.astype(BF16), (((1,), (1,)), ((), ())),
                           preferred_element_type=F32)


def _rope(x, tab):
    n_rep = x.shape[1] // LANES
    c = jnp.concatenate([tab[:, 0:LANES]] * n_rep, axis=1)
    s_hi = jnp.concatenate([tab[:, LANES:2 * LANES]] * n_rep, axis=1)
    s_lo = jnp.concatenate([tab[:, 2 * LANES:3 * LANES]] * n_rep, axis=1)
    half = ROT_DIM // 2
    x_up = pltpu.roll(x, x.shape[1] - half, 1)
    x_dn = pltpu.roll(x, half, 1)
    return x * c + x_up * s_hi + x_dn * s_lo


def _ln_kernel(x_ref, g_ref, b_ref, o_ref):
    o_ref[...] = _ln(x_ref[...], g_ref[...], b_ref[...])


def _layernorm(x, g, b, tm):
    n, d = x.shape
    return pl.pallas_call(
        _ln_kernel,
        grid=(n // tm,),
        in_specs=[pl.BlockSpec((tm, d), lambda i: (i, 0)), _const_spec((1, d)), _const_spec((1, d))],
        out_specs=pl.BlockSpec((tm, d), lambda i: (i, 0)),
        out_shape=jax.ShapeDtypeStruct((n, d), F32),
        compiler_params=_cparams(("parallel",)),
    )(x, g.reshape(1, d), b.reshape(1, d))


def _proj_split(x, w_ref, tab, sgu_g, sgu_b, q_scale):
    h = _mm(x, w_ref[...])
    ua = h[:, 0:256]
    gub = _gelu(h[:, 256:512])
    vn = _ln(_gelu(h[:, 512:768]), sgu_g, sgu_b)
    glu = h[:, 768:1024] * _sigmoid(h[:, 1024:1280])
    q = _rope(h[:, 1280:1792], tab) * q_scale
    k = _rope(h[:, 1792:2304], tab)
    v = h[:, 2304:2816]
    return ua, gub, vn, glu, q, k, v


def _sgu(gub, vn, ws_ref, sb_ref):
    tm = vn.shape[0]
    head_of_lane = lax.broadcasted_iota(jnp.int32, (CHUNK, D_SGU), 1) // (D_SGU // SGU_HEADS)
    outs = []
    for c in range(tm // CHUNK):
        vc = vn[c * CHUNK:(c + 1) * CHUNK]
        z = sb_ref[...]
        for g in range(SGU_HEADS):
            z = z + _mm(ws_ref[g], jnp.where(head_of_lane == g, vc, 0.0))
        outs.append(gub[c * CHUNK:(c + 1) * CHUNK] * z)
    return outs[0] if len(outs) == 1 else jnp.concatenate(outs, axis=0)


def _conv_post(acc, cb, g, b):
    return _silu(_ln(acc + cb, g, b))


def _to_planes(rows, tmp_ref, planes_ref):
    n_planes, per_plane = planes_ref.shape[0], planes_ref.shape[1]
    for h in range(tmp_ref.shape[0]):
        tmp_ref[h] = rows[:, h * LANES:(h + 1) * LANES]
    for i in range(n_planes):
        for h in range(tmp_ref.shape[0]):
            planes_ref[i, :, h * LANES:(h + 1) * LANES] = tmp_ref[h, pl.ds(i, per_plane, stride=n_planes), :]


def _from_planes(planes_ref, tmp_ref):
    n_planes, per_plane = planes_ref.shape[0], planes_ref.shape[1]
    for i in range(n_planes):
        for h in range(tmp_ref.shape[0]):
            tmp_ref[h, pl.ds(i, per_plane, stride=n_planes), :] = planes_ref[i, :, h * LANES:(h + 1) * LANES]
    return jnp.concatenate([tmp_ref[h] for h in range(tmp_ref.shape[0])], axis=1)


def _proj_prompt_kernel(x_ref, w_ref, tab_ref, sg_ref, sbn_ref, ws_ref, sb_ref, cw_ref, cb_ref, cg_ref, cbn_ref,
                        up_ref, yb_ref, yc_ref, glu_ref, q_ref, k_ref, v_ref, kb_ref, vx_ref,
                        ext_ref, ua_ref, *, tm, tiles_per_seq, rb):
    i = pl.program_id(0)
    ua, gub, vn, glu, q, k, v = _proj_split(x_ref[...], w_ref, tab_ref[...], sg_ref[...], sbn_ref[...],
                                            ATT_SCALE * LOG2E)
    _to_planes(ua, ua_ref, up_ref)
    glu_ref[...] = glu
    q_ref[...] = q.astype(BF16)
    k_ref[...] = k
    v_ref[...] = v
    kb_ref[...] = k.astype(BF16)
    vb = v.astype(BF16)
    ones = jnp.ones((tm, ATT_VD), BF16)
    vx_ref[...] = jnp.concatenate(
        [piece for h in range(ATT_HEADS) for piece in (vb[:, h * ATT_VD:(h + 1) * ATT_VD], ones)], axis=1)
    yb_ref[...] = _sgu(gub, vn, ws_ref, sb_ref)

    @pl.when(i % tiles_per_seq == 0)
    def _():
        ext_ref[0:HALO, :] = jnp.zeros((HALO, D_CONV), F32)

    ext_ref[HALO:HALO + tm, :] = glu
    for r in range(tm // rb):
        acc = jnp.zeros((rb, D_CONV), F32)
        for w in range(CONV_W):
            acc = acc + ext_ref[pl.ds(r * rb + HALO_OFF + w, rb), :] * cw_ref[w:w + 1, :]
        yc_ref[r * rb:(r + 1) * rb, :] = _conv_post(acc, cb_ref[...], cg_ref[...], cbn_ref[...])
    ext_ref[0:HALO, :] = ext_ref[tm:tm + HALO, :]


def _proj_prompt(x, w_in, tab, sgu_g, sgu_b, ws, sb, cw, cb, cg, cbn, *, seq, tm, chunk):
    n = x.shape[0]
    tiles_per_seq = seq // tm
    row = lambda width: pl.BlockSpec((tm, width), lambda i: (i, 0))
    f = lambda width, dt=F32: jax.ShapeDtypeStruct((n, width), dt)
    planes_spec = pl.BlockSpec((chunk, tm // chunk, D_S5), lambda i: (0, i, 0))
    planes_shape = jax.ShapeDtypeStruct((chunk, n // chunk, D_S5), F32)
    kern = functools.partial(_proj_prompt_kernel, tm=tm, tiles_per_seq=tiles_per_seq, rb=64)
    return pl.pallas_call(
        kern,
        grid=(n // tm,),
        in_specs=[row(D_MODEL), _const_spec((D_MODEL, D_IN)),
                  pl.BlockSpec((tm, 3 * LANES), lambda i: (i % tiles_per_seq, 0)),
                  _const_spec((1, D_SGU)), _const_spec((1, D_SGU)),
                  _const_spec((SGU_HEADS, CHUNK, CHUNK)), _const_spec((CHUNK, D_SGU)),
                  _const_spec((CONV_W, D_CONV)), _const_spec((1, D_CONV)), _const_spec((1, D_CONV)),
                  _const_spec((1, D_CONV))],
        out_specs=[planes_spec, row(D_SGU), row(D_CONV), row(D_CONV), row(D_ATT), row(D_ATT), row(D_ATT),
                   row(D_ATT), row(2 * D_ATT)],
        out_shape=[planes_shape, f(D_SGU), f(D_CONV), f(D_CONV), f(D_ATT, BF16), f(D_ATT), f(D_ATT),
                   f(D_ATT, BF16), f(2 * D_ATT, BF16)],
        scratch_shapes=[pltpu.VMEM((tm + HALO, D_CONV), F32), pltpu.VMEM((D_S5 // LANES, tm, LANES), F32)],
        compiler_params=_cparams(("arbitrary",)),
    )(x, w_in, tab, sgu_g, sgu_b, ws, sb, cw, cb, cg, cbn)


def _proj_sample_kernel(x_ref, w_ref, tab_ref, sg_ref, sbn_ref, ws_ref, sb_ref, cw_ref, cb_ref, cg_ref, cbn_ref,
                        pfx_ref,
                        up_ref, yb_ref, yc_ref, glu_ref, q_ref, k_ref, v_ref, vn_ref,
                        ext_ref, ua_ref, *, nb, tn):
    ua, gub, vn, glu, q, k, v = _proj_split(x_ref[...], w_ref, tab_ref[...], sg_ref[...], sbn_ref[...], ATT_SCALE)
    _to_planes(ua, ua_ref, up_ref)
    glu_ref[...] = glu
    q_ref[...] = q
    k_ref[...] = k
    v_ref[...] = v
    vn_ref[...] = vn
    yb_ref[...] = _sgu(gub, vn, ws_ref, sb_ref)

    ext_ref[:, 0:HALO, :] = pfx_ref[...]
    ext_ref[:, HALO:HALO + tn, :] = glu.reshape(nb, tn, D_CONV)
    acc = jnp.zeros((nb, tn, D_CONV), F32)
    for w in range(CONV_W):
        acc = acc + ext_ref[:, pl.ds(HALO_OFF + w, tn), :] * cw_ref[w:w + 1, :]
    yc_ref[...] = _conv_post(acc.reshape(nb * tn, D_CONV), cb_ref[...], cg_ref[...], cbn_ref[...])


def _proj_sample(x, w_in, tab, sgu_g, sgu_b, ws, sb, cw, cb, cg, cbn, pfx, *, nb, tn):
    n = x.shape[0]
    full = lambda width: _const_spec((n, width))
    f = lambda width: jax.ShapeDtypeStruct((n, width), F32)
    kern = functools.partial(_proj_sample_kernel, nb=nb, tn=tn)
    return pl.pallas_call(
        kern,
        grid=(1,),
        in_specs=[full(D_MODEL), _const_spec((D_MODEL, D_IN)), full(3 * LANES),
                  _const_spec((1, D_SGU)), _const_spec((1, D_SGU)),
                  _const_spec((SGU_HEADS, CHUNK, CHUNK)), _const_spec((CHUNK, D_SGU)),
                  _const_spec((CONV_W, D_CONV)), _const_spec((1, D_CONV)), _const_spec((1, D_CONV)),
                  _const_spec((1, D_CONV)), _const_spec((nb, HALO, D_CONV))],
        out_specs=[_const_spec((tn, nb, D_S5)), full(D_SGU), full(D_CONV), full(D_CONV), full(D_ATT), full(D_ATT),
                   full(D_ATT), full(D_SGU)],
        out_shape=[jax.ShapeDtypeStruct((tn, nb, D_S5), F32), f(D_SGU), f(D_CONV), f(D_CONV), f(D_ATT), f(D_ATT),
                   f(D_ATT), f(D_SGU)],
        scratch_shapes=[pltpu.VMEM((nb, HALO + tn, D_CONV), F32), pltpu.VMEM((D_S5 // LANES, n, LANES), F32)],
        compiler_params=_cparams(("arbitrary",)),
    )(x, w_in, tab, sgu_g, sgu_b, ws, sb, cw, cb, cg, cbn, pfx)


def _s5_mats(lam_re, lam_im, log_dt, b_re, b_im, c_re, c_im, L):
    G, P, C = S5_GROUPS, S5_STATE, S5_GROUP
    lr, li = lam_re.astype(F32), lam_im.astype(F32)
    dt = jnp.exp(log_dt.astype(F32))[:, None]
    steps = jnp.arange(L + 1, dtype=F32)[:, None, None]
    mag = jnp.exp((lr * dt)[None] * steps)
    pr = mag * jnp.cos((li * dt)[None] * steps)
    pi = mag * jnp.sin((li * dt)[None] * steps)
    nr, ni = pr[1] - 1.0, pi[1]
    den = lr * lr + li * li
    fr = (nr * lr + ni * li) / den
    fi = (ni * lr - nr * li) / den
    bre, bim = b_re.astype(F32), b_im.astype(F32)
    br = fr[..., None] * bre - fi[..., None] * bim
    bi = fr[..., None] * bim + fi[..., None] * bre
    cr, ci = c_re.astype(F32), c_im.astype(F32)
    kr = (jnp.einsum('gcp,tgp,gpd->tgcd', cr, pr[:L], br) - jnp.einsum('gcp,tgp,gpd->tgcd', cr, pi[:L], bi)
          - jnp.einsum('gcp,tgp,gpd->tgcd', ci, pr[:L], bi) - jnp.einsum('gcp,tgp,gpd->tgcd', ci, pi[:L], br))
    eye = jnp.eye(G, dtype=F32)
    kbd = jnp.einsum('tgod,gh->tgdho', kr, eye).reshape(L, G * C, G * C)
    rev = L - 1 - jnp.arange(L)
    wr = pr[rev][..., None] * br[None] - pi[rev][..., None] * bi[None]
    wi = pr[rev][..., None] * bi[None] + pi[rev][..., None] * br[None]
    m1re = jnp.einsum('jgpd,gh->jgdhp', wr, eye).reshape(L, G * C, G * P)
    m1im = jnp.einsum('jgpd,gh->jgdhp', wi, eye).reshape(L, G * C, G * P)
    vr = cr[None] * pr[1:L + 1][:, :, None, :] - ci[None] * pi[1:L + 1][:, :, None, :]
    vi = cr[None] * pi[1:L + 1][:, :, None, :] + ci[None] * pr[1:L + 1][:, :, None, :]
    m2re = jnp.einsum('igcp,gh->igphc', vr, eye).reshape(L, G * P, G * C)
    m2im = -jnp.einsum('igcp,gh->igphc', vi, eye).reshape(L, G * P, G * C)
    a_re = pr[L].reshape(1, G * P)
    a_im = pi[L].reshape(1, G * P)
    return (kbd.astype(BF16), m1re.astype(BF16), m1im.astype(BF16), m2re.astype(BF16), m2im.astype(BF16),
            a_re, a_im)


def _s5_local_kernel(up_ref, m1re_ref, m1im_ref, slre_ref, slim_ref):
    @pl.when(pl.program_id(0) == 0)
    def _():
        slre_ref[...] = jnp.zeros_like(slre_ref)
        slim_ref[...] = jnp.zeros_like(slim_ref)

    u = up_ref[...].astype(BF16)
    slre_ref[...] += jnp.dot(u, m1re_ref[...], preferred_element_type=F32)
    slim_ref[...] += jnp.dot(u, m1im_ref[...], preferred_element_type=F32)


def _s5_scan_kernel(slre_ref, slim_ref, s0re_ref, s0im_ref, are_ref, aim_ref,
                    stre_ref, stim_ref, fre_ref, fim_ref, *, nb, cpb):
    ar = are_ref[...]
    ai = aim_ref[...]

    def body(c, carry):
        out = []
        for b in range(nb):
            sr, si = carry[b]
            row = pl.ds(b * cpb + c, 1)
            stre_ref[row, :] = sr
            stim_ref[row, :] = si
            out.append((ar * sr - ai * si + slre_ref[row, :], ar * si + ai * sr + slim_ref[row, :]))
        return tuple(out)

    init = tuple((s0re_ref[b:b + 1, :], s0im_ref[b:b + 1, :]) for b in range(nb))
    fin = lax.fori_loop(0, cpb, body, init)
    for b in range(nb):
        fre_ref[b:b + 1, :] = fin[b][0]
        fim_ref[b:b + 1, :] = fin[b][1]


def _s5_out_kernel(up_ref, kbd_ref, stre_ref, stim_ref, m2re_ref, m2im_ref, d_ref, o_ref):
    i = pl.program_id(0)
    o_ref[...] = (_mm(stre_ref[...], m2re_ref[...]) + _mm(stim_ref[...], m2im_ref[...])
                  + d_ref[...] * up_ref[i])

    def body(tau, carry):
        o_ref[...] += _mm(up_ref[i - tau], kbd_ref[tau])
        return carry

    lax.fori_loop(0, i + 1, body, 0)
    o_ref[...] = _gelu(o_ref[...])


def _s5_branch(up, s0_re, s0_im, mats, d_skip, *, nb, t):
    kbd, m1re, m1im, m2re, m2im, a_re, a_im = mats
    L, ch, dc = up.shape
    G, P = S5_GROUPS, S5_STATE
    gp = G * P
    cpb = t // L
    acc_spec = _const_spec((ch, gp))
    acc_shape = jax.ShapeDtypeStruct((ch, gp), F32)
    slre, slim = pl.pallas_call(
        _s5_local_kernel,
        grid=(L,),
        in_specs=[pl.BlockSpec((None, ch, dc), lambda j: (j, 0, 0)),
                  pl.BlockSpec((None, dc, gp), lambda j: (j, 0, 0)),
                  pl.BlockSpec((None, dc, gp), lambda j: (j, 0, 0))],
        out_specs=[acc_spec, acc_spec],
        out_shape=[acc_shape, acc_shape],
        compiler_params=_cparams(("arbitrary",)),
    )(up, m1re, m1im)
    fin_shape = jax.ShapeDtypeStruct((nb, gp), F32)
    stre, stim, fre, fim = pl.pallas_call(
        functools.partial(_s5_scan_kernel, nb=nb, cpb=cpb),
        out_shape=[acc_shape, acc_shape, fin_shape, fin_shape],
        compiler_params=pltpu.CompilerParams(vmem_limit_bytes=VMEM_LIMIT),
    )(slre, slim, s0_re.reshape(nb, gp).astype(F32), s0_im.reshape(nb, gp).astype(F32), a_re, a_im)
    yp = pl.pallas_call(
        _s5_out_kernel,
        grid=(L,),
        in_specs=[_const_spec((L, ch, dc)), _const_spec((L, dc, dc)), acc_spec, acc_spec,
                  pl.BlockSpec((None, gp, dc), lambda i: (i, 0, 0)),
                  pl.BlockSpec((None, gp, dc), lambda i: (i, 0, 0)),
                  _const_spec((1, dc))],
        out_specs=pl.BlockSpec((None, ch, dc), lambda i: (i, 0, 0)),
        out_shape=jax.ShapeDtypeStruct((L, ch, dc), F32),
        compiler_params=_cparams(("parallel",)),
    )(up, kbd, stre, stim, m2re, m2im, d_skip.astype(F32).reshape(1, dc))
    return yp, fre.reshape(nb, G, P), fim.reshape(nb, G, P)


def _diff_finish(o1, o2, lam, gs):
    o = o1 - lam * o2
    return o * lax.rsqrt(jnp.mean(o * o, axis=-1, keepdims=True) + RMS_EPS) * gs


def _attn_prompt_kernel(q_ref, k_ref, v_ref, lam_ref, gs_ref, o_ref, q2_ref, m_ref, acc_ref, *, tq, rb, unroll):
    i = pl.program_id(2)
    r2 = 2 * tq
    q = q_ref[...]
    lane = lax.broadcasted_iota(jnp.int32, (tq, ATT_VD), 1)
    zero = jnp.zeros_like(q)
    q2_ref[0:tq, :] = jnp.where(lane < ATT_HD, q, zero)
    q2_ref[tq:r2, :] = jnp.where(lane >= ATT_HD, q, zero)
    m_ref[...] = jnp.full((r2, LANES), NEG_INF, F32)
    acc_ref[...] = jnp.zeros((r2, 2 * ATT_VD), F32)
    n_rep = tq // LANES

    def step(j, diagonal):
        off = pl.multiple_of(j * tq, tq)
        kt = k_ref[pl.ds(off, tq), :]
        vt = v_ref[pl.ds(off, tq), :]
        for r in range(r2 // rb):
            rows = slice(r * rb, (r + 1) * rb)
            s = _mm_nt(q2_ref[rows, :], kt)
            if diagonal:
                row = (lax.broadcasted_iota(jnp.int32, (rb, tq), 0) + r * rb) % tq
                col = lax.broadcasted_iota(jnp.int32, (rb, tq), 1)
                s = jnp.where(col <= row, s, NEG_INF)
            m_prev = m_ref[rows, :]
            m_new = jnp.maximum(m_prev, jnp.max(s, axis=-1, keepdims=True))
            alpha = jnp.exp2(m_prev - m_new)
            p = jnp.exp2(s - jnp.concatenate([m_new] * n_rep, axis=1))
            acc_ref[rows, :] = jnp.concatenate([alpha, alpha], axis=1) * acc_ref[rows, :] + _mm(p, vt)
            m_ref[rows, :] = m_new

    def group(g, carry):
        for u in range(unroll):
            step(g * unroll + u, False)
        return carry

    def single(j, carry):
        step(j, False)
        return carry

    n_grp = i // unroll
    lax.fori_loop(0, n_grp, group, 0)
    lax.fori_loop(n_grp * unroll, i, single, 0)
    step(i, True)
    acc = acc_ref[...]
    o = acc[:, 0:ATT_VD] / acc[:, ATT_VD:]
    o_ref[...] = _diff_finish(o[0:tq], o[tq:r2], lam_ref[...], gs_ref[...])


def _attn_prompt(qb, kb, vx, lamv, gsv, *, nb, t, tq):
    n = nb * t
    nq = t // tq
    return pl.pallas_call(
        functools.partial(_attn_prompt_kernel, tq=tq, rb=min(128, tq), unroll=4),
        grid=(nb, ATT_HEADS, nq),
        in_specs=[pl.BlockSpec((tq, ATT_VD), lambda b, h, i: (b * nq + i, h)),
                  pl.BlockSpec((t, ATT_VD), lambda b, h, i: (b, h)),
                  pl.BlockSpec((t, 2 * ATT_VD), lambda b, h, i: (b, h)),
                  _const_spec((1, ATT_VD)), _const_spec((1, ATT_VD))],
        out_specs=pl.BlockSpec((tq, ATT_VD), lambda b, h, i: (b * nq + i, h)),
        out_shape=jax.ShapeDtypeStruct((n, D_ATT), F32),
        scratch_shapes=[pltpu.VMEM((2 * tq, ATT_VD), BF16), pltpu.VMEM((2 * tq, LANES), F32),
                        pltpu.VMEM((2 * tq, 2 * ATT_VD), F32)],
        compiler_params=_cparams(("parallel", "parallel", "arbitrary")),
    )(qb, kb, vx, lamv, gsv)


def _attn_sample_kernel(pt_ref, q_ref, kn_ref, vn_ref, lam_ref, gs_ref, *rest, tn, pg):
    k_refs = rest[0:pg]
    v_refs = rest[pg:2 * pg]
    o_ref = rest[2 * pg]
    qbd_ref, m_ref, l_ref, acc_ref = rest[2 * pg + 1:]
    p_idx = pl.program_id(1)
    rows_h = 2 * tn
    n_rows = ATT_HEADS * rows_h

    def update(s, v_pages):
        n_rep = s.shape[1] // LANES
        m_prev = m_ref[...]
        m_new = jnp.maximum(m_prev, jnp.max(s, axis=-1, keepdims=True))
        p = jnp.exp(s - jnp.concatenate([m_new] * n_rep, axis=1))
        alpha = jnp.exp(m_prev - m_new)
        l_ref[...] = alpha * l_ref[...] + jnp.sum(p, axis=-1, keepdims=True)
        for h in range(ATT_HEADS):
            rows = slice(h * rows_h, (h + 1) * rows_h)
            v_h = [r[pl.ds(h, PAGE_SIZE, stride=ATT_HEADS), :].astype(BF16) for r in v_pages]
            v_h = v_h[0] if len(v_h) == 1 else jnp.concatenate(v_h, axis=0)
            acc_ref[rows, :] = alpha[rows] * acc_ref[rows, :] + _mm(p[rows], v_h)
        m_ref[...] = m_new

    @pl.when(p_idx == 0)
    def _():
        q = q_ref[...]
        lane_grp = lax.broadcasted_iota(jnp.int32, (tn, D_ATT), 1) // ATT_HD
        pieces = [jnp.where(lane_grp == g, q, 0.0) for g in range(2 * ATT_HEADS)]
        qbd_ref[...] = jnp.concatenate(pieces, axis=0).astype(BF16)
        m_ref[...] = jnp.full((n_rows, LANES), NEG_INF, F32)
        l_ref[...] = jnp.zeros((n_rows, LANES), F32)
        acc_ref[...] = jnp.zeros((n_rows, ATT_VD), F32)
        s = _mm(qbd_ref[...], kn_ref[...])
        t_of_row = lax.broadcasted_iota(jnp.int32, (n_rows, PAGE_SIZE), 0) % tn
        col = lax.broadcasted_iota(jnp.int32, (n_rows, PAGE_SIZE), 1)
        update(jnp.where(col <= t_of_row, s, NEG_INF), [vn_ref])

    kcat = jnp.concatenate([r[...].astype(BF16) for r in k_refs], axis=1)
    update(_mm(qbd_ref[...], kcat), list(v_refs))

    @pl.when(p_idx == pl.num_programs(1) - 1)
    def _():
        o = acc_ref[...] / l_ref[...]
        outs = []
        for h in range(ATT_HEADS):
            o1 = o[h * rows_h:h * rows_h + tn]
            o2 = o[h * rows_h + tn:(h + 1) * rows_h]
            outs.append(_diff_finish(o1, o2, lam_ref[...], gs_ref[...]))
        o_ref[...] = jnp.concatenate(outs, axis=1)


def _attn_sample(q, k_new, v_new, cache_k, cache_v, page_table, layer, lamv, gsv, *, nb, tn, pg):
    n_pages = page_table.shape[1]
    depth, n_pool = cache_k.shape[0], cache_k.shape[1]
    rows_v = PAGE_SIZE * ATT_HEADS
    ck = jnp.transpose(cache_k, (0, 1, 3, 4, 5, 2)).reshape(depth, n_pool, D_ATT, PAGE_SIZE)
    cv = cache_v.reshape(depth, n_pool, rows_v, ATT_VD)
    kn = jnp.pad(k_new.reshape(nb, tn, D_ATT).transpose(0, 2, 1), ((0, 0), (0, 0), (0, PAGE_SIZE - tn)))
    vn = jnp.pad(v_new.reshape(nb, tn * ATT_HEADS, ATT_VD), ((0, 0), (0, rows_v - tn * ATT_HEADS), (0, 0)))
    n_rows = 2 * ATT_HEADS * tn

    def page_spec(r, rows, cols):
        return pl.BlockSpec((None, None, rows, cols), lambda b, p, pt: (layer, pt[b, p * pg + r], 0, 0))

    grid_spec = pltpu.PrefetchScalarGridSpec(
        num_scalar_prefetch=1,
        grid=(nb, n_pages // pg),
        in_specs=[pl.BlockSpec((tn, D_ATT), lambda b, p, pt: (b, 0)),
                  pl.BlockSpec((None, D_ATT, PAGE_SIZE), lambda b, p, pt: (b, 0, 0)),
                  pl.BlockSpec((None, rows_v, ATT_VD), lambda b, p, pt: (b, 0, 0)),
                  pl.BlockSpec((1, ATT_VD), lambda b, p, pt: (0, 0)),
                  pl.BlockSpec((1, ATT_VD), lambda b, p, pt: (0, 0))]
                 + [page_spec(r, D_ATT, PAGE_SIZE) for r in range(pg)]
                 + [page_spec(r, rows_v, ATT_VD) for r in range(pg)],
        out_specs=pl.BlockSpec((tn, D_ATT), lambda b, p, pt: (b, 0)),
        scratch_shapes=[pltpu.VMEM((n_rows, D_ATT), BF16), pltpu.VMEM((n_rows, LANES), F32),
                        pltpu.VMEM((n_rows, LANES), F32), pltpu.VMEM((n_rows, ATT_VD), F32)],
    )
    return pl.pallas_call(
        functools.partial(_attn_sample_kernel, tn=tn, pg=pg),
        grid_spec=grid_spec,
        out_shape=jax.ShapeDtypeStruct((nb * tn, D_ATT), F32),
        compiler_params=_cparams(("parallel", "arbitrary")),
    )(page_table, q, kn, vn, lamv, gsv, *([ck] * pg), *([cv] * pg))


def _merge_kernel(x_ref, ya_ref, yb_ref, yc_ref, yd_ref, wglu_ref, wgate_ref, bgate_ref,
                  wa_ref, wb_ref, wc_ref, wd_ref, wo_ref, g_ref, b_ref, o_ref, ya_rows_ref, *, alpha):
    x = x_ref[...]
    xb = x.astype(BF16)
    hg = _mm(_from_planes(ya_ref, ya_rows_ref), wglu_ref[...])
    ya = hg[:, :D_S5] * _sigmoid(hg[:, D_S5:])
    merged = None
    for idx, (y, w_ref) in enumerate(((ya, wa_ref), (yb_ref[...], wb_ref), (yc_ref[...], wc_ref),
                                      (yd_ref[...], wd_ref))):
        cols = slice(idx * D_MODEL, (idx + 1) * D_MODEL)
        gate = _sigmoid(_mm(xb, wgate_ref[:, cols]) + bgate_ref[:, cols])
        term = gate * _mm(y, w_ref[...])
        merged = term if merged is None else merged + term
    out = _mm(merged, wo_ref[...])
    o_ref[...] = _ln(alpha * x + out, g_ref[...], b_ref[...])


def _merge(x, ya, yb, yc, yd, w, *, tm, alpha):
    n = x.shape[0]
    chunk = ya.shape[0]
    row = lambda width: pl.BlockSpec((tm, width), lambda i: (i, 0))
    return pl.pallas_call(
        functools.partial(_merge_kernel, alpha=alpha),
        grid=(n // tm,),
        in_specs=[row(D_MODEL), pl.BlockSpec((chunk, tm // chunk, D_S5), lambda i: (0, i, 0)),
                  row(D_SGU), row(D_CONV), row(D_ATT),
                  _const_spec((D_S5, 2 * D_S5)), _const_spec((D_MODEL, N_BRANCH * D_MODEL)),
                  _const_spec((1, N_BRANCH * D_MODEL)),
                  _const_spec((D_S5, D_MODEL)), _const_spec((D_SGU, D_MODEL)), _const_spec((D_CONV, D_MODEL)),
                  _const_spec((D_ATT, D_MODEL)), _const_spec((D_MODEL, D_MODEL)),
                  _const_spec((1, D_MODEL)), _const_spec((1, D_MODEL))],
        out_specs=row(D_MODEL),
        out_shape=jax.ShapeDtypeStruct((n, D_MODEL), F32),
        scratch_shapes=[pltpu.VMEM((D_S5 // LANES, tm, LANES), F32)],
        compiler_params=_cparams(("parallel",)),
    )(x, ya, yb, yc, yd, w['w_glu'], w['w_gate'], w['b_gate'], w['w_br_a'], w['w_br_b'], w['w_br_c'],
      w['w_br_d'], w['w_o'], w['ln1_g'], w['ln1_b'])


def _ffn_kernel(x_ref, wg_ref, wu_ref, wd_ref, g_ref, b_ref, o_ref, *, alpha):
    x = x_ref[...]
    xb = x.astype(BF16)
    hid = _silu(_mm(xb, wg_ref[...])) * _mm(xb, wu_ref[...])
    o_ref[...] = _ln(alpha * x + _mm(hid, wd_ref[...]), g_ref[...], b_ref[...])


def _ffn(x, wg, wu, wd, g, b, *, tm, alpha):
    n = x.shape[0]
    dff = wg.shape[1]
    row = pl.BlockSpec((tm, D_MODEL), lambda i: (i, 0))
    return pl.pallas_call(
        functools.partial(_ffn_kernel, alpha=alpha),
        grid=(n // tm,),
        in_specs=[row, _const_spec((D_MODEL, dff)), _const_spec((D_MODEL, dff)), _const_spec((dff, D_MODEL)),
                  _const_spec((1, D_MODEL)), _const_spec((1, D_MODEL))],
        out_specs=row,
        out_shape=jax.ShapeDtypeStruct((n, D_MODEL), F32),
        compiler_params=_cparams(("parallel",)),
    )(x, wg, wu, wd, g, b)


def _split_bf16(a):
    hi = a.astype(BF16)
    lo = (a - hi.astype(F32)).astype(BF16)
    return hi, lo


SEL_I1, SEL_I2, SEL_W1, SEL_W2 = 0, 1, 2, 3
MOE_TM = 512


def _moe_route_kernel(x_ref, wr_hi_ref, wr_lo_ref, br_ref, sel_ref):
    x = x_ref[...]
    tm = x.shape[0]
    lane = lax.broadcasted_iota(jnp.int32, (tm, LANES), 1)
    x_hi, x_lo = _split_bf16(x)
    logits = (jnp.dot(x_hi, wr_hi_ref[...], preferred_element_type=F32)
              + jnp.dot(x_hi, wr_lo_ref[...], preferred_element_type=F32)
              + jnp.dot(x_lo, wr_hi_ref[...], preferred_element_type=F32)) + br_ref[...]
    logits = jnp.where(lane < N_EXPERTS, logits, -jnp.inf)
    lane_f = lane.astype(F32)
    v1 = jnp.max(logits, axis=-1, keepdims=True)
    i1 = jnp.min(jnp.where(logits == v1, lane_f, float(LANES)), axis=-1, keepdims=True)
    rest = jnp.where(lane_f == i1, -jnp.inf, logits)
    v2 = jnp.max(rest, axis=-1, keepdims=True)
    i2 = jnp.min(jnp.where(rest == v2, lane_f, float(LANES)), axis=-1, keepdims=True)
    e2 = jnp.exp(v2 - v1)
    w1 = 1.0 / (1.0 + e2)
    w2 = e2 / (1.0 + e2)
    sel_ref[...] = (jnp.where(lane == SEL_I1, i1, 0.0) + jnp.where(lane == SEL_I2, i2, 0.0)
                    + jnp.where(lane == SEL_W1, w1, 0.0) + jnp.where(lane == SEL_W2, w2, 0.0))


def _moe_expert_kernel(item_e, item_tile, item_flag, starts, idx_ref, idx_next_ref, x_hbm, wg_ref, wu_ref, wd_ref,
                       y_hbm, xbuf, obuf, gsem, ssem, *, n_tiles, n_tok):
    i = pl.program_id(0)
    tm = xbuf.shape[1]
    flag = item_flag[i]
    valid = (flag & 1) == 1
    first = (flag & 2) == 2
    last = (flag & 4) == 4
    t = item_tile[i]
    e = item_e[i]
    slot = t % 2

    def gather_rows(idx, dst_slot):
        def body(r, carry):
            a = idx[0, r]
            tok = jnp.where(a >= n_tok, a - n_tok, a)
            pltpu.make_async_copy(x_hbm.at[pl.ds(tok, 1)], xbuf.at[dst_slot, pl.ds(r, 1)], gsem.at[dst_slot]).start()
            return carry
        lax.fori_loop(0, tm, body, 0, unroll=8)

    def wait_rows(buf, sem, s):
        pltpu.make_async_copy(buf.at[s], buf.at[s], sem.at[s]).wait()

    @pl.when(valid & first)
    def _():
        @pl.when(i == 0)
        def _():
            gather_rows(idx_ref, 0)

        @pl.when(t + 1 < n_tiles)
        def _():
            gather_rows(idx_next_ref, 1 - slot)

        wait_rows(xbuf, gsem, slot)

        @pl.when(t >= 2)
        def _():
            wait_rows(obuf, ssem, slot)

        obuf[slot] = jnp.zeros((tm, D_MODEL), F32)

    @pl.when(valid)
    def _():
        xb = xbuf[slot].astype(BF16)
        hid = _silu(_mm(xb, wg_ref[...])) * _mm(xb, wu_ref[...])
        y = _mm(hid, wd_ref[...])
        pos = t * tm + lax.broadcasted_iota(jnp.int32, (tm, 1), 0)
        mine = (pos >= starts[e]) & (pos < starts[e + 1])
        obuf[slot] += jnp.where(mine, y, 0.0)

    @pl.when(valid & last)
    def _():
        def body(r, carry):
            dst = idx_ref[0, r]
            pltpu.make_async_copy(obuf.at[slot, pl.ds(r, 1)], y_hbm.at[pl.ds(dst, 1)], ssem.at[slot]).start()
            return carry
        lax.fori_loop(0, tm, body, 0, unroll=8)

    @pl.when(i == pl.num_programs(0) - 1)
    def _():
        for tt in range(max(0, n_tiles - 2), n_tiles):
            wait_rows(obuf, ssem, tt % 2)


def _moe_combine_kernel(x_ref, y1_ref, y2_ref, sel_ref, g_ref, b_ref, o_ref, *, alpha):
    sel = sel_ref[...]
    lane = lax.broadcasted_iota(jnp.int32, sel.shape, 1)
    w1 = jnp.sum(jnp.where(lane == SEL_W1, sel, 0.0), axis=-1, keepdims=True)
    w2 = jnp.sum(jnp.where(lane == SEL_W2, sel, 0.0), axis=-1, keepdims=True)
    y = w1 * y1_ref[...] + w2 * y2_ref[...]
    o_ref[...] = _ln(alpha * x_ref[...] + y, g_ref[...], b_ref[...])


def _moe(x, wr_hi, wr_lo, br, wg, wu, wd, g, b, *, tm, alpha):
    n = x.shape[0]
    dfe = wg.shape[2]
    n_asg = TOP_K * n
    tmx = min(MOE_TM, n_asg)
    assert n_asg % tmx == 0
    n_tiles = n_asg // tmx
    n_items = n_tiles + N_EXPERTS - 1
    row = pl.BlockSpec((tm, D_MODEL), lambda i: (i, 0))

    sel = pl.pallas_call(
        _moe_route_kernel,
        grid=(n // tm,),
        in_specs=[row, _const_spec((D_MODEL, LANES)), _const_spec((D_MODEL, LANES)), _const_spec((1, LANES))],
        out_specs=pl.BlockSpec((tm, LANES), lambda i: (i, 0)),
        out_shape=jax.ShapeDtypeStruct((n, LANES), F32),
        compiler_params=_cparams(("parallel",)),
    )(x, wr_hi, wr_lo, br)

    experts = sel[:, SEL_I1:SEL_I2 + 1].astype(jnp.int32).T.reshape(n_asg)
    keys = jnp.sort(experts * n_asg + jnp.arange(n_asg, dtype=jnp.int32))
    idx = (keys % n_asg).reshape(n_tiles, 1, tmx)
    starts = jnp.searchsorted(keys, jnp.arange(N_EXPERTS + 1, dtype=jnp.int32) * n_asg).astype(jnp.int32)
    counts = starts[1:] - starts[:-1]
    first_tile = starts[:-1] // tmx
    last_tile = (starts[1:] - 1) // tmx
    per_e = jnp.where(counts > 0, last_tile - first_tile + 1, 0)
    ends = jnp.cumsum(per_e)
    total = ends[-1]
    it = jnp.arange(n_items, dtype=jnp.int32)
    e_raw = jnp.searchsorted(ends, it, side='right').astype(jnp.int32)
    e_last = jnp.max(jnp.where(counts > 0, jnp.arange(N_EXPERTS, dtype=jnp.int32), 0))
    valid = it < total
    item_e = jnp.where(valid, jnp.minimum(e_raw, N_EXPERTS - 1), e_last).astype(jnp.int32)
    item_tile = jnp.where(valid, first_tile[item_e] + it - (ends[item_e] - per_e[item_e]), n_tiles - 1)
    item_tile = item_tile.astype(jnp.int32)
    prev_tile = jnp.concatenate([jnp.full((1,), -1, jnp.int32), item_tile[:-1]])
    next_tile = jnp.concatenate([item_tile[1:], jnp.full((1,), -1, jnp.int32)])
    next_valid = jnp.concatenate([valid[1:], jnp.zeros((1,), bool)])
    is_first = item_tile != prev_tile
    is_last = (item_tile != next_tile) | ~next_valid
    item_flag = (valid.astype(jnp.int32) + 2 * is_first.astype(jnp.int32) + 4 * is_last.astype(jnp.int32))

    smem_idx = lambda f: pl.BlockSpec((None, 1, tmx), f, memory_space=pltpu.SMEM)
    grid_spec = pltpu.PrefetchScalarGridSpec(
        num_scalar_prefetch=4,
        grid=(n_items,),
        in_specs=[smem_idx(lambda i, ie, itl, ifl, st: (itl[i], 0, 0)),
                  smem_idx(lambda i, ie, itl, ifl, st: (jnp.minimum(itl[i] + 1, n_tiles - 1), 0, 0)),
                  pl.BlockSpec(memory_space=pl.ANY),
                  pl.BlockSpec((None, D_MODEL, dfe), lambda i, ie, itl, ifl, st: (ie[i], 0, 0)),
                  pl.BlockSpec((None, D_MODEL, dfe), lambda i, ie, itl, ifl, st: (ie[i], 0, 0)),
                  pl.BlockSpec((None, dfe, D_MODEL), lambda i, ie, itl, ifl, st: (ie[i], 0, 0))],
        out_specs=pl.BlockSpec(memory_space=pl.ANY),
        scratch_shapes=[pltpu.VMEM((2, tmx, D_MODEL), F32), pltpu.VMEM((2, tmx, D_MODEL), F32),
                        pltpu.SemaphoreType.DMA((2,)), pltpu.SemaphoreType.DMA((2,))],
    )
    y = pl.pallas_call(
        functools.partial(_moe_expert_kernel, n_tiles=n_tiles, n_tok=n),
        grid_spec=grid_spec,
        out_shape=jax.ShapeDtypeStruct((n_asg, D_MODEL), F32),
        compiler_params=_cparams(("arbitrary",)),
    )(item_e, item_tile, item_flag, starts, idx, idx, x, wg, wu, wd)

    return pl.pallas_call(
        functools.partial(_moe_combine_kernel, alpha=alpha),
        grid=(n // tm,),
        in_specs=[row, row, pl.BlockSpec((tm, D_MODEL), lambda i: (i + n // tm, 0)),
                  pl.BlockSpec((tm, LANES), lambda i: (i, 0)), _const_spec((1, D_MODEL)), _const_spec((1, D_MODEL))],
        out_specs=row,
        out_shape=jax.ShapeDtypeStruct((n, D_MODEL), F32),
        compiler_params=_cparams(("parallel",)),
    )(x, y, y, sel, g, b)


def _rope_table(pos):
    half = ROT_DIM // 2
    inv = ROPE_THETA ** (-jnp.arange(half, dtype=F32) * (2.0 / ROT_DIM))
    ang = pos.astype(F32)[:, None] * inv[None, :]
    cos, sin = jnp.cos(ang), jnp.sin(ang)
    n = pos.shape[0]
    one = jnp.ones((n, ATT_HD - ROT_DIM), F32)
    zero = jnp.zeros((n, ATT_HD - ROT_DIM), F32)
    z8 = jnp.zeros((n, half), F32)
    c = jnp.concatenate([cos, cos, one], axis=1)
    s_hi = jnp.concatenate([-sin, z8, zero], axis=1)
    s_lo = jnp.concatenate([z8, sin, zero], axis=1)
    rep = LANES // ATT_HD
    return jnp.concatenate([jnp.tile(c, (1, rep)), jnp.tile(s_hi, (1, rep)), jnp.tile(s_lo, (1, rep))], axis=1)


def _sgu_tables(w_s, b_s, tn):
    tril = jnp.tril(jnp.ones((CHUNK, CHUNK), dtype=bool))
    ws = jnp.where(tril, w_s, 0)
    hd = D_SGU // SGU_HEADS
    bias_p = jnp.repeat(b_s.T, hd, axis=1)
    rep = CHUNK // tn
    eye = jnp.eye(rep, dtype=w_s.dtype)
    ws_s = jnp.einsum('ab,gts->gatbs', eye, ws[:, :tn, :tn]).reshape(SGU_HEADS, CHUNK, CHUNK)
    bias_s = jnp.tile(bias_p[:tn], (rep, 1))
    return ws.astype(BF16), bias_p.astype(F32), ws_s.astype(BF16), bias_s.astype(F32)


def kernel(x_prompt, x_sample, cache_k, cache_v, state_ssm_re, state_ssm_im, state_conv, page_table, ln_in_g, ln_in_b, w_in, w_gate, b_gate, s5_lam_re, s5_lam_im, s5_log_dt, s5_b_re, s5_b_im, s5_c_re, s5_c_im, s5_d, s5_w_glu, sgu_ln_g, sgu_ln_b, sgu_w_s, sgu_b_s, conv_w, conv_b, conv_ln_g, conv_ln_b, att_lam_q1, att_lam_k1, att_lam_q2, att_lam_k2, att_subln_g, w_br_a, w_br_b, w_br_c, w_br_d, w_o, ln1_g, ln1_b, ffn_wg, ffn_wu, ffn_wd, moe_router, moe_router_b, moe_wg, moe_wu, moe_wd, ln2_g, ln2_b):
    bp, tp, d = x_prompt.shape
    bs, ts, _ = x_sample.shape
    depth = w_in.shape[0]
    n_pages = page_table.shape[1]
    past = n_pages * PAGE_SIZE
    alpha = (2 * depth) ** 0.25
    n_p, n_s = bp * tp, bs * ts
    tm_p = min(512, tp)
    tq = min(512, tp)
    l_p = 16
    pg = min(8, n_pages)
    assert tp % tm_p == 0 and tp % CHUNK == 0 and tp % tq == 0 and tp % l_p == 0 and tp >= CONV_W - 1
    assert n_s % CHUNK == 0 and CHUNK % ts == 0 and ts % SUBLANES == 0 and n_pages % pg == 0

    row1 = lambda a: a.reshape(1, -1).astype(F32)
    tab_p = _rope_table(jnp.arange(tp, dtype=jnp.int32))
    tab_s = jnp.tile(_rope_table(past + jnp.arange(ts, dtype=jnp.int32)), (bs, 1))
    zero_state = jnp.zeros((bp, S5_GROUPS, S5_STATE), F32)

    xp = _layernorm(x_prompt.reshape(n_p, d), ln_in_g, ln_in_b, tm_p)
    xs = _layernorm(x_sample.reshape(n_s, d), ln_in_g, ln_in_b, n_s)

    outs = [[] for _ in range(11)]
    for l in range(depth):
        lam_init = 0.8 - 0.6 * math.exp(-0.3 * l)
        lam = (jnp.exp(jnp.sum(att_lam_q1[l].astype(F32) * att_lam_k1[l].astype(F32)))
               - jnp.exp(jnp.sum(att_lam_q2[l].astype(F32) * att_lam_k2[l].astype(F32))) + lam_init)
        lamv = jnp.full((1, ATT_VD), lam, F32)
        gsv = row1(att_subln_g[l]) * (1.0 - lam_init)
        w_in_b = w_in[l].astype(BF16)
        ws_p, sb_p, ws_s, sb_s = _sgu_tables(sgu_w_s[l], sgu_b_s[l], ts)
        conv_args = (conv_w[l].astype(F32), row1(conv_b[l]), row1(conv_ln_g[l]), row1(conv_ln_b[l]))
        sgu_ln = (row1(sgu_ln_g[l]), row1(sgu_ln_b[l]))
        s5_par = (s5_lam_re[l], s5_lam_im[l], s5_log_dt[l], s5_b_re[l], s5_b_im[l], s5_c_re[l], s5_c_im[l])
        mw = {'w_glu': s5_w_glu[l].astype(BF16), 'w_gate': w_gate[l].astype(BF16), 'b_gate': row1(b_gate[l]),
              'w_br_a': w_br_a[l].astype(BF16), 'w_br_b': w_br_b[l].astype(BF16),
              'w_br_c': w_br_c[l].astype(BF16), 'w_br_d': w_br_d[l].astype(BF16), 'w_o': w_o[l].astype(BF16),
              'ln1_g': row1(ln1_g[l]), 'ln1_b': row1(ln1_b[l])}
        ln2 = (row1(ln2_g[l]), row1(ln2_b[l]))
        i = l // 2
        if l % 2 == 0:
            fw = (ffn_wg[i].astype(BF16), ffn_wu[i].astype(BF16), ffn_wd[i].astype(BF16))
            ffn = lambda x, tm: _ffn(x, *fw, *ln2, tm=tm, alpha=alpha)
        else:
            wr = jnp.pad(moe_router[i].astype(F32), ((0, 0), (0, LANES - N_EXPERTS)))
            wr_hi, wr_lo = _split_bf16(wr)
            br = jnp.pad(moe_router_b[i].astype(F32), (0, LANES - N_EXPERTS)).reshape(1, LANES)
            fw = (moe_wg[i].astype(BF16), moe_wu[i].astype(BF16), moe_wd[i].astype(BF16))
            ffn = lambda x, tm: _moe(x, wr_hi, wr_lo, br, *fw, *ln2, tm=tm, alpha=alpha)

        ua, yb, yc, glu, qb, k, v, kb, vx = _proj_prompt(xp, w_in_b, tab_p, *sgu_ln, ws_p, sb_p, *conv_args,
                                                         seq=tp, tm=tm_p, chunk=l_p)
        ya, sr_p, si_p = _s5_branch(ua, zero_state, zero_state, _s5_mats(*s5_par, l_p), s5_d[l], nb=bp, t=tp)
        yd = _attn_prompt(qb, kb, vx, lamv, gsv, nb=bp, t=tp, tq=tq)
        xp = _merge(xp, ya, yb, yc, yd, mw, tm=tm_p, alpha=alpha)
        xp = ffn(xp, tm_p)
        outs[0].append(k.reshape(bp, tp, ATT_HEADS, 2, ATT_HD))
        outs[1].append(v.reshape(bp, tp, ATT_HEADS, ATT_VD))
        outs[2].append(sr_p)
        outs[3].append(si_p)
        outs[4].append(glu.reshape(bp, tp, D_CONV)[:, tp - (CONV_W - 1):])

        pfx = jnp.pad(state_conv[l].astype(F32), ((0, 0), (HALO_OFF, 0), (0, 0)))
        ua, yb, yc, glu, q, k, v, vn = _proj_sample(xs, w_in_b, tab_s, *sgu_ln, ws_s, sb_s, *conv_args, pfx,
                                                    nb=bs, tn=ts)
        ya, sr_s, si_s = _s5_branch(ua, state_ssm_re[l], state_ssm_im[l], _s5_mats(*s5_par, ts), s5_d[l],
                                    nb=bs, t=ts)
        yd = _attn_sample(q, k, v, cache_k, cache_v, page_table, l, lamv, gsv, nb=bs, tn=ts, pg=pg)
        xs = _merge(xs, ya, yb, yc, yd, mw, tm=n_s, alpha=alpha)
        xs = ffn(xs, n_s)
        outs[5].append(k.reshape(bs, ts, ATT_HEADS, 2, ATT_HD))
        outs[6].append(v.reshape(bs, ts, ATT_HEADS, ATT_VD))
        outs[7].append(sr_s)
        outs[8].append(si_s)
        full = jnp.concatenate([state_conv[l].astype(F32), glu.reshape(bs, ts, D_CONV)], axis=1)
        outs[9].append(full[:, ts:])
        outs[10].append(vn.reshape(bs, ts, D_SGU))

    return (xp.reshape(bp, tp, d), xs.reshape(bs, ts, d), *[jnp.stack(o) for o in outs])
```

```python
import functools
import math

import jax
import jax.numpy as jnp
from jax import lax
from jax.experimental import pallas as pl
from jax.experimental.pallas import tpu as pltpu

F32 = jnp.float32
BF16 = jnp.bfloat16

D_MODEL = 1024
PAGE_SIZE = 128
S5_GROUP = 16
S5_GROUPS = 16
D_S5 = 256
S5_STATE = 64
D_SGU = 256
SGU_HEADS = 4
CHUNK = 128
D_CONV = 256
CONV_W = 31
ATT_HEADS = 4
ATT_HD = 64
ATT_VD = 128
D_ATT = 512
ROT_DIM = 16
ROPE_THETA = 500000.0
ATT_SCALE = ATT_HD ** -0.5
LOG2E = 1.4426950408889634
NEG_INF = -1e30
D_IN = 2816
N_BRANCH = 4
N_EXPERTS = 8
TOP_K = 2
LN_EPS = 1e-5
RMS_EPS = 1e-5

LANES = 128
SUBLANES = 8
HALO = 32
HALO_OFF = HALO - (CONV_W - 1)
VMEM_LIMIT = 56 * 1024 * 1024


def _cparams(sem):
    return pltpu.CompilerParams(dimension_semantics=sem, vmem_limit_bytes=VMEM_LIMIT)


def _const_spec(shape):
    nd = len(shape)
    return pl.BlockSpec(shape, lambda *_: (0,) * nd)


def _ln(x, g, b):
    mu = jnp.mean(x, axis=-1, keepdims=True)
    xc = x - mu
    var = jnp.mean(xc * xc, axis=-1, keepdims=True)
    return xc * lax.rsqrt(var + LN_EPS) * g + b


def _sigmoid(x):
    return 1.0 / (1.0 + jnp.exp(-x))


def _silu(x):
    return x * _sigmoid(x)


def _gelu(x):
    return jax.nn.gelu(x, approximate=True)


def _mm(a, b):
    return jnp.dot(a.astype(BF16), b.astype(BF16), preferred_element_type=F32)


def _mm_nt(a, b):
    return lax.dot_general(a.astype(BF16), b.astype(BF16), (((1,), (1,)), ((), ())),
                           preferred_element_type=F32)


def _rope(x, tab):
    n_rep = x.shape[1] // LANES
    c = jnp.concatenate([tab[:, 0:LANES]] * n_rep, axis=1)
    s_hi = jnp.concatenate([tab[:, LANES:2 * LANES]] * n_rep, axis=1)
    s_lo = jnp.concatenate([tab[:, 2 * LANES:3 * LANES]] * n_rep, axis=1)
    half = ROT_DIM // 2
    x_up = pltpu.roll(x, x.shape[1] - half, 1)
    x_dn = pltpu.roll(x, half, 1)
    return x * c + x_up * s_hi + x_dn * s_lo


def _ln_kernel(x_ref, g_ref, b_ref, o_ref):
    o_ref[...] = _ln(x_ref[...], g_ref[...], b_ref[...])


def _layernorm(x, g, b, tm):
    n, d = x.shape
    return pl.pallas_call(
        _ln_kernel,
        grid=(n // tm,),
        in_specs=[pl.BlockSpec((tm, d), lambda i: (i, 0)), _const_spec((1, d)), _const_spec((1, d))],
        out_specs=pl.BlockSpec((tm, d), lambda i: (i, 0)),
        out_shape=jax.ShapeDtypeStruct((n, d), F32),
        compiler_params=_cparams(("parallel",)),
    )(x, g.reshape(1, d), b.reshape(1, d))


def _proj_split(x, w_ref, tab, sgu_g, sgu_b, q_scale):
    h = _mm(x, w_ref[...])
    ua = h[:, 0:256]
    gub = _gelu(h[:, 256:512])
    vn = _ln(_gelu(h[:, 512:768]), sgu_g, sgu_b)
    glu = h[:, 768:1024] * _sigmoid(h[:, 1024:1280])
    q = _rope(h[:, 1280:1792], tab) * q_scale
    k = _rope(h[:, 1792:2304], tab)
    v = h[:, 2304:2816]
    return ua, gub, vn, glu, q, k, v


def _sgu(gub, vn, ws_ref, sb_ref):
    tm = vn.shape[0]
    head_of_lane = lax.broadcasted_iota(jnp.int32, (CHUNK, D_SGU), 1) // (D_SGU // SGU_HEADS)
    outs = []
    for c in range(tm // CHUNK):
        vc = vn[c * CHUNK:(c + 1) * CHUNK]
        z = sb_ref[...]
        for g in range(SGU_HEADS):
            z = z + _mm(ws_ref[g], jnp.where(head_of_lane == g, vc, 0.0))
        outs.append(gub[c * CHUNK:(c + 1) * CHUNK] * z)
    return outs[0] if len(outs) == 1 else jnp.concatenate(outs, axis=0)


def _conv_post(acc, cb, g, b):
    return _silu(_ln(acc + cb, g, b))


def _to_planes(rows, tmp_ref, planes_ref):
    n_planes, per_plane = planes_ref.shape[0], planes_ref.shape[1]
    for h in range(tmp_ref.shape[0]):
        tmp_ref[h] = rows[:, h * LANES:(h + 1) * LANES]
    for i in range(n_planes):
        for h in range(tmp_ref.shape[0]):
            planes_ref[i, :, h * LANES:(h + 1) * LANES] = tmp_ref[h, pl.ds(i, per_plane, stride=n_planes), :]


def _from_planes(planes_ref, tmp_ref):
    n_planes, per_plane = planes_ref.shape[0], planes_ref.shape[1]
    for i in range(n_planes):
        for h in range(tmp_ref.shape[0]):
            tmp_ref[h, pl.ds(i, per_plane, stride=n_planes), :] = planes_ref[i, :, h * LANES:(h + 1) * LANES]
    return jnp.concatenate([tmp_ref[h] for h in range(tmp_ref.shape[0])], axis=1)


def _proj_prompt_kernel(x_ref, w_ref, tab_ref, sg_ref, sbn_ref, ws_ref, sb_ref, cw_ref, cb_ref, cg_ref, cbn_ref,
                        up_ref, yb_ref, yc_ref, glu_ref, q_ref, k_ref, v_ref, kb_ref, vx_ref,
                        ext_ref, ua_ref, *, tm, tiles_per_seq, rb):
    i = pl.program_id(0)
    ua, gub, vn, glu, q, k, v = _proj_split(x_ref[...], w_ref, tab_ref[...], sg_ref[...], sbn_ref[...],
                                            ATT_SCALE * LOG2E)
    _to_planes(ua, ua_ref, up_ref)
    glu_ref[...] = glu
    q_ref[...] = q.astype(BF16)
    k_ref[...] = k
    v_ref[...] = v
    kb_ref[...] = k.astype(BF16)
    vb = v.astype(BF16)
    ones = jnp.ones((tm, ATT_VD), BF16)
    vx_ref[...] = jnp.concatenate(
        [piece for h in range(ATT_HEADS) for piece in (vb[:, h * ATT_VD:(h + 1) * ATT_VD], ones)], axis=1)
    yb_ref[...] = _sgu(gub, vn, ws_ref, sb_ref)

    @pl.when(i % tiles_per_seq == 0)
    def _():
        ext_ref[0:HALO, :] = jnp.zeros((HALO, D_CONV), F32)

    ext_ref[HALO:HALO + tm, :] = glu
    for r in range(tm // rb):
        acc = jnp.zeros((rb, D_CONV), F32)
        for w in range(CONV_W):
            acc = acc + ext_ref[pl.ds(r * rb + HALO_OFF + w, rb), :] * cw_ref[w:w + 1, :]
        yc_ref[r * rb:(r + 1) * rb, :] = _conv_post(acc, cb_ref[...], cg_ref[...], cbn_ref[...])
    ext_ref[0:HALO, :] = ext_ref[tm:tm + HALO, :]


def _proj_prompt(x, w_in, tab, sgu_g, sgu_b, ws, sb, cw, cb, cg, cbn, *, seq, tm, chunk):
    n = x.shape[0]
    tiles_per_seq = seq // tm
    row = lambda width: pl.BlockSpec((tm, width), lambda i: (i, 0))
    f = lambda width, dt=F32: jax.ShapeDtypeStruct((n, width), dt)
    planes_spec = pl.BlockSpec((chunk, tm // chunk, D_S5), lambda i: (0, i, 0))
    planes_shape = jax.ShapeDtypeStruct((chunk, n // chunk, D_S5), F32)
    kern = functools.partial(_proj_prompt_kernel, tm=tm, tiles_per_seq=tiles_per_seq, rb=64)
    return pl.pallas_call(
        kern,
        grid=(n // tm,),
        in_specs=[row(D_MODEL), _const_spec((D_MODEL, D_IN)),
                  pl.BlockSpec((tm, 3 * LANES), lambda i: (i % tiles_per_seq, 0)),
                  _const_spec((1, D_SGU)), _const_spec((1, D_SGU)),
                  _const_spec((SGU_HEADS, CHUNK, CHUNK)), _const_spec((CHUNK, D_SGU)),
                  _const_spec((CONV_W, D_CONV)), _const_spec((1, D_CONV)), _const_spec((1, D_CONV)),
                  _const_spec((1, D_CONV))],
        out_specs=[planes_spec, row(D_SGU), row(D_CONV), row(D_CONV), row(D_ATT), row(D_ATT), row(D_ATT),
                   row(D_ATT), row(2 * D_ATT)],
        out_shape=[planes_shape, f(D_SGU), f(D_CONV), f(D_CONV), f(D_ATT, BF16), f(D_ATT), f(D_ATT),
                   f(D_ATT, BF16), f(2 * D_ATT, BF16)],
        scratch_shapes=[pltpu.VMEM((tm + HALO, D_CONV), F32), pltpu.VMEM((D_S5 // LANES, tm, LANES), F32)],
        compiler_params=_cparams(("arbitrary",)),
    )(x, w_in, tab, sgu_g, sgu_b, ws, sb, cw, cb, cg, cbn)


def _proj_sample_kernel(x_ref, w_ref, tab_ref, sg_ref, sbn_ref, ws_ref, sb_ref, cw_ref, cb_ref, cg_ref, cbn_ref,
                        pfx_ref,
                        up_ref, yb_ref, yc_ref, glu_ref, q_ref, k_ref, v_ref, vn_ref,
                        ext_ref, ua_ref, *, nb, tn):
    ua, gub, vn, glu, q, k, v = _proj_split(x_ref[...], w_ref, tab_ref[...], sg_ref[...], sbn_ref[...], ATT_SCALE)
    _to_planes(ua, ua_ref, up_ref)
    glu_ref[...] = glu
    q_ref[...] = q
    k_ref[...] = k
    v_ref[...] = v
    vn_ref[...] = vn
    yb_ref[...] = _sgu(gub, vn, ws_ref, sb_ref)

    ext_ref[:, 0:HALO, :] = pfx_ref[...]
    ext_ref[:, HALO:HALO + tn, :] = glu.reshape(nb, tn, D_CONV)
    acc = jnp.zeros((nb, tn, D_CONV), F32)
    for w in range(CONV_W):
        acc = acc + ext_ref[:, pl.ds(HALO_OFF + w, tn), :] * cw_ref[w:w + 1, :]
    yc_ref[...] = _conv_post(acc.reshape(nb * tn, D_CONV), cb_ref[...], cg_ref[...], cbn_ref[...])


def _proj_sample(x, w_in, tab, sgu_g, sgu_b, ws, sb, cw, cb, cg, cbn, pfx, *, nb, tn):
    n = x.shape[0]
    full = lambda width: _const_spec((n, width))
    f = lambda width: jax.ShapeDtypeStruct((n, width), F32)
    kern = functools.partial(_proj_sample_kernel, nb=nb, tn=tn)
    return pl.pallas_call(
        kern,
        grid=(1,),
        in_specs=[full(D_MODEL), _const_spec((D_MODEL, D_IN)), full(3 * LANES),
                  _const_spec((1, D_SGU)), _const_spec((1, D_SGU)),
                  _const_spec((SGU_HEADS, CHUNK, CHUNK)), _const_spec((CHUNK, D_SGU)),
                  _const_spec((CONV_W, D_CONV)), _const_spec((1, D_CONV)), _const_spec((1, D_CONV)),
                  _const_spec((1, D_CONV)), _const_spec((nb, HALO, D_CONV))],
        out_specs=[_const_spec((tn, nb, D_S5)), full(D_SGU), full(D_CONV), full(D_CONV), full(D_ATT), full(D_ATT),
                   full(D_ATT), full(D_SGU)],
        out_shape=[jax.ShapeDtypeStruct((tn, nb, D_S5), F32), f(D_SGU), f(D_CONV), f(D_CONV), f(D_ATT), f(D_ATT),
                   f(D_ATT), f(D_SGU)],
        scratch_shapes=[pltpu.VMEM((nb, HALO + tn, D_CONV), F32), pltpu.VMEM((D_S5 // LANES, n, LANES), F32)],
        compiler_params=_cparams(("arbitrary",)),
    )(x, w_in, tab, sgu_g, sgu_b, ws, sb, cw, cb, cg, cbn, pfx)


def _s5_mats(lam_re, lam_im, log_dt, b_re, b_im, c_re, c_im, L):
    G, P, C = S5_GROUPS, S5_STATE, S5_GROUP
    lr, li = lam_re.astype(F32), lam_im.astype(F32)
    dt = jnp.exp(log_dt.astype(F32))[:, None]
    steps = jnp.arange(L + 1, dtype=F32)[:, None, None]
    mag = jnp.exp((lr * dt)[None] * steps)
    pr = mag * jnp.cos((li * dt)[None] * steps)
    pi = mag * jnp.sin((li * dt)[None] * steps)
    nr, ni = pr[1] - 1.0, pi[1]
    den = lr * lr + li * li
    fr = (nr * lr + ni * li) / den
    fi = (ni * lr - nr * li) / den
    bre, bim = b_re.astype(F32), b_im.astype(F32)
    br = fr[..., None] * bre - fi[..., None] * bim
    bi = fr[..., None] * bim + fi[..., None] * bre
    cr, ci = c_re.astype(F32), c_im.astype(F32)
    kr = (jnp.einsum('gcp,tgp,gpd->tgdc', cr, pr[:L], br) - jnp.einsum('gcp,tgp,gpd->tgdc', cr, pi[:L], bi)
          - jnp.einsum('gcp,tgp,gpd->tgdc', ci, pr[:L], bi) - jnp.einsum('gcp,tgp,gpd->tgdc', ci, pi[:L], br))

    def block_diag(compact):
        _, _, r, c = compact.shape
        repeat = jnp.tile(jnp.eye(c, dtype=F32), (1, G))
        tiled = jnp.einsum('lrc,cq->lrq', compact.reshape(L, G * r, c), repeat, precision=lax.Precision.HIGHEST)
        row_g = lax.broadcasted_iota(jnp.int32, (G * r, G * c), 0) // r
        col_g = lax.broadcasted_iota(jnp.int32, (G * r, G * c), 1) // c
        return jnp.where(row_g == col_g, tiled, 0.0).astype(BF16)

    kbd = block_diag(kr)
    rev = L - 1 - jnp.arange(L)
    brt, bit = br.transpose(0, 2, 1), bi.transpose(0, 2, 1)
    m1re = block_diag(pr[rev][:, :, None, :] * brt[None] - pi[rev][:, :, None, :] * bit[None])
    m1im = block_diag(pr[rev][:, :, None, :] * bit[None] + pi[rev][:, :, None, :] * brt[None])
    crt, cit = cr.transpose(0, 2, 1), ci.transpose(0, 2, 1)
    nr, ni = pr[1:L + 1][..., None], pi[1:L + 1][..., None]
    m2re = block_diag(crt[None] * nr - cit[None] * ni)
    m2im = block_diag(-(crt[None] * ni + cit[None] * nr))
    return (kbd, m1re, m1im, m2re, m2im), pr.reshape(L + 1, 1, G * P), pi.reshape(L + 1, 1, G * P)


def _s5_local_kernel(up_ref, m1re_ref, m1im_ref, slre_ref, slim_ref):
    @pl.when(pl.program_id(0) == 0)
    def _():
        slre_ref[...] = jnp.zeros_like(slre_ref)
        slim_ref[...] = jnp.zeros_like(slim_ref)

    u = up_ref[...].astype(BF16)
    slre_ref[...] += jnp.dot(u, m1re_ref[...], preferred_element_type=F32)
    slim_ref[...] += jnp.dot(u, m1im_ref[...], preferred_element_type=F32)


def _s5_scan_kernel(slre_ref, slim_ref, s0re_ref, s0im_ref, are_ref, aim_ref,
                    stre_ref, stim_ref, fre_ref, fim_ref, *, nb, cpb):
    ar = are_ref[...]
    ai = aim_ref[...]

    def body(c, carry):
        out = []
        for b in range(nb):
            sr, si = carry[b]
            row = pl.ds(b * cpb + c, 1)
            stre_ref[row, :] = sr
            stim_ref[row, :] = si
            out.append((ar * sr - ai * si + slre_ref[row, :], ar * si + ai * sr + slim_ref[row, :]))
        return tuple(out)

    init = tuple((s0re_ref[b:b + 1, :], s0im_ref[b:b + 1, :]) for b in range(nb))
    fin = lax.fori_loop(0, cpb, body, init)
    for b in range(nb):
        fre_ref[b:b + 1, :] = fin[b][0]
        fim_ref[b:b + 1, :] = fin[b][1]


def _s5_out_kernel(up_ref, kbd_ref, stre_ref, stim_ref, m2re_ref, m2im_ref, d_ref, o_ref):
    i = pl.program_id(0)
    o_ref[...] = (_mm(stre_ref[...], m2re_ref[...]) + _mm(stim_ref[...], m2im_ref[...])
                  + d_ref[...] * up_ref[i])

    def body(tau, carry):
        o_ref[...] += _mm(up_ref[i - tau], kbd_ref[tau])
        return carry

    lax.fori_loop(0, i + 1, body, 0)
    o_ref[...] = _gelu(o_ref[...])


def _s5_branch(up, s0_re, s0_im, tables, a_re, a_im, d_skip, *, nb, t):
    kbd, m1re, m1im, m2re, m2im = tables
    L, ch, dc = up.shape
    l_tab = kbd.shape[0]
    m1_off = l_tab - L
    G, P = S5_GROUPS, S5_STATE
    gp = G * P
    cpb = t // L
    acc_spec = _const_spec((ch, gp))
    acc_shape = jax.ShapeDtypeStruct((ch, gp), F32)
    slre, slim = pl.pallas_call(
        _s5_local_kernel,
        grid=(L,),
        in_specs=[pl.BlockSpec((None, ch, dc), lambda j: (j, 0, 0)),
                  pl.BlockSpec((None, dc, gp), lambda j: (j + m1_off, 0, 0)),
                  pl.BlockSpec((None, dc, gp), lambda j: (j + m1_off, 0, 0))],
        out_specs=[acc_spec, acc_spec],
        out_shape=[acc_shape, acc_shape],
        compiler_params=_cparams(("arbitrary",)),
    )(up, m1re, m1im)
    fin_shape = jax.ShapeDtypeStruct((nb, gp), F32)
    stre, stim, fre, fim = pl.pallas_call(
        functools.partial(_s5_scan_kernel, nb=nb, cpb=cpb),
        out_shape=[acc_shape, acc_shape, fin_shape, fin_shape],
        compiler_params=pltpu.CompilerParams(vmem_limit_bytes=VMEM_LIMIT),
    )(slre, slim, s0_re.reshape(nb, gp).astype(F32), s0_im.reshape(nb, gp).astype(F32), a_re, a_im)
    yp = pl.pallas_call(
        _s5_out_kernel,
        grid=(L,),
        in_specs=[_const_spec((L, ch, dc)), _const_spec((l_tab, dc, dc)), acc_spec, acc_spec,
                  pl.BlockSpec((None, gp, dc), lambda i: (i, 0, 0)),
                  pl.BlockSpec((None, gp, dc), lambda i: (i, 0, 0)),
                  _const_spec((1, dc))],
        out_specs=pl.BlockSpec((None, ch, dc), lambda i: (i, 0, 0)),
        out_shape=jax.ShapeDtypeStruct((L, ch, dc), F32),
        compiler_params=_cparams(("parallel",)),
    )(up, kbd, stre, stim, m2re, m2im, d_skip.astype(F32).reshape(1, dc))
    return yp, fre.reshape(nb, G, P), fim.reshape(nb, G, P)


def _diff_finish(o1, o2, lam, gs):
    o = o1 - lam * o2
    return o * lax.rsqrt(jnp.mean(o * o, axis=-1, keepdims=True) + RMS_EPS) * gs


def _attn_prompt_kernel(q_ref, k_ref, v_ref, lam_ref, gs_ref, o_ref, q2_ref, m_ref, acc_ref, *, tq, rb, unroll):
    i = pl.program_id(2)
    r2 = 2 * tq
    q = q_ref[...]
    lane = lax.broadcasted_iota(jnp.int32, (tq, ATT_VD), 1)
    zero = jnp.zeros_like(q)
    q2_ref[0:tq, :] = jnp.where(lane < ATT_HD, q, zero)
    q2_ref[tq:r2, :] = jnp.where(lane >= ATT_HD, q, zero)
    m_ref[...] = jnp.full((r2, LANES), NEG_INF, F32)
    acc_ref[...] = jnp.zeros((r2, 2 * ATT_VD), F32)

    def step(j, diagonal):
        off = pl.multiple_of(j * tq, tq)
        kt = k_ref[pl.ds(off, tq), :]
        vt = v_ref[pl.ds(off, tq), :]
        for r in range(r2 // rb):
            rows = slice(r * rb, (r + 1) * rb)
            width = ((r * rb) % tq + rb) if diagonal else tq
            s = _mm_nt(q2_ref[rows, :], kt[0:width])
            if diagonal:
                row = (lax.broadcasted_iota(jnp.int32, (rb, width), 0) + r * rb) % tq
                col = lax.broadcasted_iota(jnp.int32, (rb, width), 1)
                s = jnp.where(col <= row, s, NEG_INF)
            m_prev = m_ref[rows, :]
            m_new = jnp.maximum(m_prev, jnp.max(s, axis=-1, keepdims=True))
            alpha = jnp.exp2(m_prev - m_new)
            p = jnp.exp2(s - jnp.concatenate([m_new] * (width // LANES), axis=1))
            acc_ref[rows, :] = jnp.concatenate([alpha, alpha], axis=1) * acc_ref[rows, :] + _mm(p, vt[0:width])
            m_ref[rows, :] = m_new

    def group(g, carry):
        for u in range(unroll):
            step(g * unroll + u, False)
        return carry

    def single(j, carry):
        step(j, False)
        return carry

    n_grp = i // unroll
    lax.fori_loop(0, n_grp, group, 0)
    lax.fori_loop(n_grp * unroll, i, single, 0)
    step(i, True)
    acc = acc_ref[...]
    o = acc[:, 0:ATT_VD] / acc[:, ATT_VD:]
    o_ref[...] = _diff_finish(o[0:tq], o[tq:r2], lam_ref[...], gs_ref[...])


def _attn_prompt(qb, kb, vx, lamv, gsv, *, nb, t, tq):
    n = nb * t
    nq = t // tq
    return pl.pallas_call(
        functools.partial(_attn_prompt_kernel, tq=tq, rb=min(128, tq), unroll=4),
        grid=(nb, ATT_HEADS, nq),
        in_specs=[pl.BlockSpec((tq, ATT_VD), lambda b, h, i: (b * nq + i, h)),
                  pl.BlockSpec((t, ATT_VD), lambda b, h, i: (b, h)),
                  pl.BlockSpec((t, 2 * ATT_VD), lambda b, h, i: (b, h)),
                  _const_spec((1, ATT_VD)), _const_spec((1, ATT_VD))],
        out_specs=pl.BlockSpec((tq, ATT_VD), lambda b, h, i: (b * nq + i, h)),
        out_shape=jax.ShapeDtypeStruct((n, D_ATT), F32),
        scratch_shapes=[pltpu.VMEM((2 * tq, ATT_VD), BF16), pltpu.VMEM((2 * tq, LANES), F32),
                        pltpu.VMEM((2 * tq, 2 * ATT_VD), F32)],
        compiler_params=_cparams(("parallel", "parallel", "arbitrary")),
    )(qb, kb, vx, lamv, gsv)


def _attn_sample_kernel(pt_ref, q_ref, kn_ref, vn_ref, lam_ref, gs_ref, *rest, tn, pg):
    k_refs = rest[0:pg]
    v_refs = rest[pg:2 * pg]
    o_ref = rest[2 * pg]
    qbd_ref, m_ref, l_ref, acc_ref = rest[2 * pg + 1:]
    p_idx = pl.program_id(1)
    rows_h = 2 * tn
    n_rows = ATT_HEADS * rows_h

    def update(s, v_pages):
        n_rep = s.shape[1] // LANES
        m_prev = m_ref[...]
        m_new = jnp.maximum(m_prev, jnp.max(s, axis=-1, keepdims=True))
        p = jnp.exp(s - jnp.concatenate([m_new] * n_rep, axis=1))
        alpha = jnp.exp(m_prev - m_new)
        l_ref[...] = alpha * l_ref[...] + jnp.sum(p, axis=-1, keepdims=True)
        for h in range(ATT_HEADS):
            rows = slice(h * rows_h, (h + 1) * rows_h)
            v_h = [r[pl.ds(h, PAGE_SIZE, stride=ATT_HEADS), :].astype(BF16) for r in v_pages]
            v_h = v_h[0] if len(v_h) == 1 else jnp.concatenate(v_h, axis=0)
            acc_ref[rows, :] = alpha[rows] * acc_ref[rows, :] + _mm(p[rows], v_h)
        m_ref[...] = m_new

    @pl.when(p_idx == 0)
    def _():
        q = q_ref[...]
        lane_grp = lax.broadcasted_iota(jnp.int32, (tn, D_ATT), 1) // ATT_HD
        pieces = [jnp.where(lane_grp == g, q, 0.0) for g in range(2 * ATT_HEADS)]
        qbd_ref[...] = jnp.concatenate(pieces, axis=0).astype(BF16)
        m_ref[...] = jnp.full((n_rows, LANES), NEG_INF, F32)
        l_ref[...] = jnp.zeros((n_rows, LANES), F32)
        acc_ref[...] = jnp.zeros((n_rows, ATT_VD), F32)
        s = _mm(qbd_ref[...], kn_ref[...])
        t_of_row = lax.broadcasted_iota(jnp.int32, (n_rows, PAGE_SIZE), 0) % tn
        col = lax.broadcasted_iota(jnp.int32, (n_rows, PAGE_SIZE), 1)
        update(jnp.where(col <= t_of_row, s, NEG_INF), [vn_ref])

    kcat = jnp.concatenate([r[...].astype(BF16) for r in k_refs], axis=1)
    update(_mm(qbd_ref[...], kcat), list(v_refs))

    @pl.when(p_idx == pl.num_programs(1) - 1)
    def _():
        o = acc_ref[...] / l_ref[...]
        outs = []
        for h in range(ATT_HEADS):
            o1 = o[h * rows_h:h * rows_h + tn]
            o2 = o[h * rows_h + tn:(h + 1) * rows_h]
            outs.append(_diff_finish(o1, o2, lam_ref[...], gs_ref[...]))
        o_ref[...] = jnp.concatenate(outs, axis=1)


def _attn_sample(q, k_new, v_new, cache_k, cache_v, page_table, layer, lamv, gsv, *, nb, tn, pg):
    n_pages = page_table.shape[1]
    depth, n_pool = cache_k.shape[0], cache_k.shape[1]
    rows_v = PAGE_SIZE * ATT_HEADS
    ck = jnp.transpose(cache_k, (0, 1, 3, 4, 5, 2)).reshape(depth, n_pool, D_ATT, PAGE_SIZE)
    cv = cache_v.reshape(depth, n_pool, rows_v, ATT_VD)
    kn = jnp.pad(k_new.reshape(nb, tn, D_ATT).transpose(0, 2, 1), ((0, 0), (0, 0), (0, PAGE_SIZE - tn)))
    vn = jnp.pad(v_new.reshape(nb, tn * ATT_HEADS, ATT_VD), ((0, 0), (0, rows_v - tn * ATT_HEADS), (0, 0)))
    n_rows = 2 * ATT_HEADS * tn

    def page_spec(r, rows, cols):
        return pl.BlockSpec((None, None, rows, cols), lambda b, p, pt: (layer, pt[b, p * pg + r], 0, 0))

    grid_spec = pltpu.PrefetchScalarGridSpec(
        num_scalar_prefetch=1,
        grid=(nb, n_pages // pg),
        in_specs=[pl.BlockSpec((tn, D_ATT), lambda b, p, pt: (b, 0)),
                  pl.BlockSpec((None, D_ATT, PAGE_SIZE), lambda b, p, pt: (b, 0, 0)),
                  pl.BlockSpec((None, rows_v, ATT_VD), lambda b, p, pt: (b, 0, 0)),
                  pl.BlockSpec((1, ATT_VD), lambda b, p, pt: (0, 0)),
                  pl.BlockSpec((1, ATT_VD), lambda b, p, pt: (0, 0))]
                 + [page_spec(r, D_ATT, PAGE_SIZE) for r in range(pg)]
                 + [page_spec(r, rows_v, ATT_VD) for r in range(pg)],
        out_specs=pl.BlockSpec((tn, D_ATT), lambda b, p, pt: (b, 0)),
        scratch_shapes=[pltpu.VMEM((n_rows, D_ATT), BF16), pltpu.VMEM((n_rows, LANES), F32),
                        pltpu.VMEM((n_rows, LANES), F32), pltpu.VMEM((n_rows, ATT_VD), F32)],
    )
    return pl.pallas_call(
        functools.partial(_attn_sample_kernel, tn=tn, pg=pg),
        grid_spec=grid_spec,
        out_shape=jax.ShapeDtypeStruct((nb * tn, D_ATT), F32),
        compiler_params=_cparams(("parallel", "arbitrary")),
    )(page_table, q, kn, vn, lamv, gsv, *([ck] * pg), *([cv] * pg))


def _merge_kernel(x_ref, ya_ref, yb_ref, yc_ref, yd_ref, wglu_ref, wgate_ref, bgate_ref,
                  wa_ref, wb_ref, wc_ref, wd_ref, wo_ref, g_ref, b_ref, o_ref, ya_rows_ref, *, alpha):
    x = x_ref[...]
    xb = x.astype(BF16)
    hg = _mm(_from_planes(ya_ref, ya_rows_ref), wglu_ref[...])
    ya = hg[:, :D_S5] * _sigmoid(hg[:, D_S5:])
    merged = None
    for idx, (y, w_ref) in enumerate(((ya, wa_ref), (yb_ref[...], wb_ref), (yc_ref[...], wc_ref),
                                      (yd_ref[...], wd_ref))):
        cols = slice(idx * D_MODEL, (idx + 1) * D_MODEL)
        gate = _sigmoid(_mm(xb, wgate_ref[:, cols]) + bgate_ref[:, cols])
        term = gate * _mm(y, w_ref[...])
        merged = term if merged is None else merged + term
    out = _mm(merged, wo_ref[...])
    o_ref[...] = _ln(alpha * x + out, g_ref[...], b_ref[...])


def _merge(x, ya, yb, yc, yd, w, *, tm, alpha):
    n = x.shape[0]
    chunk = ya.shape[0]
    row = lambda width: pl.BlockSpec((tm, width), lambda i: (i, 0))
    return pl.pallas_call(
        functools.partial(_merge_kernel, alpha=alpha),
        grid=(n // tm,),
        in_specs=[row(D_MODEL), pl.BlockSpec((chunk, tm // chunk, D_S5), lambda i: (0, i, 0)),
                  row(D_SGU), row(D_CONV), row(D_ATT),
                  _const_spec((D_S5, 2 * D_S5)), _const_spec((D_MODEL, N_BRANCH * D_MODEL)),
                  _const_spec((1, N_BRANCH * D_MODEL)),
                  _const_spec((D_S5, D_MODEL)), _const_spec((D_SGU, D_MODEL)), _const_spec((D_CONV, D_MODEL)),
                  _const_spec((D_ATT, D_MODEL)), _const_spec((D_MODEL, D_MODEL)),
                  _const_spec((1, D_MODEL)), _const_spec((1, D_MODEL))],
        out_specs=row(D_MODEL),
        out_shape=jax.ShapeDtypeStruct((n, D_MODEL), F32),
        scratch_shapes=[pltpu.VMEM((D_S5 // LANES, tm, LANES), F32)],
        compiler_params=_cparams(("parallel",)),
    )(x, ya, yb, yc, yd, w['w_glu'], w['w_gate'], w['b_gate'], w['w_br_a'], w['w_br_b'], w['w_br_c'],
      w['w_br_d'], w['w_o'], w['ln1_g'], w['ln1_b'])


def _ffn_kernel(x_ref, wg_ref, wu_ref, wd_ref, g_ref, b_ref, o_ref, *, alpha):
    x = x_ref[...]
    xb = x.astype(BF16)
    hid = _silu(_mm(xb, wg_ref[...])) * _mm(xb, wu_ref[...])
    o_ref[...] = _ln(alpha * x + _mm(hid, wd_ref[...]), g_ref[...], b_ref[...])


def _ffn(x, wg, wu, wd, g, b, *, tm, alpha):
    n = x.shape[0]
    dff = wg.shape[1]
    row = pl.BlockSpec((tm, D_MODEL), lambda i: (i, 0))
    return pl.pallas_call(
        functools.partial(_ffn_kernel, alpha=alpha),
        grid=(n // tm,),
        in_specs=[row, _const_spec((D_MODEL, dff)), _const_spec((D_MODEL, dff)), _const_spec((dff, D_MODEL)),
                  _const_spec((1, D_MODEL)), _const_spec((1, D_MODEL))],
        out_specs=row,
        out_shape=jax.ShapeDtypeStruct((n, D_MODEL), F32),
        compiler_params=_cparams(("parallel",)),
    )(x, wg, wu, wd, g, b)


def _split_bf16(a):
    hi = a.astype(BF16)
    lo = (a - hi.astype(F32)).astype(BF16)
    return hi, lo


SEL_I1, SEL_I2, SEL_W1, SEL_W2 = 0, 1, 2, 3
MOE_TM = 512


def _moe_route_kernel(x_ref, wr_hi_ref, wr_lo_ref, br_ref, sel_ref):
    x = x_ref[...]
    tm = x.shape[0]
    lane = lax.broadcasted_iota(jnp.int32, (tm, LANES), 1)
    x_hi, x_lo = _split_bf16(x)
    logits = (jnp.dot(x_hi, wr_hi_ref[...], preferred_element_type=F32)
              + jnp.dot(x_hi, wr_lo_ref[...], preferred_element_type=F32)
              + jnp.dot(x_lo, wr_hi_ref[...], preferred_element_type=F32)) + br_ref[...]
    logits = jnp.where(lane < N_EXPERTS, logits, -jnp.inf)
    lane_f = lane.astype(F32)
    v1 = jnp.max(logits, axis=-1, keepdims=True)
    i1 = jnp.min(jnp.where(logits == v1, lane_f, float(LANES)), axis=-1, keepdims=True)
    rest = jnp.where(lane_f == i1, -jnp.inf, logits)
    v2 = jnp.max(rest, axis=-1, keepdims=True)
    i2 = jnp.min(jnp.where(rest == v2, lane_f, float(LANES)), axis=-1, keepdims=True)
    e2 = jnp.exp(v2 - v1)
    w1 = 1.0 / (1.0 + e2)
    w2 = e2 / (1.0 + e2)
    sel_ref[...] = (jnp.where(lane == SEL_I1, i1, 0.0) + jnp.where(lane == SEL_I2, i2, 0.0)
                    + jnp.where(lane == SEL_W1, w1, 0.0) + jnp.where(lane == SEL_W2, w2, 0.0))


def _moe_expert_kernel(item_e, item_tile, item_flag, starts, idx_ref, idx_next_ref, x_hbm, wg_ref, wu_ref, wd_ref,
                       y_hbm, xbuf, obuf, gsem, ssem, *, n_tiles, n_tok):
    i = pl.program_id(0)
    tm = xbuf.shape[1]
    flag = item_flag[i]
    valid = (flag & 1) == 1
    first = (flag & 2) == 2
    last = (flag & 4) == 4
    t = item_tile[i]
    e = item_e[i]
    slot = t % 2

    def gather_rows(idx, dst_slot):
        def body(r, carry):
            a = idx[0, r]
            tok = jnp.where(a >= n_tok, a - n_tok, a)
            pltpu.make_async_copy(x_hbm.at[pl.ds(tok, 1)], xbuf.at[dst_slot, pl.ds(r, 1)], gsem.at[dst_slot]).start()
            return carry
        lax.fori_loop(0, tm, body, 0, unroll=8)

    def wait_rows(buf, sem, s):
        pltpu.make_async_copy(buf.at[s], buf.at[s], sem.at[s]).wait()

    @pl.when(valid & first)
    def _():
        @pl.when(i == 0)
        def _():
            gather_rows(idx_ref, 0)

        @pl.when(t + 1 < n_tiles)
        def _():
            gather_rows(idx_next_ref, 1 - slot)

        wait_rows(xbuf, gsem, slot)

        @pl.when(t >= 2)
        def _():
            wait_rows(obuf, ssem, slot)

        obuf[slot] = jnp.zeros((tm, D_MODEL), F32)

    @pl.when(valid)
    def _():
        xb = xbuf[slot].astype(BF16)
        hid = _silu(_mm(xb, wg_ref[...])) * _mm(xb, wu_ref[...])
        y = _mm(hid, wd_ref[...])
        pos = t * tm + lax.broadcasted_iota(jnp.int32, (tm, 1), 0)
        mine = (pos >= starts[e]) & (pos < starts[e + 1])
        obuf[slot] += jnp.where(mine, y, 0.0)

    @pl.when(valid & last)
    def _():
        def body(r, carry):
            dst = idx_ref[0, r]
            pltpu.make_async_copy(obuf.at[slot, pl.ds(r, 1)], y_hbm.at[pl.ds(dst, 1)], ssem.at[slot]).start()
            return carry
        lax.fori_loop(0, tm, body, 0, unroll=8)

    @pl.when(i == pl.num_programs(0) - 1)
    def _():
        for tt in range(max(0, n_tiles - 2), n_tiles):
            wait_rows(obuf, ssem, tt % 2)


def _moe_combine_kernel(x_ref, y1_ref, y2_ref, sel_ref, g_ref, b_ref, o_ref, *, alpha):
    sel = sel_ref[...]
    lane = lax.broadcasted_iota(jnp.int32, sel.shape, 1)
    w1 = jnp.sum(jnp.where(lane == SEL_W1, sel, 0.0), axis=-1, keepdims=True)
    w2 = jnp.sum(jnp.where(lane == SEL_W2, sel, 0.0), axis=-1, keepdims=True)
    y = w1 * y1_ref[...] + w2 * y2_ref[...]
    o_ref[...] = _ln(alpha * x_ref[...] + y, g_ref[...], b_ref[...])


def _moe(x, wr_hi, wr_lo, br, wg, wu, wd, g, b, *, tm, alpha):
    n = x.shape[0]
    dfe = wg.shape[2]
    n_asg = TOP_K * n
    tmx = min(MOE_TM, n_asg)
    assert n_asg % tmx == 0
    n_tiles = n_asg // tmx
    n_items = n_tiles + N_EXPERTS - 1
    row = pl.BlockSpec((tm, D_MODEL), lambda i: (i, 0))

    sel = pl.pallas_call(
        _moe_route_kernel,
        grid=(n // tm,),
        in_specs=[row, _const_spec((D_MODEL, LANES)), _const_spec((D_MODEL, LANES)), _const_spec((1, LANES))],
        out_specs=pl.BlockSpec((tm, LANES), lambda i: (i, 0)),
        out_shape=jax.ShapeDtypeStruct((n, LANES), F32),
        compiler_params=_cparams(("parallel",)),
    )(x, wr_hi, wr_lo, br)

    experts = sel[:, SEL_I1:SEL_I2 + 1].astype(jnp.int32).T.reshape(n_asg)
    keys = jnp.sort(experts * n_asg + jnp.arange(n_asg, dtype=jnp.int32))
    idx = (keys % n_asg).reshape(n_tiles, 1, tmx)
    bounds = jnp.arange(N_EXPERTS + 1, dtype=jnp.int32) * n_asg
    starts = jnp.sum((keys[None, :] < bounds[:, None]).astype(jnp.int32), axis=1)
    counts = starts[1:] - starts[:-1]
    first_tile = starts[:-1] // tmx
    last_tile = (starts[1:] - 1) // tmx
    per_e = jnp.where(counts > 0, last_tile - first_tile + 1, 0)
    ends = jnp.cumsum(per_e)
    total = ends[-1]
    it = jnp.arange(n_items, dtype=jnp.int32)
    e_raw = jnp.sum((ends[None, :] <= it[:, None]).astype(jnp.int32), axis=1)
    e_last = jnp.max(jnp.where(counts > 0, jnp.arange(N_EXPERTS, dtype=jnp.int32), 0))
    valid = it < total
    item_e = jnp.where(valid, jnp.minimum(e_raw, N_EXPERTS - 1), e_last).astype(jnp.int32)
    item_tile = jnp.where(valid, first_tile[item_e] + it - (ends[item_e] - per_e[item_e]), n_tiles - 1)
    item_tile = item_tile.astype(jnp.int32)
    prev_tile = jnp.concatenate([jnp.full((1,), -1, jnp.int32), item_tile[:-1]])
    next_tile = jnp.concatenate([item_tile[1:], jnp.full((1,), -1, jnp.int32)])
    next_valid = jnp.concatenate([valid[1:], jnp.zeros((1,), bool)])
    is_first = item_tile != prev_tile
    is_last = (item_tile != next_tile) | ~next_valid
    item_flag = (valid.astype(jnp.int32) + 2 * is_first.astype(jnp.int32) + 4 * is_last.astype(jnp.int32))

    smem_idx = lambda f: pl.BlockSpec((None, 1, tmx), f, memory_space=pltpu.SMEM)
    grid_spec = pltpu.PrefetchScalarGridSpec(
        num_scalar_prefetch=4,
        grid=(n_items,),
        in_specs=[smem_idx(lambda i, ie, itl, ifl, st: (itl[i], 0, 0)),
                  smem_idx(lambda i, ie, itl, ifl, st: (jnp.minimum(itl[i] + 1, n_tiles - 1), 0, 0)),
                  pl.BlockSpec(memory_space=pl.ANY),
                  pl.BlockSpec((None, D_MODEL, dfe), lambda i, ie, itl, ifl, st: (ie[i], 0, 0)),
                  pl.BlockSpec((None, D_MODEL, dfe), lambda i, ie, itl, ifl, st: (ie[i], 0, 0)),
                  pl.BlockSpec((None, dfe, D_MODEL), lambda i, ie, itl, ifl, st: (ie[i], 0, 0))],
        out_specs=pl.BlockSpec(memory_space=pl.ANY),
        scratch_shapes=[pltpu.VMEM((2, tmx, D_MODEL), F32), pltpu.VMEM((2, tmx, D_MODEL), F32),
                        pltpu.SemaphoreType.DMA((2,)), pltpu.SemaphoreType.DMA((2,))],
    )
    y = pl.pallas_call(
        functools.partial(_moe_expert_kernel, n_tiles=n_tiles, n_tok=n),
        grid_spec=grid_spec,
        out_shape=jax.ShapeDtypeStruct((n_asg, D_MODEL), F32),
        compiler_params=_cparams(("arbitrary",)),
    )(item_e, item_tile, item_flag, starts, idx, idx, x, wg, wu, wd)

    return pl.pallas_call(
        functools.partial(_moe_combine_kernel, alpha=alpha),
        grid=(n // tm,),
        in_specs=[row, row, pl.BlockSpec((tm, D_MODEL), lambda i: (i + n // tm, 0)),
                  pl.BlockSpec((tm, LANES), lambda i: (i, 0)), _const_spec((1, D_MODEL)), _const_spec((1, D_MODEL))],
        out_specs=row,
        out_shape=jax.ShapeDtypeStruct((n, D_MODEL), F32),
        compiler_params=_cparams(("parallel",)),
    )(x, y, y, sel, g, b)


def _rope_table(pos):
    half = ROT_DIM // 2
    inv = ROPE_THETA ** (-jnp.arange(half, dtype=F32) * (2.0 / ROT_DIM))
    ang = pos.astype(F32)[:, None] * inv[None, :]
    cos, sin = jnp.cos(ang), jnp.sin(ang)
    n = pos.shape[0]
    one = jnp.ones((n, ATT_HD - ROT_DIM), F32)
    zero = jnp.zeros((n, ATT_HD - ROT_DIM), F32)
    z8 = jnp.zeros((n, half), F32)
    c = jnp.concatenate([cos, cos, one], axis=1)
    s_hi = jnp.concatenate([-sin, z8, zero], axis=1)
    s_lo = jnp.concatenate([z8, sin, zero], axis=1)
    rep = LANES // ATT_HD
    return jnp.concatenate([jnp.tile(c, (1, rep)), jnp.tile(s_hi, (1, rep)), jnp.tile(s_lo, (1, rep))], axis=1)


def _sgu_tables(w_s, b_s, tn):
    tril = jnp.tril(jnp.ones((CHUNK, CHUNK), dtype=bool))
    ws = jnp.where(tril, w_s, 0)
    hd = D_SGU // SGU_HEADS
    bias_p = jnp.repeat(b_s.T, hd, axis=1)
    rep = CHUNK // tn
    eye = jnp.eye(rep, dtype=w_s.dtype)
    ws_s = jnp.einsum('ab,gts->gatbs', eye, ws[:, :tn, :tn]).reshape(SGU_HEADS, CHUNK, CHUNK)
    bias_s = jnp.tile(bias_p[:tn], (rep, 1))
    return ws.astype(BF16), bias_p.astype(F32), ws_s.astype(BF16), bias_s.astype(F32)


def kernel(x_prompt, x_sample, cache_k, cache_v, state_ssm_re, state_ssm_im, state_conv, page_table, ln_in_g, ln_in_b, w_in, w_gate, b_gate, s5_lam_re, s5_lam_im, s5_log_dt, s5_b_re, s5_b_im, s5_c_re, s5_c_im, s5_d, s5_w_glu, sgu_ln_g, sgu_ln_b, sgu_w_s, sgu_b_s, conv_w, conv_b, conv_ln_g, conv_ln_b, att_lam_q1, att_lam_k1, att_lam_q2, att_lam_k2, att_subln_g, w_br_a, w_br_b, w_br_c, w_br_d, w_o, ln1_g, ln1_b, ffn_wg, ffn_wu, ffn_wd, moe_router, moe_router_b, moe_wg, moe_wu, moe_wd, ln2_g, ln2_b):
    bp, tp, d = x_prompt.shape
    bs, ts, _ = x_sample.shape
    depth = w_in.shape[0]
    n_pages = page_table.shape[1]
    past = n_pages * PAGE_SIZE
    alpha = (2 * depth) ** 0.25
    n_p, n_s = bp * tp, bs * ts
    tm_p = min(512, tp)
    tq = min(512, tp)
    l_p = 16
    pg = min(8, n_pages)
    assert tp % tm_p == 0 and tp % CHUNK == 0 and tp % tq == 0 and tp % l_p == 0 and tp >= CONV_W - 1
    assert n_s % CHUNK == 0 and CHUNK % ts == 0 and ts % SUBLANES == 0 and n_pages % pg == 0 and ts <= l_p

    row1 = lambda a: a.reshape(1, -1).astype(F32)
    tab_p = _rope_table(jnp.arange(tp, dtype=jnp.int32))
    tab_s = jnp.tile(_rope_table(past + jnp.arange(ts, dtype=jnp.int32)), (bs, 1))
    zero_state = jnp.zeros((bp, S5_GROUPS, S5_STATE), F32)

    xp = _layernorm(x_prompt.reshape(n_p, d), ln_in_g, ln_in_b, tm_p)
    xs = _layernorm(x_sample.reshape(n_s, d), ln_in_g, ln_in_b, n_s)

    outs = [[] for _ in range(11)]
    for l in range(depth):
        lam_init = 0.8 - 0.6 * math.exp(-0.3 * l)
        lam = (jnp.exp(jnp.sum(att_lam_q1[l].astype(F32) * att_lam_k1[l].astype(F32)))
               - jnp.exp(jnp.sum(att_lam_q2[l].astype(F32) * att_lam_k2[l].astype(F32))) + lam_init)
        lamv = jnp.full((1, ATT_VD), lam, F32)
        gsv = row1(att_subln_g[l]) * (1.0 - lam_init)
        w_in_b = w_in[l].astype(BF16)
        ws_p, sb_p, ws_s, sb_s = _sgu_tables(sgu_w_s[l], sgu_b_s[l], ts)
        conv_args = (conv_w[l].astype(F32), row1(conv_b[l]), row1(conv_ln_g[l]), row1(conv_ln_b[l]))
        sgu_ln = (row1(sgu_ln_g[l]), row1(sgu_ln_b[l]))
        s5_par = (s5_lam_re[l], s5_lam_im[l], s5_log_dt[l], s5_b_re[l], s5_b_im[l], s5_c_re[l], s5_c_im[l])
        mw = {'w_glu': s5_w_glu[l].astype(BF16), 'w_gate': w_gate[l].astype(BF16), 'b_gate': row1(b_gate[l]),
              'w_br_a': w_br_a[l].astype(BF16), 'w_br_b': w_br_b[l].astype(BF16),
              'w_br_c': w_br_c[l].astype(BF16), 'w_br_d': w_br_d[l].astype(BF16), 'w_o': w_o[l].astype(BF16),
              'ln1_g': row1(ln1_g[l]), 'ln1_b': row1(ln1_b[l])}
        ln2 = (row1(ln2_g[l]), row1(ln2_b[l]))
        i = l // 2
        if l % 2 == 0:
            fw = (ffn_wg[i].astype(BF16), ffn_wu[i].astype(BF16), ffn_wd[i].astype(BF16))
            ffn = lambda x, tm: _ffn(x, *fw, *ln2, tm=tm, alpha=alpha)
        else:
            wr = jnp.pad(moe_router[i].astype(F32), ((0, 0), (0, LANES - N_EXPERTS)))
            wr_hi, wr_lo = _split_bf16(wr)
            br = jnp.pad(moe_router_b[i].astype(F32), (0, LANES - N_EXPERTS)).reshape(1, LANES)
            fw = (moe_wg[i].astype(BF16), moe_wu[i].astype(BF16), moe_wd[i].astype(BF16))
            ffn = lambda x, tm: _moe(x, wr_hi, wr_lo, br, *fw, *ln2, tm=tm, alpha=alpha)

        ua, yb, yc, glu, qb, k, v, kb, vx = _proj_prompt(xp, w_in_b, tab_p, *sgu_ln, ws_p, sb_p, *conv_args,
                                                         seq=tp, tm=tm_p, chunk=l_p)
        s5_tab, pw_re, pw_im = _s5_mats(*s5_par, l_p)
        ya, sr_p, si_p = _s5_branch(ua, zero_state, zero_state, s5_tab, pw_re[l_p], pw_im[l_p], s5_d[l],
                                    nb=bp, t=tp)
        yd = _attn_prompt(qb, kb, vx, lamv, gsv, nb=bp, t=tp, tq=tq)
        xp = _merge(xp, ya, yb, yc, yd, mw, tm=tm_p, alpha=alpha)
        xp = ffn(xp, tm_p)
        outs[0].append(k.reshape(bp, tp, ATT_HEADS, 2, ATT_HD))
        outs[1].append(v.reshape(bp, tp, ATT_HEADS, ATT_VD))
        outs[2].append(sr_p)
        outs[3].append(si_p)
        outs[4].append(glu.reshape(bp, tp, D_CONV)[:, tp - (CONV_W - 1):])

        pfx = jnp.pad(state_conv[l].astype(F32), ((0, 0), (HALO_OFF, 0), (0, 0)))
        ua, yb, yc, glu, q, k, v, vn = _proj_sample(xs, w_in_b, tab_s, *sgu_ln, ws_s, sb_s, *conv_args, pfx,
                                                    nb=bs, tn=ts)
        ya, sr_s, si_s = _s5_branch(ua, state_ssm_re[l], state_ssm_im[l], s5_tab, pw_re[ts], pw_im[ts], s5_d[l],
                                    nb=bs, t=ts)
        yd = _attn_sample(q, k, v, cache_k, cache_v, page_table, l, lamv, gsv, nb=bs, tn=ts, pg=pg)
        xs = _merge(xs, ya, yb, yc, yd, mw, tm=n_s, alpha=alpha)
        xs = ffn(xs, n_s)
        outs[5].append(k.reshape(bs, ts, ATT_HEADS, 2, ATT_HD))
        outs[6].append(v.reshape(bs, ts, ATT_HEADS, ATT_VD))
        outs[7].append(sr_s)
        outs[8].append(si_s)
        full = jnp.concatenate([state_conv[l].astype(F32), glu.reshape(bs, ts, D_CONV)], axis=1)
        outs[9].append(full[:, ts:])
        outs[10].append(vn.reshape(bs, ts, D_SGU))

    return (xp.reshape(bp, tp, d), xs.reshape(bs, ts, d), *[jnp.stack(o) for o in outs])
```

```python
import functools
import math

import jax
import jax.numpy as jnp
from jax import lax
from jax.experimental import pallas as pl
from jax.experimental.pallas import tpu as pltpu

F32 = jnp.float32
BF16 = jnp.bfloat16

D_MODEL = 1024
PAGE_SIZE = 128
S5_GROUP = 16
S5_GROUPS = 16
D_S5 = 256
S5_STATE = 64
D_SGU = 256
SGU_HEADS = 4
CHUNK = 128
D_CONV = 256
CONV_W = 31
ATT_HEADS = 4
ATT_HD = 64
ATT_VD = 128
D_ATT = 512
ROT_DIM = 16
ROPE_THETA = 500000.0
ATT_SCALE = ATT_HD ** -0.5
LOG2E = 1.4426950408889634
NEG_INF = -1e30
D_IN = 2816
N_BRANCH = 4
N_EXPERTS = 8
TOP_K = 2
LN_EPS = 1e-5
RMS_EPS = 1e-5

LANES = 128
SUBLANES = 8
HALO = 32
HALO_OFF = HALO - (CONV_W - 1)
VMEM_LIMIT = 56 * 1024 * 1024


def _cparams(sem):
    return pltpu.CompilerParams(dimension_semantics=sem, vmem_limit_bytes=VMEM_LIMIT)


def _const_spec(shape):
    nd = len(shape)
    return pl.BlockSpec(shape, lambda *_: (0,) * nd)


def _ln(x, g, b):
    mu = jnp.mean(x, axis=-1, keepdims=True)
    xc = x - mu
    var = jnp.mean(xc * xc, axis=-1, keepdims=True)
    return xc * lax.rsqrt(var + LN_EPS) * g + b


def _sigmoid(x):
    return 1.0 / (1.0 + jnp.exp(-x))


def _silu(x):
    return x * _sigmoid(x)


def _gelu(x):
    return jax.nn.gelu(x, approximate=True)


def _mm(a, b):
    return jnp.dot(a.astype(BF16), b.astype(BF16), preferred_element_type=F32)


def _mm_nt(a, b):
    return lax.dot_general(a.astype(BF16), b.astype(BF16), (((1,), (1,)), ((), ())),
                           preferred_element_type=F32)


def _rope(x, tab):
    n_rep = x.shape[1] // LANES
    c = jnp.concatenate([tab[:, 0:LANES]] * n_rep, axis=1)
    s_hi = jnp.concatenate([tab[:, LANES:2 * LANES]] * n_rep, axis=1)
    s_lo = jnp.concatenate([tab[:, 2 * LANES:3 * LANES]] * n_rep, axis=1)
    half = ROT_DIM // 2
    x_up = pltpu.roll(x, x.shape[1] - half, 1)
    x_dn = pltpu.roll(x, half, 1)
    return x * c + x_up * s_hi + x_dn * s_lo


def _ln_kernel(x_ref, g_ref, b_ref, o_ref):
    o_ref[...] = _ln(x_ref[...], g_ref[...], b_ref[...])


def _layernorm(x, g, b, tm):
    n, d = x.shape
    return pl.pallas_call(
        _ln_kernel,
        grid=(n // tm,),
        in_specs=[pl.BlockSpec((tm, d), lambda i: (i, 0)), _const_spec((1, d)), _const_spec((1, d))],
        out_specs=pl.BlockSpec((tm, d), lambda i: (i, 0)),
        out_shape=jax.ShapeDtypeStruct((n, d), F32),
        compiler_params=_cparams(("parallel",)),
    )(x, g.reshape(1, d), b.reshape(1, d))


def _proj_split(x, w_ref, tab, sgu_g, sgu_b, q_scale):
    h = _mm(x, w_ref[...])
    ua = h[:, 0:256]
    gub = _gelu(h[:, 256:512])
    vn = _ln(_gelu(h[:, 512:768]), sgu_g, sgu_b)
    glu = h[:, 768:1024] * _sigmoid(h[:, 1024:1280])
    q = _rope(h[:, 1280:1792], tab) * q_scale
    k = _rope(h[:, 1792:2304], tab)
    v = h[:, 2304:2816]
    return ua, gub, vn, glu, q, k, v


def _sgu(gub, vn, ws_ref, sb_ref):
    tm = vn.shape[0]
    head_of_lane = lax.broadcasted_iota(jnp.int32, (CHUNK, D_SGU), 1) // (D_SGU // SGU_HEADS)
    outs = []
    for c in range(tm // CHUNK):
        vc = vn[c * CHUNK:(c + 1) * CHUNK]
        z = sb_ref[...]
        for g in range(SGU_HEADS):
            z = z + _mm(ws_ref[g], jnp.where(head_of_lane == g, vc, 0.0))
        outs.append(gub[c * CHUNK:(c + 1) * CHUNK] * z)
    return outs[0] if len(outs) == 1 else jnp.concatenate(outs, axis=0)


def _conv_post(acc, cb, g, b):
    return _silu(_ln(acc + cb, g, b))


def _to_planes(rows, tmp_ref, planes_ref):
    n_planes, per_plane = planes_ref.shape[0], planes_ref.shape[1]
    for h in range(tmp_ref.shape[0]):
        tmp_ref[h] = rows[:, h * LANES:(h + 1) * LANES]
    for i in range(n_planes):
        for h in range(tmp_ref.shape[0]):
            planes_ref[i, :, h * LANES:(h + 1) * LANES] = tmp_ref[h, pl.ds(i, per_plane, stride=n_planes), :]


def _from_planes(planes_ref, tmp_ref):
    n_planes, per_plane = planes_ref.shape[0], planes_ref.shape[1]
    for i in range(n_planes):
        for h in range(tmp_ref.shape[0]):
            tmp_ref[h, pl.ds(i, per_plane, stride=n_planes), :] = planes_ref[i, :, h * LANES:(h + 1) * LANES]
    return jnp.concatenate([tmp_ref[h] for h in range(tmp_ref.shape[0])], axis=1)


def _proj_prompt_kernel(x_ref, w_ref, tab_ref, sg_ref, sbn_ref, ws_ref, sb_ref, cw_ref, cb_ref, cg_ref, cbn_ref,
                        up_ref, yb_ref, yc_ref, glu_ref, q_ref, k_ref, v_ref, kb_ref, vx_ref,
                        ext_ref, ua_ref, *, tm, tiles_per_seq, rb):
    i = pl.program_id(0)
    ua, gub, vn, glu, q, k, v = _proj_split(x_ref[...], w_ref, tab_ref[...], sg_ref[...], sbn_ref[...],
                                            ATT_SCALE * LOG2E)
    _to_planes(ua, ua_ref, up_ref)
    glu_ref[...] = glu
    q_ref[...] = q.astype(BF16)
    k_ref[...] = k
    v_ref[...] = v
    kb_ref[...] = k.astype(BF16)
    vb = v.astype(BF16)
    ones = jnp.ones((tm, ATT_VD), BF16)
    vx_ref[...] = jnp.concatenate(
        [piece for h in range(ATT_HEADS) for piece in (vb[:, h * ATT_VD:(h + 1) * ATT_VD], ones)], axis=1)
    yb_ref[...] = _sgu(gub, vn, ws_ref, sb_ref)

    @pl.when(i % tiles_per_seq == 0)
    def _():
        ext_ref[0:HALO, :] = jnp.zeros((HALO, D_CONV), F32)

    ext_ref[HALO:HALO + tm, :] = glu
    for r in range(tm // rb):
        acc = jnp.zeros((rb, D_CONV), F32)
        for w in range(CONV_W):
            acc = acc + ext_ref[pl.ds(r * rb + HALO_OFF + w, rb), :] * cw_ref[w:w + 1, :]
        yc_ref[r * rb:(r + 1) * rb, :] = _conv_post(acc, cb_ref[...], cg_ref[...], cbn_ref[...])
    ext_ref[0:HALO, :] = ext_ref[tm:tm + HALO, :]


def _proj_prompt(x, w_in, tab, sgu_g, sgu_b, ws, sb, cw, cb, cg, cbn, *, seq, tm, chunk):
    n = x.shape[0]
    tiles_per_seq = seq // tm
    row = lambda width: pl.BlockSpec((tm, width), lambda i: (i, 0))
    f = lambda width, dt=F32: jax.ShapeDtypeStruct((n, width), dt)
    planes_spec = pl.BlockSpec((chunk, tm // chunk, D_S5), lambda i: (0, i, 0))
    planes_shape = jax.ShapeDtypeStruct((chunk, n // chunk, D_S5), F32)
    kern = functools.partial(_proj_prompt_kernel, tm=tm, tiles_per_seq=tiles_per_seq, rb=64)
    return pl.pallas_call(
        kern,
        grid=(n // tm,),
        in_specs=[row(D_MODEL), _const_spec((D_MODEL, D_IN)),
                  pl.BlockSpec((tm, 3 * LANES), lambda i: (i % tiles_per_seq, 0)),
                  _const_spec((1, D_SGU)), _const_spec((1, D_SGU)),
                  _const_spec((SGU_HEADS, CHUNK, CHUNK)), _const_spec((CHUNK, D_SGU)),
                  _const_spec((CONV_W, D_CONV)), _const_spec((1, D_CONV)), _const_spec((1, D_CONV)),
                  _const_spec((1, D_CONV))],
        out_specs=[planes_spec, row(D_SGU), row(D_CONV), row(D_CONV), row(D_ATT), row(D_ATT), row(D_ATT),
                   row(D_ATT), row(2 * D_ATT)],
        out_shape=[planes_shape, f(D_SGU), f(D_CONV), f(D_CONV), f(D_ATT, BF16), f(D_ATT), f(D_ATT),
                   f(D_ATT, BF16), f(2 * D_ATT, BF16)],
        scratch_shapes=[pltpu.VMEM((tm + HALO, D_CONV), F32), pltpu.VMEM((D_S5 // LANES, tm, LANES), F32)],
        compiler_params=_cparams(("arbitrary",)),
    )(x, w_in, tab, sgu_g, sgu_b, ws, sb, cw, cb, cg, cbn)


def _proj_sample_kernel(x_ref, w_ref, tab_ref, sg_ref, sbn_ref, ws_ref, sb_ref, cw_ref, cb_ref, cg_ref, cbn_ref,
                        pfx_ref,
                        up_ref, yb_ref, yc_ref, glu_ref, q_ref, k_ref, v_ref, vn_ref,
                        ext_ref, ua_ref, *, nb, tn):
    ua, gub, vn, glu, q, k, v = _proj_split(x_ref[...], w_ref, tab_ref[...], sg_ref[...], sbn_ref[...], ATT_SCALE)
    _to_planes(ua, ua_ref, up_ref)
    glu_ref[...] = glu
    q_ref[...] = q
    k_ref[...] = k
    v_ref[...] = v
    vn_ref[...] = vn
    yb_ref[...] = _sgu(gub, vn, ws_ref, sb_ref)

    ext_ref[:, 0:HALO, :] = pfx_ref[...]
    ext_ref[:, HALO:HALO + tn, :] = glu.reshape(nb, tn, D_CONV)
    acc = jnp.zeros((nb, tn, D_CONV), F32)
    for w in range(CONV_W):
        acc = acc + ext_ref[:, pl.ds(HALO_OFF + w, tn), :] * cw_ref[w:w + 1, :]
    yc_ref[...] = _conv_post(acc.reshape(nb * tn, D_CONV), cb_ref[...], cg_ref[...], cbn_ref[...])


def _proj_sample(x, w_in, tab, sgu_g, sgu_b, ws, sb, cw, cb, cg, cbn, pfx, *, nb, tn):
    n = x.shape[0]
    full = lambda width: _const_spec((n, width))
    f = lambda width: jax.ShapeDtypeStruct((n, width), F32)
    kern = functools.partial(_proj_sample_kernel, nb=nb, tn=tn)
    return pl.pallas_call(
        kern,
        grid=(1,),
        in_specs=[full(D_MODEL), _const_spec((D_MODEL, D_IN)), full(3 * LANES),
                  _const_spec((1, D_SGU)), _const_spec((1, D_SGU)),
                  _const_spec((SGU_HEADS, CHUNK, CHUNK)), _const_spec((CHUNK, D_SGU)),
                  _const_spec((CONV_W, D_CONV)), _const_spec((1, D_CONV)), _const_spec((1, D_CONV)),
                  _const_spec((1, D_CONV)), _const_spec((nb, HALO, D_CONV))],
        out_specs=[_const_spec((tn, nb, D_S5)), full(D_SGU), full(D_CONV), full(D_CONV), full(D_ATT), full(D_ATT),
                   full(D_ATT), full(D_SGU)],
        out_shape=[jax.ShapeDtypeStruct((tn, nb, D_S5), F32), f(D_SGU), f(D_CONV), f(D_CONV), f(D_ATT), f(D_ATT),
                   f(D_ATT), f(D_SGU)],
        scratch_shapes=[pltpu.VMEM((nb, HALO + tn, D_CONV), F32), pltpu.VMEM((D_S5 // LANES, n, LANES), F32)],
        compiler_params=_cparams(("arbitrary",)),
    )(x, w_in, tab, sgu_g, sgu_b, ws, sb, cw, cb, cg, cbn, pfx)


def _s5_mats(lam_re, lam_im, log_dt, b_re, b_im, c_re, c_im, L):
    G, P, C = S5_GROUPS, S5_STATE, S5_GROUP
    lr, li = lam_re.astype(F32), lam_im.astype(F32)
    dt = jnp.exp(log_dt.astype(F32))[:, None]
    steps = jnp.arange(L + 1, dtype=F32)[:, None, None]
    mag = jnp.exp((lr * dt)[None] * steps)
    pr = mag * jnp.cos((li * dt)[None] * steps)
    pi = mag * jnp.sin((li * dt)[None] * steps)
    nr, ni = pr[1] - 1.0, pi[1]
    den = lr * lr + li * li
    fr = (nr * lr + ni * li) / den
    fi = (ni * lr - nr * li) / den
    bre, bim = b_re.astype(F32), b_im.astype(F32)
    br = fr[..., None] * bre - fi[..., None] * bim
    bi = fr[..., None] * bim + fi[..., None] * bre
    cr, ci = c_re.astype(F32), c_im.astype(F32)
    kr = (jnp.einsum('gcp,tgp,gpd->tgdc', cr, pr[:L], br) - jnp.einsum('gcp,tgp,gpd->tgdc', cr, pi[:L], bi)
          - jnp.einsum('gcp,tgp,gpd->tgdc', ci, pr[:L], bi) - jnp.einsum('gcp,tgp,gpd->tgdc', ci, pi[:L], br))

    def block_diag(compact):
        _, _, r, c = compact.shape
        repeat = jnp.tile(jnp.eye(c, dtype=F32), (1, G))
        tiled = jnp.einsum('lrc,cq->lrq', compact.reshape(L, G * r, c), repeat, precision=lax.Precision.HIGHEST)
        row_g = lax.broadcasted_iota(jnp.int32, (G * r, G * c), 0) // r
        col_g = lax.broadcasted_iota(jnp.int32, (G * r, G * c), 1) // c
        return jnp.where(row_g == col_g, tiled, 0.0).astype(BF16)

    kbd = block_diag(kr)
    rev = L - 1 - jnp.arange(L)
    brt, bit = br.transpose(0, 2, 1), bi.transpose(0, 2, 1)
    m1re = block_diag(pr[rev][:, :, None, :] * brt[None] - pi[rev][:, :, None, :] * bit[None])
    m1im = block_diag(pr[rev][:, :, None, :] * bit[None] + pi[rev][:, :, None, :] * brt[None])
    crt, cit = cr.transpose(0, 2, 1), ci.transpose(0, 2, 1)
    nr, ni = pr[1:L + 1][..., None], pi[1:L + 1][..., None]
    m2re = block_diag(crt[None] * nr - cit[None] * ni)
    m2im = block_diag(-(crt[None] * ni + cit[None] * nr))
    return (kbd, m1re, m1im, m2re, m2im), pr.reshape(L + 1, 1, G * P), pi.reshape(L + 1, 1, G * P)


def _s5_local_kernel(up_ref, m1re_ref, m1im_ref, slre_ref, slim_ref):
    @pl.when(pl.program_id(0) == 0)
    def _():
        slre_ref[...] = jnp.zeros_like(slre_ref)
        slim_ref[...] = jnp.zeros_like(slim_ref)

    u = up_ref[...].astype(BF16)
    slre_ref[...] += jnp.dot(u, m1re_ref[...], preferred_element_type=F32)
    slim_ref[...] += jnp.dot(u, m1im_ref[...], preferred_element_type=F32)


def _s5_scan_kernel(slre_ref, slim_ref, s0re_ref, s0im_ref, are_ref, aim_ref,
                    stre_ref, stim_ref, fre_ref, fim_ref, *, nb, cpb):
    ar = are_ref[...]
    ai = aim_ref[...]

    def body(c, carry):
        out = []
        for b in range(nb):
            sr, si = carry[b]
            row = pl.ds(b * cpb + c, 1)
            stre_ref[row, :] = sr
            stim_ref[row, :] = si
            out.append((ar * sr - ai * si + slre_ref[row, :], ar * si + ai * sr + slim_ref[row, :]))
        return tuple(out)

    init = tuple((s0re_ref[b:b + 1, :], s0im_ref[b:b + 1, :]) for b in range(nb))
    fin = lax.fori_loop(0, cpb, body, init)
    for b in range(nb):
        fre_ref[b:b + 1, :] = fin[b][0]
        fim_ref[b:b + 1, :] = fin[b][1]


def _s5_out_kernel(up_ref, kbd_ref, stre_ref, stim_ref, m2re_ref, m2im_ref, d_ref, o_ref):
    i = pl.program_id(0)
    o_ref[...] = (_mm(stre_ref[...], m2re_ref[...]) + _mm(stim_ref[...], m2im_ref[...])
                  + d_ref[...] * up_ref[i])

    def body(tau, carry):
        o_ref[...] += _mm(up_ref[i - tau], kbd_ref[tau])
        return carry

    lax.fori_loop(0, i + 1, body, 0)
    o_ref[...] = _gelu(o_ref[...])


def _s5_branch(up, s0_re, s0_im, tables, a_re, a_im, d_skip, *, nb, t):
    kbd, m1re, m1im, m2re, m2im = tables
    L, ch, dc = up.shape
    l_tab = kbd.shape[0]
    m1_off = l_tab - L
    G, P = S5_GROUPS, S5_STATE
    gp = G * P
    cpb = t // L
    acc_spec = _const_spec((ch, gp))
    acc_shape = jax.ShapeDtypeStruct((ch, gp), F32)
    slre, slim = pl.pallas_call(
        _s5_local_kernel,
        grid=(L,),
        in_specs=[pl.BlockSpec((None, ch, dc), lambda j: (j, 0, 0)),
                  pl.BlockSpec((None, dc, gp), lambda j: (j + m1_off, 0, 0)),
                  pl.BlockSpec((None, dc, gp), lambda j: (j + m1_off, 0, 0))],
        out_specs=[acc_spec, acc_spec],
        out_shape=[acc_shape, acc_shape],
        compiler_params=_cparams(("arbitrary",)),
    )(up, m1re, m1im)
    fin_shape = jax.ShapeDtypeStruct((nb, gp), F32)
    stre, stim, fre, fim = pl.pallas_call(
        functools.partial(_s5_scan_kernel, nb=nb, cpb=cpb),
        out_shape=[acc_shape, acc_shape, fin_shape, fin_shape],
        compiler_params=pltpu.CompilerParams(vmem_limit_bytes=VMEM_LIMIT),
    )(slre, slim, s0_re.reshape(nb, gp).astype(F32), s0_im.reshape(nb, gp).astype(F32), a_re, a_im)
    yp = pl.pallas_call(
        _s5_out_kernel,
        grid=(L,),
        in_specs=[_const_spec((L, ch, dc)), _const_spec((l_tab, dc, dc)), acc_spec, acc_spec,
                  pl.BlockSpec((None, gp, dc), lambda i: (i, 0, 0)),
                  pl.BlockSpec((None, gp, dc), lambda i: (i, 0, 0)),
                  _const_spec((1, dc))],
        out_specs=pl.BlockSpec((None, ch, dc), lambda i: (i, 0, 0)),
        out_shape=jax.ShapeDtypeStruct((L, ch, dc), F32),
        compiler_params=_cparams(("parallel",)),
    )(up, kbd, stre, stim, m2re, m2im, d_skip.astype(F32).reshape(1, dc))
    return yp, fre.reshape(nb, G, P), fim.reshape(nb, G, P)


def _diff_finish(o1, o2, lam, gs):
    o = o1 - lam * o2
    return o * lax.rsqrt(jnp.mean(o * o, axis=-1, keepdims=True) + RMS_EPS) * gs


def _attn_prompt_kernel(q_ref, k_ref, v_ref, lam_ref, gs_ref, o_ref, q2_ref, m_ref, acc_ref, *, tq, rb, unroll):
    i = pl.program_id(2)
    r2 = 2 * tq
    q = q_ref[...]
    lane = lax.broadcasted_iota(jnp.int32, (tq, ATT_VD), 1)
    zero = jnp.zeros_like(q)
    q2_ref[0:tq, :] = jnp.where(lane < ATT_HD, q, zero)
    q2_ref[tq:r2, :] = jnp.where(lane >= ATT_HD, q, zero)
    m_ref[...] = jnp.full((r2, LANES), NEG_INF, F32)
    acc_ref[...] = jnp.zeros((r2, 2 * ATT_VD), F32)

    def step(j, diagonal):
        off = pl.multiple_of(j * tq, tq)
        kt = k_ref[pl.ds(off, tq), :]
        vt = v_ref[pl.ds(off, tq), :]
        for r in range(r2 // rb):
            rows = slice(r * rb, (r + 1) * rb)
            s = _mm_nt(q2_ref[rows, :], kt)
            if diagonal:
                row = (lax.broadcasted_iota(jnp.int32, (rb, tq), 0) + r * rb) % tq
                col = lax.broadcasted_iota(jnp.int32, (rb, tq), 1)
                s = jnp.where(col <= row, s, NEG_INF)
            m_prev = m_ref[rows, :]
            m_new = jnp.maximum(m_prev, jnp.max(s, axis=-1, keepdims=True))
            alpha = jnp.exp2(m_prev - m_new)
            p = jnp.exp2(s - jnp.concatenate([m_new] * (tq // LANES), axis=1))
            acc_ref[rows, :] = jnp.concatenate([alpha, alpha], axis=1) * acc_ref[rows, :] + _mm(p, vt)
            m_ref[rows, :] = m_new

    def group(g, carry):
        for u in range(unroll):
            step(g * unroll + u, False)
        return carry

    def single(j, carry):
        step(j, False)
        return carry

    n_grp = i // unroll
    lax.fori_loop(0, n_grp, group, 0)
    lax.fori_loop(n_grp * unroll, i, single, 0)
    step(i, True)
    acc = acc_ref[...]
    o = acc[:, 0:ATT_VD] / acc[:, ATT_VD:]
    o_ref[...] = _diff_finish(o[0:tq], o[tq:r2], lam_ref[...], gs_ref[...])


def _attn_prompt(qb, kb, vx, lamv, gsv, *, nb, t, tq):
    n = nb * t
    nq = t // tq
    return pl.pallas_call(
        functools.partial(_attn_prompt_kernel, tq=tq, rb=min(128, tq), unroll=4),
        grid=(nb, ATT_HEADS, nq),
        in_specs=[pl.BlockSpec((tq, ATT_VD), lambda b, h, i: (b * nq + i, h)),
                  pl.BlockSpec((t, ATT_VD), lambda b, h, i: (b, h)),
                  pl.BlockSpec((t, 2 * ATT_VD), lambda b, h, i: (b, h)),
                  _const_spec((1, ATT_VD)), _const_spec((1, ATT_VD))],
        out_specs=pl.BlockSpec((tq, ATT_VD), lambda b, h, i: (b * nq + i, h)),
        out_shape=jax.ShapeDtypeStruct((n, D_ATT), F32),
        scratch_shapes=[pltpu.VMEM((2 * tq, ATT_VD), BF16), pltpu.VMEM((2 * tq, LANES), F32),
                        pltpu.VMEM((2 * tq, 2 * ATT_VD), F32)],
        compiler_params=_cparams(("parallel", "parallel", "arbitrary")),
    )(qb, kb, vx, lamv, gsv)


def _attn_sample_kernel(pt_ref, q_ref, kn_ref, vn_ref, lam_ref, gs_ref, *rest, tn, pg):
    k_refs = rest[0:pg]
    v_refs = rest[pg:2 * pg]
    o_ref = rest[2 * pg]
    qbd_ref, m_ref, l_ref, acc_ref = rest[2 * pg + 1:]
    p_idx = pl.program_id(1)
    rows_h = 2 * tn
    n_rows = ATT_HEADS * rows_h

    def update(s, v_pages):
        n_rep = s.shape[1] // LANES
        m_prev = m_ref[...]
        m_new = jnp.maximum(m_prev, jnp.max(s, axis=-1, keepdims=True))
        p = jnp.exp(s - jnp.concatenate([m_new] * n_rep, axis=1))
        alpha = jnp.exp(m_prev - m_new)
        l_ref[...] = alpha * l_ref[...] + jnp.sum(p, axis=-1, keepdims=True)
        for h in range(ATT_HEADS):
            rows = slice(h * rows_h, (h + 1) * rows_h)
            v_h = [r[pl.ds(h, PAGE_SIZE, stride=ATT_HEADS), :].astype(BF16) for r in v_pages]
            v_h = v_h[0] if len(v_h) == 1 else jnp.concatenate(v_h, axis=0)
            acc_ref[rows, :] = alpha[rows] * acc_ref[rows, :] + _mm(p[rows], v_h)
        m_ref[...] = m_new

    @pl.when(p_idx == 0)
    def _():
        q = q_ref[...]
        lane_grp = lax.broadcasted_iota(jnp.int32, (tn, D_ATT), 1) // ATT_HD
        pieces = [jnp.where(lane_grp == g, q, 0.0) for g in range(2 * ATT_HEADS)]
        qbd_ref[...] = jnp.concatenate(pieces, axis=0).astype(BF16)
        m_ref[...] = jnp.full((n_rows, LANES), NEG_INF, F32)
        l_ref[...] = jnp.zeros((n_rows, LANES), F32)
        acc_ref[...] = jnp.zeros((n_rows, ATT_VD), F32)
        s = _mm(qbd_ref[...], kn_ref[...])
        t_of_row = lax.broadcasted_iota(jnp.int32, (n_rows, PAGE_SIZE), 0) % tn
        col = lax.broadcasted_iota(jnp.int32, (n_rows, PAGE_SIZE), 1)
        update(jnp.where(col <= t_of_row, s, NEG_INF), [vn_ref])

    kcat = jnp.concatenate([r[...].astype(BF16) for r in k_refs], axis=1)
    update(_mm(qbd_ref[...], kcat), list(v_refs))

    @pl.when(p_idx == pl.num_programs(1) - 1)
    def _():
        o = acc_ref[...] / l_ref[...]
        outs = []
        for h in range(ATT_HEADS):
            o1 = o[h * rows_h:h * rows_h + tn]
            o2 = o[h * rows_h + tn:(h + 1) * rows_h]
            outs.append(_diff_finish(o1, o2, lam_ref[...], gs_ref[...]))
        o_ref[...] = jnp.concatenate(outs, axis=1)


def _attn_sample(q, k_new, v_new, cache_k, cache_v, page_table, layer, lamv, gsv, *, nb, tn, pg):
    n_pages = page_table.shape[1]
    depth, n_pool = cache_k.shape[0], cache_k.shape[1]
    rows_v = PAGE_SIZE * ATT_HEADS
    ck = jnp.transpose(cache_k, (0, 1, 3, 4, 5, 2)).reshape(depth, n_pool, D_ATT, PAGE_SIZE)
    cv = cache_v.reshape(depth, n_pool, rows_v, ATT_VD)
    kn = jnp.pad(k_new.reshape(nb, tn, D_ATT).transpose(0, 2, 1), ((0, 0), (0, 0), (0, PAGE_SIZE - tn)))
    vn = jnp.pad(v_new.reshape(nb, tn * ATT_HEADS, ATT_VD), ((0, 0), (0, rows_v - tn * ATT_HEADS), (0, 0)))
    n_rows = 2 * ATT_HEADS * tn

    def page_spec(r, rows, cols):
        return pl.BlockSpec((None, None, rows, cols), lambda b, p, pt: (layer, pt[b, p * pg + r], 0, 0))

    grid_spec = pltpu.PrefetchScalarGridSpec(
        num_scalar_prefetch=1,
        grid=(nb, n_pages // pg),
        in_specs=[pl.BlockSpec((tn, D_ATT), lambda b, p, pt: (b, 0)),
                  pl.BlockSpec((None, D_ATT, PAGE_SIZE), lambda b, p, pt: (b, 0, 0)),
                  pl.BlockSpec((None, rows_v, ATT_VD), lambda b, p, pt: (b, 0, 0)),
                  pl.BlockSpec((1, ATT_VD), lambda b, p, pt: (0, 0)),
                  pl.BlockSpec((1, ATT_VD), lambda b, p, pt: (0, 0))]
                 + [page_spec(r, D_ATT, PAGE_SIZE) for r in range(pg)]
                 + [page_spec(r, rows_v, ATT_VD) for r in range(pg)],
        out_specs=pl.BlockSpec((tn, D_ATT), lambda b, p, pt: (b, 0)),
        scratch_shapes=[pltpu.VMEM((n_rows, D_ATT), BF16), pltpu.VMEM((n_rows, LANES), F32),
                        pltpu.VMEM((n_rows, LANES), F32), pltpu.VMEM((n_rows, ATT_VD), F32)],
    )
    return pl.pallas_call(
        functools.partial(_attn_sample_kernel, tn=tn, pg=pg),
        grid_spec=grid_spec,
        out_shape=jax.ShapeDtypeStruct((nb * tn, D_ATT), F32),
        compiler_params=_cparams(("parallel", "arbitrary")),
    )(page_table, q, kn, vn, lamv, gsv, *([ck] * pg), *([cv] * pg))


def _merge_kernel(x_ref, ya_ref, yb_ref, yc_ref, yd_ref, wglu_ref, wgate_ref, bgate_ref,
                  wa_ref, wb_ref, wc_ref, wd_ref, wo_ref, g_ref, b_ref, o_ref, ya_rows_ref, *, alpha):
    x = x_ref[...]
    xb = x.astype(BF16)
    hg = _mm(_from_planes(ya_ref, ya_rows_ref), wglu_ref[...])
    ya = hg[:, :D_S5] * _sigmoid(hg[:, D_S5:])
    merged = None
    for idx, (y, w_ref) in enumerate(((ya, wa_ref), (yb_ref[...], wb_ref), (yc_ref[...], wc_ref),
                                      (yd_ref[...], wd_ref))):
        cols = slice(idx * D_MODEL, (idx + 1) * D_MODEL)
        gate = _sigmoid(_mm(xb, wgate_ref[:, cols]) + bgate_ref[:, cols])
        term = gate * _mm(y, w_ref[...])
        merged = term if merged is None else merged + term
    out = _mm(merged, wo_ref[...])
    o_ref[...] = _ln(alpha * x + out, g_ref[...], b_ref[...])


def _merge(x, ya, yb, yc, yd, w, *, tm, alpha):
    n = x.shape[0]
    chunk = ya.shape[0]
    row = lambda width: pl.BlockSpec((tm, width), lambda i: (i, 0))
    return pl.pallas_call(
        functools.partial(_merge_kernel, alpha=alpha),
        grid=(n // tm,),
        in_specs=[row(D_MODEL), pl.BlockSpec((chunk, tm // chunk, D_S5), lambda i: (0, i, 0)),
                  row(D_SGU), row(D_CONV), row(D_ATT),
                  _const_spec((D_S5, 2 * D_S5)), _const_spec((D_MODEL, N_BRANCH * D_MODEL)),
                  _const_spec((1, N_BRANCH * D_MODEL)),
                  _const_spec((D_S5, D_MODEL)), _const_spec((D_SGU, D_MODEL)), _const_spec((D_CONV, D_MODEL)),
                  _const_spec((D_ATT, D_MODEL)), _const_spec((D_MODEL, D_MODEL)),
                  _const_spec((1, D_MODEL)), _const_spec((1, D_MODEL))],
        out_specs=row(D_MODEL),
        out_shape=jax.ShapeDtypeStruct((n, D_MODEL), F32),
        scratch_shapes=[pltpu.VMEM((D_S5 // LANES, tm, LANES), F32)],
        compiler_params=_cparams(("parallel",)),
    )(x, ya, yb, yc, yd, w['w_glu'], w['w_gate'], w['b_gate'], w['w_br_a'], w['w_br_b'], w['w_br_c'],
      w['w_br_d'], w['w_o'], w['ln1_g'], w['ln1_b'])


def _ffn_kernel(x_ref, wg_ref, wu_ref, wd_ref, g_ref, b_ref, o_ref, *, alpha):
    x = x_ref[...]
    xb = x.astype(BF16)
    hid = _silu(_mm(xb, wg_ref[...])) * _mm(xb, wu_ref[...])
    o_ref[...] = _ln(alpha * x + _mm(hid, wd_ref[...]), g_ref[...], b_ref[...])


def _ffn(x, wg, wu, wd, g, b, *, tm, alpha):
    n = x.shape[0]
    dff = wg.shape[1]
    row = pl.BlockSpec((tm, D_MODEL), lambda i: (i, 0))
    return pl.pallas_call(
        functools.partial(_ffn_kernel, alpha=alpha),
        grid=(n // tm,),
        in_specs=[row, _const_spec((D_MODEL, dff)), _const_spec((D_MODEL, dff)), _const_spec((dff, D_MODEL)),
                  _const_spec((1, D_MODEL)), _const_spec((1, D_MODEL))],
        out_specs=row,
        out_shape=jax.ShapeDtypeStruct((n, D_MODEL), F32),
        compiler_params=_cparams(("parallel",)),
    )(x, wg, wu, wd, g, b)


def _split_bf16(a):
    hi = a.astype(BF16)
    lo = (a - hi.astype(F32)).astype(BF16)
    return hi, lo


SEL_I1, SEL_I2, SEL_W1, SEL_W2 = 0, 1, 2, 3
MOE_TM = 512


def _moe_route_kernel(x_ref, wr_hi_ref, wr_lo_ref, br_ref, sel_ref):
    x = x_ref[...]
    tm = x.shape[0]
    lane = lax.broadcasted_iota(jnp.int32, (tm, LANES), 1)
    x_hi, x_lo = _split_bf16(x)
    logits = (jnp.dot(x_hi, wr_hi_ref[...], preferred_element_type=F32)
              + jnp.dot(x_hi, wr_lo_ref[...], preferred_element_type=F32)
              + jnp.dot(x_lo, wr_hi_ref[...], preferred_element_type=F32)) + br_ref[...]
    logits = jnp.where(lane < N_EXPERTS, logits, -jnp.inf)
    lane_f = lane.astype(F32)
    v1 = jnp.max(logits, axis=-1, keepdims=True)
    i1 = jnp.min(jnp.where(logits == v1, lane_f, float(LANES)), axis=-1, keepdims=True)
    rest = jnp.where(lane_f == i1, -jnp.inf, logits)
    v2 = jnp.max(rest, axis=-1, keepdims=True)
    i2 = jnp.min(jnp.where(rest == v2, lane_f, float(LANES)), axis=-1, keepdims=True)
    e2 = jnp.exp(v2 - v1)
    w1 = 1.0 / (1.0 + e2)
    w2 = e2 / (1.0 + e2)
    sel_ref[...] = (jnp.where(lane == SEL_I1, i1, 0.0) + jnp.where(lane == SEL_I2, i2, 0.0)
                    + jnp.where(lane == SEL_W1, w1, 0.0) + jnp.where(lane == SEL_W2, w2, 0.0))


def _moe_expert_kernel(item_e, item_tile, item_flag, starts, idx_ref, idx_next_ref, idx_prev_ref, x_hbm,
                       wg_ref, wu_ref, wd_ref, y_hbm, xbuf, obuf, gsem, ssem, *, n_tiles, n_tok):
    i = pl.program_id(0)
    tm = xbuf.shape[1]
    flag = item_flag[i]
    valid = (flag & 1) == 1
    first = (flag & 2) == 2
    last = (flag & 4) == 4
    t = item_tile[i]
    e = item_e[i]
    slot = t % 2
    other = 1 - slot

    def for_rows(body, unrolled):
        if unrolled:
            for r in range(tm):
                body(r)
        else:
            def step(r, carry):
                body(r)
                return carry
            lax.fori_loop(0, tm, step, 0, unroll=8)

    def gather_rows(idx, dst_slot, unrolled=False):
        def body(r):
            a = idx[0, r]
            tok = jnp.where(a >= n_tok, a - n_tok, a)
            pltpu.make_async_copy(x_hbm.at[pl.ds(tok, 1)], xbuf.at[dst_slot, pl.ds(r, 1)], gsem.at[dst_slot]).start()
        for_rows(body, unrolled)

    def scatter_rows(idx, src_slot, unrolled=False):
        def body(r):
            dst = idx[0, r]
            pltpu.make_async_copy(obuf.at[src_slot, pl.ds(r, 1)], y_hbm.at[pl.ds(dst, 1)], ssem.at[src_slot]).start()
        for_rows(body, unrolled)

    def wait_rows(buf, sem, s):
        pltpu.make_async_copy(buf.at[s], buf.at[s], sem.at[s]).wait()

    def compute():
        xb = xbuf[slot].astype(BF16)
        hid = _silu(_mm(xb, wg_ref[...])) * _mm(xb, wu_ref[...])
        y = _mm(hid, wd_ref[...])
        pos = t * tm + lax.broadcasted_iota(jnp.int32, (tm, 1), 0)
        mine = (pos >= starts[e]) & (pos < starts[e + 1])
        obuf[slot] += jnp.where(mine, y, 0.0)

    @pl.when(valid & first)
    def _():
        @pl.when(t == 0)
        def _():
            gather_rows(idx_ref, 0)

        wait_rows(xbuf, gsem, slot)

        @pl.when(t >= 2)
        def _():
            wait_rows(obuf, ssem, slot)

        obuf[slot] = jnp.zeros((tm, D_MODEL), F32)

    has_next = t + 1 < n_tiles
    has_prev = t >= 1
    interior = valid & last & has_next & has_prev

    @pl.when(interior)
    def _():
        gather_rows(idx_next_ref, other, unrolled=True)
        scatter_rows(idx_prev_ref, other, unrolled=True)
        compute()

    @pl.when(valid & jnp.logical_not(interior))
    def _():
        @pl.when(last & has_next)
        def _():
            gather_rows(idx_next_ref, other)

        @pl.when(last & has_prev)
        def _():
            scatter_rows(idx_prev_ref, other)

        compute()

    @pl.when(i == pl.num_programs(0) - 1)
    def _():
        scatter_rows(idx_ref, (n_tiles - 1) % 2)
        for tt in range(max(0, n_tiles - 2), n_tiles):
            wait_rows(obuf, ssem, tt % 2)


def _moe_combine_kernel(x_ref, y1_ref, y2_ref, sel_ref, g_ref, b_ref, o_ref, *, alpha):
    sel = sel_ref[...]
    lane = lax.broadcasted_iota(jnp.int32, sel.shape, 1)
    w1 = jnp.sum(jnp.where(lane == SEL_W1, sel, 0.0), axis=-1, keepdims=True)
    w2 = jnp.sum(jnp.where(lane == SEL_W2, sel, 0.0), axis=-1, keepdims=True)
    y = w1 * y1_ref[...] + w2 * y2_ref[...]
    o_ref[...] = _ln(alpha * x_ref[...] + y, g_ref[...], b_ref[...])


def _moe(x, wr_hi, wr_lo, br, wg, wu, wd, g, b, *, tm, alpha):
    n = x.shape[0]
    dfe = wg.shape[2]
    n_asg = TOP_K * n
    tmx = min(MOE_TM, n_asg)
    assert n_asg % tmx == 0
    n_tiles = n_asg // tmx
    n_items = n_tiles + N_EXPERTS - 1
    row = pl.BlockSpec((tm, D_MODEL), lambda i: (i, 0))

    sel = pl.pallas_call(
        _moe_route_kernel,
        grid=(n // tm,),
        in_specs=[row, _const_spec((D_MODEL, LANES)), _const_spec((D_MODEL, LANES)), _const_spec((1, LANES))],
        out_specs=pl.BlockSpec((tm, LANES), lambda i: (i, 0)),
        out_shape=jax.ShapeDtypeStruct((n, LANES), F32),
        compiler_params=_cparams(("parallel",)),
    )(x, wr_hi, wr_lo, br)

    experts = sel[:, SEL_I1:SEL_I2 + 1].astype(jnp.int32).T.reshape(n_asg)
    keys = jnp.sort(experts * n_asg + jnp.arange(n_asg, dtype=jnp.int32))
    idx = (keys % n_asg).reshape(n_tiles, 1, tmx)
    bounds = jnp.arange(N_EXPERTS + 1, dtype=jnp.int32) * n_asg
    starts = jnp.sum((keys[None, :] < bounds[:, None]).astype(jnp.int32), axis=1)
    counts = starts[1:] - starts[:-1]
    first_tile = starts[:-1] // tmx
    last_tile = (starts[1:] - 1) // tmx
    per_e = jnp.where(counts > 0, last_tile - first_tile + 1, 0)
    ends = jnp.cumsum(per_e)
    total = ends[-1]
    it = jnp.arange(n_items, dtype=jnp.int32)
    e_raw = jnp.sum((ends[None, :] <= it[:, None]).astype(jnp.int32), axis=1)
    e_last = jnp.max(jnp.where(counts > 0, jnp.arange(N_EXPERTS, dtype=jnp.int32), 0))
    valid = it < total
    item_e = jnp.where(valid, jnp.minimum(e_raw, N_EXPERTS - 1), e_last).astype(jnp.int32)
    item_tile = jnp.where(valid, first_tile[item_e] + it - (ends[item_e] - per_e[item_e]), n_tiles - 1)
    item_tile = item_tile.astype(jnp.int32)
    prev_tile = jnp.concatenate([jnp.full((1,), -1, jnp.int32), item_tile[:-1]])
    next_tile = jnp.concatenate([item_tile[1:], jnp.full((1,), -1, jnp.int32)])
    next_valid = jnp.concatenate([valid[1:], jnp.zeros((1,), bool)])
    is_first = item_tile != prev_tile
    is_last = (item_tile != next_tile) | ~next_valid
    item_flag = (valid.astype(jnp.int32) + 2 * is_first.astype(jnp.int32) + 4 * is_last.astype(jnp.int32))

    smem_idx = lambda f: pl.BlockSpec((None, 1, tmx), f, memory_space=pltpu.SMEM)
    grid_spec = pltpu.PrefetchScalarGridSpec(
        num_scalar_prefetch=4,
        grid=(n_items,),
        in_specs=[smem_idx(lambda i, ie, itl, ifl, st: (itl[i], 0, 0)),
                  smem_idx(lambda i, ie, itl, ifl, st: (jnp.minimum(itl[i] + 1, n_tiles - 1), 0, 0)),
                  smem_idx(lambda i, ie, itl, ifl, st: (jnp.maximum(itl[i] - 1, 0), 0, 0)),
                  pl.BlockSpec(memory_space=pl.ANY),
                  pl.BlockSpec((None, D_MODEL, dfe), lambda i, ie, itl, ifl, st: (ie[i], 0, 0)),
                  pl.BlockSpec((None, D_MODEL, dfe), lambda i, ie, itl, ifl, st: (ie[i], 0, 0)),
                  pl.BlockSpec((None, dfe, D_MODEL), lambda i, ie, itl, ifl, st: (ie[i], 0, 0))],
        out_specs=pl.BlockSpec(memory_space=pl.ANY),
        scratch_shapes=[pltpu.VMEM((2, tmx, D_MODEL), F32), pltpu.VMEM((2, tmx, D_MODEL), F32),
                        pltpu.SemaphoreType.DMA((2,)), pltpu.SemaphoreType.DMA((2,))],
    )
    y = pl.pallas_call(
        functools.partial(_moe_expert_kernel, n_tiles=n_tiles, n_tok=n),
        grid_spec=grid_spec,
        out_shape=jax.ShapeDtypeStruct((n_asg, D_MODEL), F32),
        compiler_params=_cparams(("arbitrary",)),
    )(item_e, item_tile, item_flag, starts, idx, idx, idx, x, wg, wu, wd)

    return pl.pallas_call(
        functools.partial(_moe_combine_kernel, alpha=alpha),
        grid=(n // tm,),
        in_specs=[row, row, pl.BlockSpec((tm, D_MODEL), lambda i: (i + n // tm, 0)),
                  pl.BlockSpec((tm, LANES), lambda i: (i, 0)), _const_spec((1, D_MODEL)), _const_spec((1, D_MODEL))],
        out_specs=row,
        out_shape=jax.ShapeDtypeStruct((n, D_MODEL), F32),
        compiler_params=_cparams(("parallel",)),
    )(x, y, y, sel, g, b)


def _rope_table(pos):
    half = ROT_DIM // 2
    inv = ROPE_THETA ** (-jnp.arange(half, dtype=F32) * (2.0 / ROT_DIM))
    ang = pos.astype(F32)[:, None] * inv[None, :]
    cos, sin = jnp.cos(ang), jnp.sin(ang)
    n = pos.shape[0]
    one = jnp.ones((n, ATT_HD - ROT_DIM), F32)
    zero = jnp.zeros((n, ATT_HD - ROT_DIM), F32)
    z8 = jnp.zeros((n, half), F32)
    c = jnp.concatenate([cos, cos, one], axis=1)
    s_hi = jnp.concatenate([-sin, z8, zero], axis=1)
    s_lo = jnp.concatenate([z8, sin, zero], axis=1)
    rep = LANES // ATT_HD
    return jnp.concatenate([jnp.tile(c, (1, rep)), jnp.tile(s_hi, (1, rep)), jnp.tile(s_lo, (1, rep))], axis=1)


def _sgu_tables(w_s, b_s, tn):
    tril = jnp.tril(jnp.ones((CHUNK, CHUNK), dtype=bool))
    ws = jnp.where(tril, w_s, 0)
    hd = D_SGU // SGU_HEADS
    bias_p = jnp.repeat(b_s.T, hd, axis=1)
    rep = CHUNK // tn
    eye = jnp.eye(rep, dtype=w_s.dtype)
    ws_s = jnp.einsum('ab,gts->gatbs', eye, ws[:, :tn, :tn]).reshape(SGU_HEADS, CHUNK, CHUNK)
    bias_s = jnp.tile(bias_p[:tn], (rep, 1))
    return ws.astype(BF16), bias_p.astype(F32), ws_s.astype(BF16), bias_s.astype(F32)


def kernel(x_prompt, x_sample, cache_k, cache_v, state_ssm_re, state_ssm_im, state_conv, page_table, ln_in_g, ln_in_b, w_in, w_gate, b_gate, s5_lam_re, s5_lam_im, s5_log_dt, s5_b_re, s5_b_im, s5_c_re, s5_c_im, s5_d, s5_w_glu, sgu_ln_g, sgu_ln_b, sgu_w_s, sgu_b_s, conv_w, conv_b, conv_ln_g, conv_ln_b, att_lam_q1, att_lam_k1, att_lam_q2, att_lam_k2, att_subln_g, w_br_a, w_br_b, w_br_c, w_br_d, w_o, ln1_g, ln1_b, ffn_wg, ffn_wu, ffn_wd, moe_router, moe_router_b, moe_wg, moe_wu, moe_wd, ln2_g, ln2_b):
    bp, tp, d = x_prompt.shape
    bs, ts, _ = x_sample.shape
    depth = w_in.shape[0]
    n_pages = page_table.shape[1]
    past = n_pages * PAGE_SIZE
    alpha = (2 * depth) ** 0.25
    n_p, n_s = bp * tp, bs * ts
    tm_p = min(512, tp)
    tq = min(512, tp)
    l_p = 16
    pg = min(8, n_pages)
    assert tp % tm_p == 0 and tp % CHUNK == 0 and tp % tq == 0 and tp % l_p == 0 and tp >= CONV_W - 1
    assert n_s % CHUNK == 0 and CHUNK % ts == 0 and ts % SUBLANES == 0 and n_pages % pg == 0 and ts <= l_p

    row1 = lambda a: a.reshape(1, -1).astype(F32)
    tab_p = _rope_table(jnp.arange(tp, dtype=jnp.int32))
    tab_s = jnp.tile(_rope_table(past + jnp.arange(ts, dtype=jnp.int32)), (bs, 1))
    zero_state = jnp.zeros((bp, S5_GROUPS, S5_STATE), F32)

    xp = _layernorm(x_prompt.reshape(n_p, d), ln_in_g, ln_in_b, tm_p)
    xs = _layernorm(x_sample.reshape(n_s, d), ln_in_g, ln_in_b, n_s)

    outs = [[] for _ in range(11)]
    for l in range(depth):
        lam_init = 0.8 - 0.6 * math.exp(-0.3 * l)
        lam = (jnp.exp(jnp.sum(att_lam_q1[l].astype(F32) * att_lam_k1[l].astype(F32)))
               - jnp.exp(jnp.sum(att_lam_q2[l].astype(F32) * att_lam_k2[l].astype(F32))) + lam_init)
        lamv = jnp.full((1, ATT_VD), lam, F32)
        gsv = row1(att_subln_g[l]) * (1.0 - lam_init)
        w_in_b = w_in[l].astype(BF16)
        ws_p, sb_p, ws_s, sb_s = _sgu_tables(sgu_w_s[l], sgu_b_s[l], ts)
        conv_args = (conv_w[l].astype(F32), row1(conv_b[l]), row1(conv_ln_g[l]), row1(conv_ln_b[l]))
        sgu_ln = (row1(sgu_ln_g[l]), row1(sgu_ln_b[l]))
        s5_par = (s5_lam_re[l], s5_lam_im[l], s5_log_dt[l], s5_b_re[l], s5_b_im[l], s5_c_re[l], s5_c_im[l])
        mw = {'w_glu': s5_w_glu[l].astype(BF16), 'w_gate': w_gate[l].astype(BF16), 'b_gate': row1(b_gate[l]),
              'w_br_a': w_br_a[l].astype(BF16), 'w_br_b': w_br_b[l].astype(BF16),
              'w_br_c': w_br_c[l].astype(BF16), 'w_br_d': w_br_d[l].astype(BF16), 'w_o': w_o[l].astype(BF16),
              'ln1_g': row1(ln1_g[l]), 'ln1_b': row1(ln1_b[l])}
        ln2 = (row1(ln2_g[l]), row1(ln2_b[l]))
        i = l // 2
        if l % 2 == 0:
            fw = (ffn_wg[i].astype(BF16), ffn_wu[i].astype(BF16), ffn_wd[i].astype(BF16))
            ffn = lambda x, tm: _ffn(x, *fw, *ln2, tm=tm, alpha=alpha)
        else:
            wr = jnp.pad(moe_router[i].astype(F32), ((0, 0), (0, LANES - N_EXPERTS)))
            wr_hi, wr_lo = _split_bf16(wr)
            br = jnp.pad(moe_router_b[i].astype(F32), (0, LANES - N_EXPERTS)).reshape(1, LANES)
            fw = (moe_wg[i].astype(BF16), moe_wu[i].astype(BF16), moe_wd[i].astype(BF16))
            ffn = lambda x, tm: _moe(x, wr_hi, wr_lo, br, *fw, *ln2, tm=tm, alpha=alpha)

        ua, yb, yc, glu, qb, k, v, kb, vx = _proj_prompt(xp, w_in_b, tab_p, *sgu_ln, ws_p, sb_p, *conv_args,
                                                         seq=tp, tm=tm_p, chunk=l_p)
        s5_tab, pw_re, pw_im = _s5_mats(*s5_par, l_p)
        ya, sr_p, si_p = _s5_branch(ua, zero_state, zero_state, s5_tab, pw_re[l_p], pw_im[l_p], s5_d[l],
                                    nb=bp, t=tp)
        yd = _attn_prompt(qb, kb, vx, lamv, gsv, nb=bp, t=tp, tq=tq)
        xp = _merge(xp, ya, yb, yc, yd, mw, tm=tm_p, alpha=alpha)
        xp = ffn(xp, tm_p)
        outs[0].append(k.reshape(bp, tp, ATT_HEADS, 2, ATT_HD))
        outs[1].append(v.reshape(bp, tp, ATT_HEADS, ATT_VD))
        outs[2].append(sr_p)
        outs[3].append(si_p)
        outs[4].append(glu.reshape(bp, tp, D_CONV)[:, tp - (CONV_W - 1):])

        pfx = jnp.pad(state_conv[l].astype(F32), ((0, 0), (HALO_OFF, 0), (0, 0)))
        ua, yb, yc, glu, q, k, v, vn = _proj_sample(xs, w_in_b, tab_s, *sgu_ln, ws_s, sb_s, *conv_args, pfx,
                                                    nb=bs, tn=ts)
        ya, sr_s, si_s = _s5_branch(ua, state_ssm_re[l], state_ssm_im[l], s5_tab, pw_re[ts], pw_im[ts], s5_d[l],
                                    nb=bs, t=ts)
        yd = _attn_sample(q, k, v, cache_k, cache_v, page_table, l, lamv, gsv, nb=bs, tn=ts, pg=pg)
        xs = _merge(xs, ya, yb, yc, yd, mw, tm=n_s, alpha=alpha)
        xs = ffn(xs, n_s)
        outs[5].append(k.reshape(bs, ts, ATT_HEADS, 2, ATT_HD))
        outs[6].append(v.reshape(bs, ts, ATT_HEADS, ATT_VD))
        outs[7].append(sr_s)
        outs[8].append(si_s)
        full = jnp.concatenate([state_conv[l].astype(F32), glu.reshape(bs, ts, D_CONV)], axis=1)
        outs[9].append(full[:, ts:])
        outs[10].append(vn.reshape(bs, ts, D_SGU))

    return (xp.reshape(bp, tp, d), xs.reshape(bs, ts, d), *[jnp.stack(o) for o in outs])
```

```python
import functools
import math

import jax
import jax.numpy as jnp
from jax import lax
from jax.experimental import pallas as pl
from jax.experimental.pallas import tpu as pltpu

F32 = jnp.float32
BF16 = jnp.bfloat16

D_MODEL = 1024
PAGE_SIZE = 128
S5_GROUP = 16
S5_GROUPS = 16
D_S5 = 256
S5_STATE = 64
D_SGU = 256
SGU_HEADS = 4
CHUNK = 128
D_CONV = 256
CONV_W = 31
ATT_HEADS = 4
ATT_HD = 64
ATT_VD = 128
D_ATT = 512
ROT_DIM = 16
ROPE_THETA = 500000.0
ATT_SCALE = ATT_HD ** -0.5
LOG2E = 1.4426950408889634
NEG_INF = -1e30
D_IN = 2816
N_BRANCH = 4
N_EXPERTS = 8
TOP_K = 2
LN_EPS = 1e-5
RMS_EPS = 1e-5

LANES = 128
SUBLANES = 8
HALO = 32
HALO_OFF = HALO - (CONV_W - 1)
VMEM_LIMIT = 56 * 1024 * 1024


def _cparams(sem):
    return pltpu.CompilerParams(dimension_semantics=sem, vmem_limit_bytes=VMEM_LIMIT)


def _const_spec(shape):
    nd = len(shape)
    return pl.BlockSpec(shape, lambda *_: (0,) * nd)


def _ln(x, g, b):
    mu = jnp.mean(x, axis=-1, keepdims=True)
    xc = x - mu
    var = jnp.mean(xc * xc, axis=-1, keepdims=True)
    return xc * lax.rsqrt(var + LN_EPS) * g + b


def _sigmoid(x):
    return 1.0 / (1.0 + jnp.exp(-x))


def _silu(x):
    return x * _sigmoid(x)


def _gelu(x):
    return jax.nn.gelu(x, approximate=True)


def _mm(a, b):
    return jnp.dot(a.astype(BF16), b.astype(BF16), preferred_element_type=F32)


def _mm_nt(a, b):
    return lax.dot_general(a.astype(BF16), b.astype(BF16), (((1,), (1,)), ((), ())),
                           preferred_element_type=F32)


def _rope(x, tab):
    n_rep = x.shape[1] // LANES
    c = jnp.concatenate([tab[:, 0:LANES]] * n_rep, axis=1)
    s_hi = jnp.concatenate([tab[:, LANES:2 * LANES]] * n_rep, axis=1)
    s_lo = jnp.concatenate([tab[:, 2 * LANES:3 * LANES]] * n_rep, axis=1)
    half = ROT_DIM // 2
    x_up = pltpu.roll(x, x.shape[1] - half, 1)
    x_dn = pltpu.roll(x, half, 1)
    return x * c + x_up * s_hi + x_dn * s_lo


def _ln_kernel(x_ref, g_ref, b_ref, o_ref):
    o_ref[...] = _ln(x_ref[...], g_ref[...], b_ref[...])


def _layernorm(x, g, b, tm):
    n, d = x.shape
    return pl.pallas_call(
        _ln_kernel,
        grid=(n // tm,),
        in_specs=[pl.BlockSpec((tm, d), lambda i: (i, 0)), _const_spec((1, d)), _const_spec((1, d))],
        out_specs=pl.BlockSpec((tm, d), lambda i: (i, 0)),
        out_shape=jax.ShapeDtypeStruct((n, d), F32),
        compiler_params=_cparams(("parallel",)),
    )(x, g.reshape(1, d), b.reshape(1, d))


def _proj_split(x, w_ref, tab, sgu_g, sgu_b, q_scale):
    h = _mm(x, w_ref[...])
    ua = h[:, 0:256]
    gub = _gelu(h[:, 256:512])
    vn = _ln(_gelu(h[:, 512:768]), sgu_g, sgu_b)
    glu = h[:, 768:1024] * _sigmoid(h[:, 1024:1280])
    q = _rope(h[:, 1280:1792], tab) * q_scale
    k = _rope(h[:, 1792:2304], tab)
    v = h[:, 2304:2816]
    return ua, gub, vn, glu, q, k, v


def _sgu(gub, vn, ws_ref, sb_ref):
    tm = vn.shape[0]
    head_of_lane = lax.broadcasted_iota(jnp.int32, (CHUNK, D_SGU), 1) // (D_SGU // SGU_HEADS)
    outs = []
    for c in range(tm // CHUNK):
        vc = vn[c * CHUNK:(c + 1) * CHUNK]
        z = sb_ref[...]
        for g in range(SGU_HEADS):
            z = z + _mm(ws_ref[g], jnp.where(head_of_lane == g, vc, 0.0))
        outs.append(gub[c * CHUNK:(c + 1) * CHUNK] * z)
    return outs[0] if len(outs) == 1 else jnp.concatenate(outs, axis=0)


def _conv_post(acc, cb, g, b):
    return _silu(_ln(acc + cb, g, b))


def _to_planes(rows, tmp_ref, planes_ref):
    n_planes, per_plane = planes_ref.shape[0], planes_ref.shape[1]
    for h in range(tmp_ref.shape[0]):
        tmp_ref[h] = rows[:, h * LANES:(h + 1) * LANES]
    for i in range(n_planes):
        for h in range(tmp_ref.shape[0]):
            planes_ref[i, :, h * LANES:(h + 1) * LANES] = tmp_ref[h, pl.ds(i, per_plane, stride=n_planes), :]


def _from_planes(planes_ref, tmp_ref):
    n_planes, per_plane = planes_ref.shape[0], planes_ref.shape[1]
    for i in range(n_planes):
        for h in range(tmp_ref.shape[0]):
            tmp_ref[h, pl.ds(i, per_plane, stride=n_planes), :] = planes_ref[i, :, h * LANES:(h + 1) * LANES]
    return jnp.concatenate([tmp_ref[h] for h in range(tmp_ref.shape[0])], axis=1)


def _proj_prompt_kernel(x_ref, w_ref, tab_ref, sg_ref, sbn_ref, ws_ref, sb_ref, cw_ref, cb_ref, cg_ref, cbn_ref,
                        up_ref, yb_ref, yc_ref, glu_ref, q_ref, k_ref, v_ref, kb_ref, vx_ref,
                        ext_ref, ua_ref, *, tm, tiles_per_seq, rb):
    i = pl.program_id(0)
    ua, gub, vn, glu, q, k, v = _proj_split(x_ref[...], w_ref, tab_ref[...], sg_ref[...], sbn_ref[...],
                                            ATT_SCALE * LOG2E)
    _to_planes(ua, ua_ref, up_ref)
    glu_ref[...] = glu
    q_ref[...] = q.astype(BF16)
    k_ref[...] = k.T
    for h in range(ATT_HEADS):
        v_ref[pl.ds(h, tm, stride=ATT_HEADS), :] = v[:, h * ATT_VD:(h + 1) * ATT_VD]
    kb_ref[...] = k.astype(BF16)
    vb = v.astype(BF16)
    ones = jnp.ones((tm, ATT_VD), BF16)
    vx_ref[...] = jnp.concatenate(
        [piece for h in range(ATT_HEADS) for piece in (vb[:, h * ATT_VD:(h + 1) * ATT_VD], ones)], axis=1)
    yb_ref[...] = _sgu(gub, vn, ws_ref, sb_ref)

    @pl.when(i % tiles_per_seq == 0)
    def _():
        ext_ref[0:HALO, :] = jnp.zeros((HALO, D_CONV), F32)

    ext_ref[HALO:HALO + tm, :] = glu
    for r in range(tm // rb):
        base = r * rb
        acc = jnp.zeros((rb, D_CONV), F32)
        for b in range(SUBLANES):
            zlen = rb if b == 0 else rb + SUBLANES
            z = jnp.zeros((zlen, D_CONV), F32)
            for a in range(HALO // SUBLANES + 1):
                w = SUBLANES * a + b - HALO_OFF
                if 0 <= w < CONV_W:
                    z = z + ext_ref[pl.ds(base + SUBLANES * a, zlen), :] * cw_ref[w:w + 1, :]
            acc = acc + z[b:b + rb]
        yc_ref[base:base + rb, :] = _conv_post(acc, cb_ref[...], cg_ref[...], cbn_ref[...])
    ext_ref[0:HALO, :] = ext_ref[tm:tm + HALO, :]


def _proj_prompt(x, w_in, tab, sgu_g, sgu_b, ws, sb, cw, cb, cg, cbn, *, seq, tm, chunk):
    n = x.shape[0]
    tiles_per_seq = seq // tm
    row = lambda width: pl.BlockSpec((tm, width), lambda i: (i, 0))
    f = lambda width, dt=F32: jax.ShapeDtypeStruct((n, width), dt)
    planes_spec = pl.BlockSpec((chunk, tm // chunk, D_S5), lambda i: (0, i, 0))
    planes_shape = jax.ShapeDtypeStruct((chunk, n // chunk, D_S5), F32)
    kern = functools.partial(_proj_prompt_kernel, tm=tm, tiles_per_seq=tiles_per_seq, rb=64)
    return pl.pallas_call(
        kern,
        grid=(n // tm,),
        in_specs=[row(D_MODEL), _const_spec((D_MODEL, D_IN)),
                  pl.BlockSpec((tm, 3 * LANES), lambda i: (i % tiles_per_seq, 0)),
                  _const_spec((1, D_SGU)), _const_spec((1, D_SGU)),
                  _const_spec((SGU_HEADS, CHUNK, CHUNK)), _const_spec((CHUNK, D_SGU)),
                  _const_spec((CONV_W, D_CONV)), _const_spec((1, D_CONV)), _const_spec((1, D_CONV)),
                  _const_spec((1, D_CONV))],
        out_specs=[planes_spec, row(D_SGU), row(D_CONV), row(D_CONV), row(D_ATT),
                   pl.BlockSpec((None, D_ATT, tm), lambda i: (i // tiles_per_seq, 0, i % tiles_per_seq)),
                   pl.BlockSpec((ATT_HEADS * tm, ATT_VD), lambda i: (i, 0)),
                   row(D_ATT), row(2 * D_ATT)],
        out_shape=[planes_shape, f(D_SGU), f(D_CONV), f(D_CONV), f(D_ATT, BF16),
                   jax.ShapeDtypeStruct((n // seq, D_ATT, seq), F32),
                   jax.ShapeDtypeStruct((ATT_HEADS * n, ATT_VD), F32),
                   f(D_ATT, BF16), f(2 * D_ATT, BF16)],
        scratch_shapes=[pltpu.VMEM((tm + HALO, D_CONV), F32), pltpu.VMEM((D_S5 // LANES, tm, LANES), F32)],
        compiler_params=_cparams(("arbitrary",)),
    )(x, w_in, tab, sgu_g, sgu_b, ws, sb, cw, cb, cg, cbn)


def _proj_sample_kernel(x_ref, w_ref, tab_ref, sg_ref, sbn_ref, ws_ref, sb_ref, cw_ref, cb_ref, cg_ref, cbn_ref,
                        pfx_ref,
                        up_ref, yb_ref, yc_ref, glu_ref, q_ref, k_ref, v_ref, vn_ref,
                        ext_ref, ua_ref, *, nb, tn):
    ua, gub, vn, glu, q, k, v = _proj_split(x_ref[...], w_ref, tab_ref[...], sg_ref[...], sbn_ref[...], ATT_SCALE)
    _to_planes(ua, ua_ref, up_ref)
    glu_ref[...] = glu
    q_ref[...] = q
    k_ref[...] = k
    v_ref[...] = v
    vn_ref[...] = vn
    yb_ref[...] = _sgu(gub, vn, ws_ref, sb_ref)

    ext_ref[:, 0:HALO, :] = pfx_ref[...]
    ext_ref[:, HALO:HALO + tn, :] = glu.reshape(nb, tn, D_CONV)
    acc = jnp.zeros((nb, tn, D_CONV), F32)
    for w in range(CONV_W):
        acc = acc + ext_ref[:, pl.ds(HALO_OFF + w, tn), :] * cw_ref[w:w + 1, :]
    yc_ref[...] = _conv_post(acc.reshape(nb * tn, D_CONV), cb_ref[...], cg_ref[...], cbn_ref[...])


def _proj_sample(x, w_in, tab, sgu_g, sgu_b, ws, sb, cw, cb, cg, cbn, pfx, *, nb, tn):
    n = x.shape[0]
    full = lambda width: _const_spec((n, width))
    f = lambda width: jax.ShapeDtypeStruct((n, width), F32)
    kern = functools.partial(_proj_sample_kernel, nb=nb, tn=tn)
    return pl.pallas_call(
        kern,
        grid=(1,),
        in_specs=[full(D_MODEL), _const_spec((D_MODEL, D_IN)), full(3 * LANES),
                  _const_spec((1, D_SGU)), _const_spec((1, D_SGU)),
                  _const_spec((SGU_HEADS, CHUNK, CHUNK)), _const_spec((CHUNK, D_SGU)),
                  _const_spec((CONV_W, D_CONV)), _const_spec((1, D_CONV)), _const_spec((1, D_CONV)),
                  _const_spec((1, D_CONV)), _const_spec((nb, HALO, D_CONV))],
        out_specs=[_const_spec((tn, nb, D_S5)), full(D_SGU), full(D_CONV), full(D_CONV), full(D_ATT), full(D_ATT),
                   full(D_ATT), full(D_SGU)],
        out_shape=[jax.ShapeDtypeStruct((tn, nb, D_S5), F32), f(D_SGU), f(D_CONV), f(D_CONV), f(D_ATT), f(D_ATT),
                   f(D_ATT), f(D_SGU)],
        scratch_shapes=[pltpu.VMEM((nb, HALO + tn, D_CONV), F32), pltpu.VMEM((D_S5 // LANES, n, LANES), F32)],
        compiler_params=_cparams(("arbitrary",)),
    )(x, w_in, tab, sgu_g, sgu_b, ws, sb, cw, cb, cg, cbn, pfx)


def _s5_mats(lam_re, lam_im, log_dt, b_re, b_im, c_re, c_im, L):
    G, P, C = S5_GROUPS, S5_STATE, S5_GROUP
    lr, li = lam_re.astype(F32), lam_im.astype(F32)
    dt = jnp.exp(log_dt.astype(F32))[:, None]
    steps = jnp.arange(L + 1, dtype=F32)[:, None, None]
    mag = jnp.exp((lr * dt)[None] * steps)
    pr = mag * jnp.cos((li * dt)[None] * steps)
    pi = mag * jnp.sin((li * dt)[None] * steps)
    nr, ni = pr[1] - 1.0, pi[1]
    den = lr * lr + li * li
    fr = (nr * lr + ni * li) / den
    fi = (ni * lr - nr * li) / den
    bre, bim = b_re.astype(F32), b_im.astype(F32)
    br = fr[..., None] * bre - fi[..., None] * bim
    bi = fr[..., None] * bim + fi[..., None] * bre
    cr, ci = c_re.astype(F32), c_im.astype(F32)
    kr = (jnp.einsum('gcp,tgp,gpd->tgdc', cr, pr[:L], br) - jnp.einsum('gcp,tgp,gpd->tgdc', cr, pi[:L], bi)
          - jnp.einsum('gcp,tgp,gpd->tgdc', ci, pr[:L], bi) - jnp.einsum('gcp,tgp,gpd->tgdc', ci, pi[:L], br))

    def block_diag(compact):
        _, _, r, c = compact.shape
        repeat = jnp.tile(jnp.eye(c, dtype=F32), (1, G))
        tiled = jnp.einsum('lrc,cq->lrq', compact.reshape(L, G * r, c), repeat, precision=lax.Precision.HIGHEST)
        row_g = lax.broadcasted_iota(jnp.int32, (G * r, G * c), 0) // r
        col_g = lax.broadcasted_iota(jnp.int32, (G * r, G * c), 1) // c
        return jnp.where(row_g == col_g, tiled, 0.0).astype(BF16)

    kbd = block_diag(kr)
    rev = L - 1 - jnp.arange(L)
    brt, bit = br.transpose(0, 2, 1), bi.transpose(0, 2, 1)
    m1re = block_diag(pr[rev][:, :, None, :] * brt[None] - pi[rev][:, :, None, :] * bit[None])
    m1im = block_diag(pr[rev][:, :, None, :] * bit[None] + pi[rev][:, :, None, :] * brt[None])
    crt, cit = cr.transpose(0, 2, 1), ci.transpose(0, 2, 1)
    nr, ni = pr[1:L + 1][..., None], pi[1:L + 1][..., None]
    m2re = block_diag(crt[None] * nr - cit[None] * ni)
    m2im = block_diag(-(crt[None] * ni + cit[None] * nr))
    return (kbd, m1re, m1im, m2re, m2im), pr.reshape(L + 1, 1, G * P), pi.reshape(L + 1, 1, G * P)


def _s5_local_kernel(up_ref, m1re_ref, m1im_ref, slre_ref, slim_ref):
    @pl.when(pl.program_id(0) == 0)
    def _():
        slre_ref[...] = jnp.zeros_like(slre_ref)
        slim_ref[...] = jnp.zeros_like(slim_ref)

    u = up_ref[...].astype(BF16)
    slre_ref[...] += jnp.dot(u, m1re_ref[...], preferred_element_type=F32)
    slim_ref[...] += jnp.dot(u, m1im_ref[...], preferred_element_type=F32)


def _s5_scan_kernel(slre_ref, slim_ref, s0re_ref, s0im_ref, are_ref, aim_ref,
                    stre_ref, stim_ref, fre_ref, fim_ref, *, nb, cpb):
    ar = are_ref[...]
    ai = aim_ref[...]

    def body(c, carry):
        out = []
        for b in range(nb):
            sr, si = carry[b]
            row = pl.ds(b * cpb + c, 1)
            stre_ref[row, :] = sr
            stim_ref[row, :] = si
            out.append((ar * sr - ai * si + slre_ref[row, :], ar * si + ai * sr + slim_ref[row, :]))
        return tuple(out)

    init = tuple((s0re_ref[b:b + 1, :], s0im_ref[b:b + 1, :]) for b in range(nb))
    fin = lax.fori_loop(0, cpb, body, init)
    for b in range(nb):
        fre_ref[b:b + 1, :] = fin[b][0]
        fim_ref[b:b + 1, :] = fin[b][1]


def _s5_out_kernel(up_ref, kbd_ref, stre_ref, stim_ref, m2re_ref, m2im_ref, d_ref, o_ref):
    i = pl.program_id(0)
    o_ref[...] = (_mm(stre_ref[...], m2re_ref[...]) + _mm(stim_ref[...], m2im_ref[...])
                  + d_ref[...] * up_ref[i])

    def body(tau, carry):
        o_ref[...] += _mm(up_ref[i - tau], kbd_ref[tau])
        return carry

    lax.fori_loop(0, i + 1, body, 0)
    o_ref[...] = _gelu(o_ref[...])


def _s5_branch(up, s0_re, s0_im, tables, a_re, a_im, d_skip, *, nb, t):
    kbd, m1re, m1im, m2re, m2im = tables
    L, ch, dc = up.shape
    l_tab = kbd.shape[0]
    m1_off = l_tab - L
    G, P = S5_GROUPS, S5_STATE
    gp = G * P
    cpb = t // L
    acc_spec = _const_spec((ch, gp))
    acc_shape = jax.ShapeDtypeStruct((ch, gp), F32)
    slre, slim = pl.pallas_call(
        _s5_local_kernel,
        grid=(L,),
        in_specs=[pl.BlockSpec((None, ch, dc), lambda j: (j, 0, 0)),
                  pl.BlockSpec((None, dc, gp), lambda j: (j + m1_off, 0, 0)),
                  pl.BlockSpec((None, dc, gp), lambda j: (j + m1_off, 0, 0))],
        out_specs=[acc_spec, acc_spec],
        out_shape=[acc_shape, acc_shape],
        compiler_params=_cparams(("arbitrary",)),
    )(up, m1re, m1im)
    fin_shape = jax.ShapeDtypeStruct((nb, gp), F32)
    stre, stim, fre, fim = pl.pallas_call(
        functools.partial(_s5_scan_kernel, nb=nb, cpb=cpb),
        out_shape=[acc_shape, acc_shape, fin_shape, fin_shape],
        compiler_params=pltpu.CompilerParams(vmem_limit_bytes=VMEM_LIMIT),
    )(slre, slim, s0_re.reshape(nb, gp).astype(F32), s0_im.reshape(nb, gp).astype(F32), a_re, a_im)
    yp = pl.pallas_call(
        _s5_out_kernel,
        grid=(L,),
        in_specs=[_const_spec((L, ch, dc)), _const_spec((l_tab, dc, dc)), acc_spec, acc_spec,
                  pl.BlockSpec((None, gp, dc), lambda i: (i, 0, 0)),
                  pl.BlockSpec((None, gp, dc), lambda i: (i, 0, 0)),
                  _const_spec((1, dc))],
        out_specs=pl.BlockSpec((None, ch, dc), lambda i: (i, 0, 0)),
        out_shape=jax.ShapeDtypeStruct((L, ch, dc), F32),
        compiler_params=_cparams(("parallel",)),
    )(up, kbd, stre, stim, m2re, m2im, d_skip.astype(F32).reshape(1, dc))
    return yp, fre.reshape(nb, G, P), fim.reshape(nb, G, P)


def _diff_finish(o1, o2, lam, gs):
    o = o1 - lam * o2
    return o * lax.rsqrt(jnp.mean(o * o, axis=-1, keepdims=True) + RMS_EPS) * gs


def _attn_prompt_kernel(q_ref, k_ref, v_ref, lam_ref, gs_ref, o_ref, q2_ref, m_ref, acc_ref, *, tq, rb, unroll):
    i = pl.program_id(2)
    r2 = 2 * tq
    q = q_ref[...]
    lane = lax.broadcasted_iota(jnp.int32, (tq, ATT_VD), 1)
    zero = jnp.zeros_like(q)
    q2_ref[0:tq, :] = jnp.where(lane < ATT_HD, q, zero)
    q2_ref[tq:r2, :] = jnp.where(lane >= ATT_HD, q, zero)
    m_ref[...] = jnp.full((r2, LANES), NEG_INF, F32)
    acc_ref[...] = jnp.zeros((r2, 2 * ATT_VD), F32)

    def step(j, diagonal):
        off = pl.multiple_of(j * tq, tq)
        kt = k_ref[pl.ds(off, tq), :]
        vt = v_ref[pl.ds(off, tq), :]
        for r in range(r2 // rb):
            rows = slice(r * rb, (r + 1) * rb)
            s = _mm_nt(q2_ref[rows, :], kt)
            if diagonal:
                row = (lax.broadcasted_iota(jnp.int32, (rb, tq), 0) + r * rb) % tq
                col = lax.broadcasted_iota(jnp.int32, (rb, tq), 1)
                s = jnp.where(col <= row, s, NEG_INF)
            m_prev = m_ref[rows, :]
            m_new = jnp.maximum(m_prev, jnp.max(s, axis=-1, keepdims=True))
            alpha = jnp.exp2(m_prev - m_new)
            p = jnp.exp2(s - jnp.concatenate([m_new] * (tq // LANES), axis=1))
            acc_ref[rows, :] = jnp.concatenate([alpha, alpha], axis=1) * acc_ref[rows, :] + _mm(p, vt)
            m_ref[rows, :] = m_new

    def group(g, carry):
        for u in range(unroll):
            step(g * unroll + u, False)
        return carry

    def single(j, carry):
        step(j, False)
        return carry

    n_grp = i // unroll
    lax.fori_loop(0, n_grp, group, 0)
    lax.fori_loop(n_grp * unroll, i, single, 0)
    step(i, True)
    acc = acc_ref[...]
    o = acc[:, 0:ATT_VD] / acc[:, ATT_VD:]
    o_ref[...] = _diff_finish(o[0:tq], o[tq:r2], lam_ref[...], gs_ref[...])


def _attn_prompt(qb, kb, vx, lamv, gsv, *, nb, t, tq):
    n = nb * t
    nq = t // tq
    return pl.pallas_call(
        functools.partial(_attn_prompt_kernel, tq=tq, rb=min(128, tq), unroll=4),
        grid=(nb, ATT_HEADS, nq),
        in_specs=[pl.BlockSpec((tq, ATT_VD), lambda b, h, i: (b * nq + i, h)),
                  pl.BlockSpec((t, ATT_VD), lambda b, h, i: (b, h)),
                  pl.BlockSpec((t, 2 * ATT_VD), lambda b, h, i: (b, h)),
                  _const_spec((1, ATT_VD)), _const_spec((1, ATT_VD))],
        out_specs=pl.BlockSpec((tq, ATT_VD), lambda b, h, i: (b * nq + i, h)),
        out_shape=jax.ShapeDtypeStruct((n, D_ATT), F32),
        scratch_shapes=[pltpu.VMEM((2 * tq, ATT_VD), BF16), pltpu.VMEM((2 * tq, LANES), F32),
                        pltpu.VMEM((2 * tq, 2 * ATT_VD), F32)],
        compiler_params=_cparams(("parallel", "parallel", "arbitrary")),
    )(qb, kb, vx, lamv, gsv)


def _attn_sample_kernel(pt_ref, q_ref, kn_ref, vn_ref, lam_ref, gs_ref, *rest, tn, pg):
    k_refs = rest[0:pg]
    v_refs = rest[pg:2 * pg]
    o_ref = rest[2 * pg]
    qbd_ref, m_ref, l_ref, acc_ref = rest[2 * pg + 1:]
    p_idx = pl.program_id(1)
    rows_h = 2 * tn
    n_rows = ATT_HEADS * rows_h

    def update(s, v_pages):
        n_rep = s.shape[1] // LANES
        m_prev = m_ref[...]
        m_new = jnp.maximum(m_prev, jnp.max(s, axis=-1, keepdims=True))
        p = jnp.exp(s - jnp.concatenate([m_new] * n_rep, axis=1))
        alpha = jnp.exp(m_prev - m_new)
        l_ref[...] = alpha * l_ref[...] + jnp.sum(p, axis=-1, keepdims=True)
        for h in range(ATT_HEADS):
            rows = slice(h * rows_h, (h + 1) * rows_h)
            v_h = [r[pl.ds(h, PAGE_SIZE, stride=ATT_HEADS), :].astype(BF16) for r in v_pages]
            v_h = v_h[0] if len(v_h) == 1 else jnp.concatenate(v_h, axis=0)
            acc_ref[rows, :] = alpha[rows] * acc_ref[rows, :] + _mm(p[rows], v_h)
        m_ref[...] = m_new

    @pl.when(p_idx == 0)
    def _():
        q = q_ref[...]
        lane_grp = lax.broadcasted_iota(jnp.int32, (tn, D_ATT), 1) // ATT_HD
        pieces = [jnp.where(lane_grp == g, q, 0.0) for g in range(2 * ATT_HEADS)]
        qbd_ref[...] = jnp.concatenate(pieces, axis=0).astype(BF16)
        m_ref[...] = jnp.full((n_rows, LANES), NEG_INF, F32)
        l_ref[...] = jnp.zeros((n_rows, LANES), F32)
        acc_ref[...] = jnp.zeros((n_rows, ATT_VD), F32)
        s = _mm(qbd_ref[...], kn_ref[...])
        t_of_row = lax.broadcasted_iota(jnp.int32, (n_rows, PAGE_SIZE), 0) % tn
        col = lax.broadcasted_iota(jnp.int32, (n_rows, PAGE_SIZE), 1)
        update(jnp.where(col <= t_of_row, s, NEG_INF), [vn_ref])

    kcat = jnp.concatenate([r[...].astype(BF16) for r in k_refs], axis=1)
    update(_mm(qbd_ref[...], kcat), list(v_refs))

    @pl.when(p_idx == pl.num_programs(1) - 1)
    def _():
        o = acc_ref[...] / l_ref[...]
        outs = []
        for h in range(ATT_HEADS):
            o1 = o[h * rows_h:h * rows_h + tn]
            o2 = o[h * rows_h + tn:(h + 1) * rows_h]
            outs.append(_diff_finish(o1, o2, lam_ref[...], gs_ref[...]))
        o_ref[...] = jnp.concatenate(outs, axis=1)


def _attn_sample(q, k_new, v_new, cache_k, cache_v, page_table, layer, lamv, gsv, *, nb, tn, pg):
    n_pages = page_table.shape[1]
    depth, n_pool = cache_k.shape[0], cache_k.shape[1]
    rows_v = PAGE_SIZE * ATT_HEADS
    ck = jnp.transpose(cache_k, (0, 1, 3, 4, 5, 2)).reshape(depth, n_pool, D_ATT, PAGE_SIZE)
    cv = cache_v.reshape(depth, n_pool, rows_v, ATT_VD)
    kn = jnp.pad(k_new.reshape(nb, tn, D_ATT).transpose(0, 2, 1), ((0, 0), (0, 0), (0, PAGE_SIZE - tn)))
    vn = jnp.pad(v_new.reshape(nb, tn * ATT_HEADS, ATT_VD), ((0, 0), (0, rows_v - tn * ATT_HEADS), (0, 0)))
    n_rows = 2 * ATT_HEADS * tn

    def page_spec(r, rows, cols):
        return pl.BlockSpec((None, None, rows, cols), lambda b, p, pt: (layer, pt[b, p * pg + r], 0, 0))

    grid_spec = pltpu.PrefetchScalarGridSpec(
        num_scalar_prefetch=1,
        grid=(nb, n_pages // pg),
        in_specs=[pl.BlockSpec((tn, D_ATT), lambda b, p, pt: (b, 0)),
                  pl.BlockSpec((None, D_ATT, PAGE_SIZE), lambda b, p, pt: (b, 0, 0)),
                  pl.BlockSpec((None, rows_v, ATT_VD), lambda b, p, pt: (b, 0, 0)),
                  pl.BlockSpec((1, ATT_VD), lambda b, p, pt: (0, 0)),
                  pl.BlockSpec((1, ATT_VD), lambda b, p, pt: (0, 0))]
                 + [page_spec(r, D_ATT, PAGE_SIZE) for r in range(pg)]
                 + [page_spec(r, rows_v, ATT_VD) for r in range(pg)],
        out_specs=pl.BlockSpec((tn, D_ATT), lambda b, p, pt: (b, 0)),
        scratch_shapes=[pltpu.VMEM((n_rows, D_ATT), BF16), pltpu.VMEM((n_rows, LANES), F32),
                        pltpu.VMEM((n_rows, LANES), F32), pltpu.VMEM((n_rows, ATT_VD), F32)],
    )
    return pl.pallas_call(
        functools.partial(_attn_sample_kernel, tn=tn, pg=pg),
        grid_spec=grid_spec,
        out_shape=jax.ShapeDtypeStruct((nb * tn, D_ATT), F32),
        compiler_params=_cparams(("parallel", "arbitrary")),
    )(page_table, q, kn, vn, lamv, gsv, *([ck] * pg), *([cv] * pg))


def _merge_kernel(x_ref, ya_ref, yb_ref, yc_ref, yd_ref, wglu_ref, wgate_ref, bgate_ref,
                  wa_ref, wb_ref, wc_ref, wd_ref, wo_ref, g_ref, b_ref, o_ref, ya_rows_ref, *, alpha):
    x = x_ref[...]
    xb = x.astype(BF16)
    hg = _mm(_from_planes(ya_ref, ya_rows_ref), wglu_ref[...])
    ya = hg[:, :D_S5] * _sigmoid(hg[:, D_S5:])
    merged = None
    for idx, (y, w_ref) in enumerate(((ya, wa_ref), (yb_ref[...], wb_ref), (yc_ref[...], wc_ref),
                                      (yd_ref[...], wd_ref))):
        cols = slice(idx * D_MODEL, (idx + 1) * D_MODEL)
        gate = _sigmoid(_mm(xb, wgate_ref[:, cols]) + bgate_ref[:, cols])
        term = gate * _mm(y, w_ref[...])
        merged = term if merged is None else merged + term
    out = _mm(merged, wo_ref[...])
    o_ref[...] = _ln(alpha * x + out, g_ref[...], b_ref[...])


def _merge(x, ya, yb, yc, yd, w, *, tm, alpha):
    n = x.shape[0]
    chunk = ya.shape[0]
    row = lambda width: pl.BlockSpec((tm, width), lambda i: (i, 0))
    return pl.pallas_call(
        functools.partial(_merge_kernel, alpha=alpha),
        grid=(n // tm,),
        in_specs=[row(D_MODEL), pl.BlockSpec((chunk, tm // chunk, D_S5), lambda i: (0, i, 0)),
                  row(D_SGU), row(D_CONV), row(D_ATT),
                  _const_spec((D_S5, 2 * D_S5)), _const_spec((D_MODEL, N_BRANCH * D_MODEL)),
                  _const_spec((1, N_BRANCH * D_MODEL)),
                  _const_spec((D_S5, D_MODEL)), _const_spec((D_SGU, D_MODEL)), _const_spec((D_CONV, D_MODEL)),
                  _const_spec((D_ATT, D_MODEL)), _const_spec((D_MODEL, D_MODEL)),
                  _const_spec((1, D_MODEL)), _const_spec((1, D_MODEL))],
        out_specs=row(D_MODEL),
        out_shape=jax.ShapeDtypeStruct((n, D_MODEL), F32),
        scratch_shapes=[pltpu.VMEM((D_S5 // LANES, tm, LANES), F32)],
        compiler_params=_cparams(("parallel",)),
    )(x, ya, yb, yc, yd, w['w_glu'], w['w_gate'], w['b_gate'], w['w_br_a'], w['w_br_b'], w['w_br_c'],
      w['w_br_d'], w['w_o'], w['ln1_g'], w['ln1_b'])


def _ffn_kernel(x_ref, wg_ref, wu_ref, wd_ref, g_ref, b_ref, o_ref, *, alpha):
    x = x_ref[...]
    xb = x.astype(BF16)
    hid = _silu(_mm(xb, wg_ref[...])) * _mm(xb, wu_ref[...])
    o_ref[...] = _ln(alpha * x + _mm(hid, wd_ref[...]), g_ref[...], b_ref[...])


def _ffn(x, wg, wu, wd, g, b, *, tm, alpha):
    n = x.shape[0]
    dff = wg.shape[1]
    row = pl.BlockSpec((tm, D_MODEL), lambda i: (i, 0))
    return pl.pallas_call(
        functools.partial(_ffn_kernel, alpha=alpha),
        grid=(n // tm,),
        in_specs=[row, _const_spec((D_MODEL, dff)), _const_spec((D_MODEL, dff)), _const_spec((dff, D_MODEL)),
                  _const_spec((1, D_MODEL)), _const_spec((1, D_MODEL))],
        out_specs=row,
        out_shape=jax.ShapeDtypeStruct((n, D_MODEL), F32),
        compiler_params=_cparams(("parallel",)),
    )(x, wg, wu, wd, g, b)


def _split_bf16(a):
    hi = a.astype(BF16)
    lo = (a - hi.astype(F32)).astype(BF16)
    return hi, lo


SEL_I1, SEL_I2, SEL_W1, SEL_W2 = 0, 1, 2, 3
MOE_TM = 512


def _moe_route_kernel(x_ref, wr_hi_ref, wr_lo_ref, br_ref, sel_ref):
    x = x_ref[...]
    tm = x.shape[0]
    lane = lax.broadcasted_iota(jnp.int32, (tm, LANES), 1)
    x_hi, x_lo = _split_bf16(x)
    logits = (jnp.dot(x_hi, wr_hi_ref[...], preferred_element_type=F32)
              + jnp.dot(x_hi, wr_lo_ref[...], preferred_element_type=F32)
              + jnp.dot(x_lo, wr_hi_ref[...], preferred_element_type=F32)) + br_ref[...]
    logits = jnp.where(lane < N_EXPERTS, logits, -jnp.inf)
    lane_f = lane.astype(F32)
    v1 = jnp.max(logits, axis=-1, keepdims=True)
    i1 = jnp.min(jnp.where(logits == v1, lane_f, float(LANES)), axis=-1, keepdims=True)
    rest = jnp.where(lane_f == i1, -jnp.inf, logits)
    v2 = jnp.max(rest, axis=-1, keepdims=True)
    i2 = jnp.min(jnp.where(rest == v2, lane_f, float(LANES)), axis=-1, keepdims=True)
    e2 = jnp.exp(v2 - v1)
    w1 = 1.0 / (1.0 + e2)
    w2 = e2 / (1.0 + e2)
    sel_ref[...] = (jnp.where(lane == SEL_I1, i1, 0.0) + jnp.where(lane == SEL_I2, i2, 0.0)
                    + jnp.where(lane == SEL_W1, w1, 0.0) + jnp.where(lane == SEL_W2, w2, 0.0))


def _moe_expert_kernel(item_e, item_tile, item_flag, starts, idx_ref, idx_next_ref, idx_prev_ref, x_hbm,
                       wg_ref, wu_ref, wd_ref, y_hbm, xbuf, obuf, gsem, ssem, *, n_tiles, n_tok):
    i = pl.program_id(0)
    tm = xbuf.shape[1]
    flag = item_flag[i]
    valid = (flag & 1) == 1
    first = (flag & 2) == 2
    last = (flag & 4) == 4
    t = item_tile[i]
    e = item_e[i]
    slot = t % 2
    other = 1 - slot

    def for_rows(body, unrolled):
        if unrolled:
            for r in range(tm):
                body(r)
        else:
            def step(r, carry):
                body(r)
                return carry
            lax.fori_loop(0, tm, step, 0, unroll=8)

    def gather_rows(idx, dst_slot, unrolled=False):
        def body(r):
            a = idx[0, r]
            tok = jnp.where(a >= n_tok, a - n_tok, a)
            pltpu.make_async_copy(x_hbm.at[pl.ds(tok, 1)], xbuf.at[dst_slot, pl.ds(r, 1)], gsem.at[dst_slot]).start()
        for_rows(body, unrolled)

    def scatter_rows(idx, src_slot, unrolled=False):
        def body(r):
            dst = idx[0, r]
            pltpu.make_async_copy(obuf.at[src_slot, pl.ds(r, 1)], y_hbm.at[pl.ds(dst, 1)], ssem.at[src_slot]).start()
        for_rows(body, unrolled)

    def wait_rows(buf, sem, s):
        pltpu.make_async_copy(buf.at[s], buf.at[s], sem.at[s]).wait()

    def compute():
        xb = xbuf[slot].astype(BF16)
        hid = _silu(_mm(xb, wg_ref[...])) * _mm(xb, wu_ref[...])
        y = _mm(hid, wd_ref[...])
        pos = t * tm + lax.broadcasted_iota(jnp.int32, (tm, 1), 0)
        mine = (pos >= starts[e]) & (pos < starts[e + 1])
        obuf[slot] += jnp.where(mine, y, 0.0)

    @pl.when(valid & first)
    def _():
        @pl.when(t == 0)
        def _():
            gather_rows(idx_ref, 0)

        wait_rows(xbuf, gsem, slot)

        @pl.when(t >= 2)
        def _():
            wait_rows(obuf, ssem, slot)

        obuf[slot] = jnp.zeros((tm, D_MODEL), F32)

    has_next = t + 1 < n_tiles
    has_prev = t >= 1
    interior = valid & last & has_next & has_prev

    @pl.when(interior)
    def _():
        gather_rows(idx_next_ref, other, unrolled=True)
        scatter_rows(idx_prev_ref, other, unrolled=True)
        compute()

    @pl.when(valid & jnp.logical_not(interior))
    def _():
        @pl.when(last & has_next)
        def _():
            gather_rows(idx_next_ref, other)

        @pl.when(last & has_prev)
        def _():
            scatter_rows(idx_prev_ref, other)

        compute()

    @pl.when(i == pl.num_programs(0) - 1)
    def _():
        scatter_rows(idx_ref, (n_tiles - 1) % 2)
        for tt in range(max(0, n_tiles - 2), n_tiles):
            wait_rows(obuf, ssem, tt % 2)


def _moe_combine_kernel(x_ref, y1_ref, y2_ref, sel_ref, g_ref, b_ref, o_ref, *, alpha):
    sel = sel_ref[...]
    lane = lax.broadcasted_iota(jnp.int32, sel.shape, 1)
    w1 = jnp.sum(jnp.where(lane == SEL_W1, sel, 0.0), axis=-1, keepdims=True)
    w2 = jnp.sum(jnp.where(lane == SEL_W2, sel, 0.0), axis=-1, keepdims=True)
    y = w1 * y1_ref[...] + w2 * y2_ref[...]
    o_ref[...] = _ln(alpha * x_ref[...] + y, g_ref[...], b_ref[...])


def _moe(x, wr_hi, wr_lo, br, wg, wu, wd, g, b, *, tm, alpha):
    n = x.shape[0]
    dfe = wg.shape[2]
    n_asg = TOP_K * n
    tmx = min(MOE_TM, n_asg)
    assert n_asg % tmx == 0
    n_tiles = n_asg // tmx
    n_items = n_tiles + N_EXPERTS - 1
    row = pl.BlockSpec((tm, D_MODEL), lambda i: (i, 0))

    sel = pl.pallas_call(
        _moe_route_kernel,
        grid=(n // tm,),
        in_specs=[row, _const_spec((D_MODEL, LANES)), _const_spec((D_MODEL, LANES)), _const_spec((1, LANES))],
        out_specs=pl.BlockSpec((tm, LANES), lambda i: (i, 0)),
        out_shape=jax.ShapeDtypeStruct((n, LANES), F32),
        compiler_params=_cparams(("parallel",)),
    )(x, wr_hi, wr_lo, br)

    experts = sel[:, SEL_I1:SEL_I2 + 1].astype(jnp.int32).T.reshape(n_asg)
    keys = jnp.sort(experts * n_asg + jnp.arange(n_asg, dtype=jnp.int32))
    idx = (keys % n_asg).reshape(n_tiles, 1, tmx)
    bounds = jnp.arange(N_EXPERTS + 1, dtype=jnp.int32) * n_asg
    starts = jnp.sum((keys[None, :] < bounds[:, None]).astype(jnp.int32), axis=1)
    counts = starts[1:] - starts[:-1]
    first_tile = starts[:-1] // tmx
    last_tile = (starts[1:] - 1) // tmx
    per_e = jnp.where(counts > 0, last_tile - first_tile + 1, 0)
    ends = jnp.cumsum(per_e)
    total = ends[-1]
    it = jnp.arange(n_items, dtype=jnp.int32)
    e_raw = jnp.sum((ends[None, :] <= it[:, None]).astype(jnp.int32), axis=1)
    e_last = jnp.max(jnp.where(counts > 0, jnp.arange(N_EXPERTS, dtype=jnp.int32), 0))
    valid = it < total
    item_e = jnp.where(valid, jnp.minimum(e_raw, N_EXPERTS - 1), e_last).astype(jnp.int32)
    item_tile = jnp.where(valid, first_tile[item_e] + it - (ends[item_e] - per_e[item_e]), n_tiles - 1)
    item_tile = item_tile.astype(jnp.int32)
    prev_tile = jnp.concatenate([jnp.full((1,), -1, jnp.int32), item_tile[:-1]])
    next_tile = jnp.concatenate([item_tile[1:], jnp.full((1,), -1, jnp.int32)])
    next_valid = jnp.concatenate([valid[1:], jnp.zeros((1,), bool)])
    is_first = item_tile != prev_tile
    is_last = (item_tile != next_tile) | ~next_valid
    item_flag = (valid.astype(jnp.int32) + 2 * is_first.astype(jnp.int32) + 4 * is_last.astype(jnp.int32))

    smem_idx = lambda f: pl.BlockSpec((None, 1, tmx), f, memory_space=pltpu.SMEM)
    grid_spec = pltpu.PrefetchScalarGridSpec(
        num_scalar_prefetch=4,
        grid=(n_items,),
        in_specs=[smem_idx(lambda i, ie, itl, ifl, st: (itl[i], 0, 0)),
                  smem_idx(lambda i, ie, itl, ifl, st: (jnp.minimum(itl[i] + 1, n_tiles - 1), 0, 0)),
                  smem_idx(lambda i, ie, itl, ifl, st: (jnp.maximum(itl[i] - 1, 0), 0, 0)),
                  pl.BlockSpec(memory_space=pl.ANY),
                  pl.BlockSpec((None, D_MODEL, dfe), lambda i, ie, itl, ifl, st: (ie[i], 0, 0)),
                  pl.BlockSpec((None, D_MODEL, dfe), lambda i, ie, itl, ifl, st: (ie[i], 0, 0)),
                  pl.BlockSpec((None, dfe, D_MODEL), lambda i, ie, itl, ifl, st: (ie[i], 0, 0))],
        out_specs=pl.BlockSpec(memory_space=pl.ANY),
        scratch_shapes=[pltpu.VMEM((2, tmx, D_MODEL), F32), pltpu.VMEM((2, tmx, D_MODEL), F32),
                        pltpu.SemaphoreType.DMA((2,)), pltpu.SemaphoreType.DMA((2,))],
    )
    y = pl.pallas_call(
        functools.partial(_moe_expert_kernel, n_tiles=n_tiles, n_tok=n),
        grid_spec=grid_spec,
        out_shape=jax.ShapeDtypeStruct((n_asg, D_MODEL), F32),
        compiler_params=_cparams(("arbitrary",)),
    )(item_e, item_tile, item_flag, starts, idx, idx, idx, x, wg, wu, wd)

    return pl.pallas_call(
        functools.partial(_moe_combine_kernel, alpha=alpha),
        grid=(n // tm,),
        in_specs=[row, row, pl.BlockSpec((tm, D_MODEL), lambda i: (i + n // tm, 0)),
                  pl.BlockSpec((tm, LANES), lambda i: (i, 0)), _const_spec((1, D_MODEL)), _const_spec((1, D_MODEL))],
        out_specs=row,
        out_shape=jax.ShapeDtypeStruct((n, D_MODEL), F32),
        compiler_params=_cparams(("parallel",)),
    )(x, y, y, sel, g, b)


def _rope_table(pos):
    half = ROT_DIM // 2
    inv = ROPE_THETA ** (-jnp.arange(half, dtype=F32) * (2.0 / ROT_DIM))
    ang = pos.astype(F32)[:, None] * inv[None, :]
    cos, sin = jnp.cos(ang), jnp.sin(ang)
    n = pos.shape[0]
    one = jnp.ones((n, ATT_HD - ROT_DIM), F32)
    zero = jnp.zeros((n, ATT_HD - ROT_DIM), F32)
    z8 = jnp.zeros((n, half), F32)
    c = jnp.concatenate([cos, cos, one], axis=1)
    s_hi = jnp.concatenate([-sin, z8, zero], axis=1)
    s_lo = jnp.concatenate([z8, sin, zero], axis=1)
    rep = LANES // ATT_HD
    return jnp.concatenate([jnp.tile(c, (1, rep)), jnp.tile(s_hi, (1, rep)), jnp.tile(s_lo, (1, rep))], axis=1)


def _sgu_tables(w_s, b_s, tn):
    tril = jnp.tril(jnp.ones((CHUNK, CHUNK), dtype=bool))
    ws = jnp.where(tril, w_s, 0)
    hd = D_SGU // SGU_HEADS
    bias_p = jnp.repeat(b_s.T, hd, axis=1)
    rep = CHUNK // tn
    eye = jnp.eye(rep, dtype=w_s.dtype)
    ws_s = jnp.einsum('ab,gts->gatbs', eye, ws[:, :tn, :tn]).reshape(SGU_HEADS, CHUNK, CHUNK)
    bias_s = jnp.tile(bias_p[:tn], (rep, 1))
    return ws.astype(BF16), bias_p.astype(F32), ws_s.astype(BF16), bias_s.astype(F32)


def kernel(x_prompt, x_sample, cache_k, cache_v, state_ssm_re, state_ssm_im, state_conv, page_table, ln_in_g, ln_in_b, w_in, w_gate, b_gate, s5_lam_re, s5_lam_im, s5_log_dt, s5_b_re, s5_b_im, s5_c_re, s5_c_im, s5_d, s5_w_glu, sgu_ln_g, sgu_ln_b, sgu_w_s, sgu_b_s, conv_w, conv_b, conv_ln_g, conv_ln_b, att_lam_q1, att_lam_k1, att_lam_q2, att_lam_k2, att_subln_g, w_br_a, w_br_b, w_br_c, w_br_d, w_o, ln1_g, ln1_b, ffn_wg, ffn_wu, ffn_wd, moe_router, moe_router_b, moe_wg, moe_wu, moe_wd, ln2_g, ln2_b):
    bp, tp, d = x_prompt.shape
    bs, ts, _ = x_sample.shape
    depth = w_in.shape[0]
    n_pages = page_table.shape[1]
    past = n_pages * PAGE_SIZE
    alpha = (2 * depth) ** 0.25
    n_p, n_s = bp * tp, bs * ts
    tm_p = min(512, tp)
    tq = min(512, tp)
    l_p = 16
    pg = min(8, n_pages)
    assert tp % tm_p == 0 and tp % CHUNK == 0 and tp % tq == 0 and tp % l_p == 0 and tp >= CONV_W - 1
    assert n_s % CHUNK == 0 and CHUNK % ts == 0 and ts % SUBLANES == 0 and n_pages % pg == 0 and ts <= l_p

    row1 = lambda a: a.reshape(1, -1).astype(F32)
    tab_p = _rope_table(jnp.arange(tp, dtype=jnp.int32))
    tab_s = jnp.tile(_rope_table(past + jnp.arange(ts, dtype=jnp.int32)), (bs, 1))
    zero_state = jnp.zeros((bp, S5_GROUPS, S5_STATE), F32)

    xp = _layernorm(x_prompt.reshape(n_p, d), ln_in_g, ln_in_b, tm_p)
    xs = _layernorm(x_sample.reshape(n_s, d), ln_in_g, ln_in_b, n_s)

    outs = [[] for _ in range(11)]
    for l in range(depth):
        lam_init = 0.8 - 0.6 * math.exp(-0.3 * l)
        lam = (jnp.exp(jnp.sum(att_lam_q1[l].astype(F32) * att_lam_k1[l].astype(F32)))
               - jnp.exp(jnp.sum(att_lam_q2[l].astype(F32) * att_lam_k2[l].astype(F32))) + lam_init)
        lamv = jnp.full((1, ATT_VD), lam, F32)
        gsv = row1(att_subln_g[l]) * (1.0 - lam_init)
        w_in_b = w_in[l].astype(BF16)
        ws_p, sb_p, ws_s, sb_s = _sgu_tables(sgu_w_s[l], sgu_b_s[l], ts)
        conv_args = (conv_w[l].astype(F32), row1(conv_b[l]), row1(conv_ln_g[l]), row1(conv_ln_b[l]))
        sgu_ln = (row1(sgu_ln_g[l]), row1(sgu_ln_b[l]))
        s5_par = (s5_lam_re[l], s5_lam_im[l], s5_log_dt[l], s5_b_re[l], s5_b_im[l], s5_c_re[l], s5_c_im[l])
        mw = {'w_glu': s5_w_glu[l].astype(BF16), 'w_gate': w_gate[l].astype(BF16), 'b_gate': row1(b_gate[l]),
              'w_br_a': w_br_a[l].astype(BF16), 'w_br_b': w_br_b[l].astype(BF16),
              'w_br_c': w_br_c[l].astype(BF16), 'w_br_d': w_br_d[l].astype(BF16), 'w_o': w_o[l].astype(BF16),
              'ln1_g': row1(ln1_g[l]), 'ln1_b': row1(ln1_b[l])}
        ln2 = (row1(ln2_g[l]), row1(ln2_b[l]))
        i = l // 2
        if l % 2 == 0:
            fw = (ffn_wg[i].astype(BF16), ffn_wu[i].astype(BF16), ffn_wd[i].astype(BF16))
            ffn = lambda x, tm: _ffn(x, *fw, *ln2, tm=tm, alpha=alpha)
        else:
            wr = jnp.pad(moe_router[i].astype(F32), ((0, 0), (0, LANES - N_EXPERTS)))
            wr_hi, wr_lo = _split_bf16(wr)
            br = jnp.pad(moe_router_b[i].astype(F32), (0, LANES - N_EXPERTS)).reshape(1, LANES)
            fw = (moe_wg[i].astype(BF16), moe_wu[i].astype(BF16), moe_wd[i].astype(BF16))
            ffn = lambda x, tm: _moe(x, wr_hi, wr_lo, br, *fw, *ln2, tm=tm, alpha=alpha)

        ua, yb, yc, glu, qb, k, v, kb, vx = _proj_prompt(xp, w_in_b, tab_p, *sgu_ln, ws_p, sb_p, *conv_args,
                                                         seq=tp, tm=tm_p, chunk=l_p)
        s5_tab, pw_re, pw_im = _s5_mats(*s5_par, l_p)
        ya, sr_p, si_p = _s5_branch(ua, zero_state, zero_state, s5_tab, pw_re[l_p], pw_im[l_p], s5_d[l],
                                    nb=bp, t=tp)
        yd = _attn_prompt(qb, kb, vx, lamv, gsv, nb=bp, t=tp, tq=tq)
        xp = _merge(xp, ya, yb, yc, yd, mw, tm=tm_p, alpha=alpha)
        xp = ffn(xp, tm_p)
        outs[0].append(k.reshape(bp, ATT_HEADS, 2, ATT_HD, tp).transpose(0, 4, 1, 2, 3))
        outs[1].append(v.reshape(bp, tp, ATT_HEADS, ATT_VD))
        outs[2].append(sr_p)
        outs[3].append(si_p)
        outs[4].append(glu.reshape(bp, tp, D_CONV)[:, tp - (CONV_W - 1):])

        pfx = jnp.pad(state_conv[l].astype(F32), ((0, 0), (HALO_OFF, 0), (0, 0)))
        ua, yb, yc, glu, q, k, v, vn = _proj_sample(xs, w_in_b, tab_s, *sgu_ln, ws_s, sb_s, *conv_args, pfx,
                                                    nb=bs, tn=ts)
        ya, sr_s, si_s = _s5_branch(ua, state_ssm_re[l], state_ssm_im[l], s5_tab, pw_re[ts], pw_im[ts], s5_d[l],
                                    nb=bs, t=ts)
        yd = _attn_sample(q, k, v, cache_k, cache_v, page_table, l, lamv, gsv, nb=bs, tn=ts, pg=pg)
        xs = _merge(xs, ya, yb, yc, yd, mw, tm=n_s, alpha=alpha)
        xs = ffn(xs, n_s)
        outs[5].append(k.reshape(bs, ts, ATT_HEADS, 2, ATT_HD))
        outs[6].append(v.reshape(bs, ts, ATT_HEADS, ATT_VD))
        outs[7].append(sr_s)
        outs[8].append(si_s)
        full = jnp.concatenate([state_conv[l].astype(F32), glu.reshape(bs, ts, D_CONV)], axis=1)
        outs[9].append(full[:, ts:])
        outs[10].append(vn.reshape(bs, ts, D_SGU))

    return (xp.reshape(bp, tp, d), xs.reshape(bs, ts, d), *[jnp.stack(o) for o in outs])
```

```python
import functools
import math
from typing import NamedTuple

import jax
import jax.numpy as jnp
from jax import lax
from jax.experimental import pallas as pl
from jax.experimental.pallas import tpu as pltpu

F32 = jnp.float32
BF16 = jnp.bfloat16

D_MODEL = 1024
PAGE_SIZE = 128
S5_GROUP = 16
S5_GROUPS = 16
D_S5 = 256
S5_STATE = 64
D_SGU = 256
SGU_HEADS = 4
CHUNK = 128
D_CONV = 256
CONV_W = 31
ATT_HEADS = 4
ATT_HD = 64
ATT_VD = 128
D_ATT = 512
ROT_DIM = 16
ROPE_THETA = 500000.0
ATT_SCALE = ATT_HD ** -0.5
LOG2E = 1.4426950408889634
NEG_INF = -1e30
D_IN = 2816
N_BRANCH = 4
N_EXPERTS = 8
TOP_K = 2
LN_EPS = 1e-5
RMS_EPS = 1e-5

LANES = 128
SUBLANES = 8
HALO = 32
HALO_OFF = HALO - (CONV_W - 1)
VMEM_LIMIT = 56 * 1024 * 1024


def _cparams(sem):
    return pltpu.CompilerParams(dimension_semantics=sem, vmem_limit_bytes=VMEM_LIMIT)


def _const_spec(shape):
    nd = len(shape)
    return pl.BlockSpec(shape, lambda *_: (0,) * nd)


class _Layer(NamedTuple):
    stack: jax.Array
    index: int


def _layer_spec(w):
    shape = w.stack.shape[1:]
    return pl.BlockSpec((None,) + shape, lambda *_: (w.index,) + (0,) * len(shape))


def _ln(x, g, b):
    mu = jnp.mean(x, axis=-1, keepdims=True)
    xc = x - mu
    var = jnp.mean(xc * xc, axis=-1, keepdims=True)
    return xc * lax.rsqrt(var + LN_EPS) * g + b


def _sigmoid(x):
    return 1.0 / (1.0 + jnp.exp(-x))


def _silu(x):
    return x * _sigmoid(x)


def _gelu(x):
    return jax.nn.gelu(x, approximate=True)


def _mm(a, b):
    return jnp.dot(a.astype(BF16), b.astype(BF16), preferred_element_type=F32)


def _mm_nt(a, b):
    return lax.dot_general(a.astype(BF16), b.astype(BF16), (((1,), (1,)), ((), ())),
                           preferred_element_type=F32)


def _rope(x, tab):
    n_rep = x.shape[1] // LANES
    c = jnp.concatenate([tab[:, 0:LANES]] * n_rep, axis=1)
    s_hi = jnp.concatenate([tab[:, LANES:2 * LANES]] * n_rep, axis=1)
    s_lo = jnp.concatenate([tab[:, 2 * LANES:3 * LANES]] * n_rep, axis=1)
    half = ROT_DIM // 2
    x_up = pltpu.roll(x, x.shape[1] - half, 1)
    x_dn = pltpu.roll(x, half, 1)
    return x * c + x_up * s_hi + x_dn * s_lo


def _ln_kernel(x_ref, g_ref, b_ref, o_ref):
    o_ref[...] = _ln(x_ref[...], g_ref[...], b_ref[...])


def _layernorm(x, g, b, tm):
    n, d = x.shape
    return pl.pallas_call(
        _ln_kernel,
        grid=(n // tm,),
        in_specs=[pl.BlockSpec((tm, d), lambda i: (i, 0)), _const_spec((1, d)), _const_spec((1, d))],
        out_specs=pl.BlockSpec((tm, d), lambda i: (i, 0)),
        out_shape=jax.ShapeDtypeStruct((n, d), F32),
        compiler_params=_cparams(("parallel",)),
    )(x, g.reshape(1, d), b.reshape(1, d))


def _proj_split(x, w_ref, tab, sgu_g, sgu_b, q_scale):
    h = _mm(x, w_ref[...])
    ua = h[:, 0:256]
    gub = _gelu(h[:, 256:512])
    vn = _ln(_gelu(h[:, 512:768]), sgu_g, sgu_b)
    glu = h[:, 768:1024] * _sigmoid(h[:, 1024:1280])
    q = _rope(h[:, 1280:1792], tab) * q_scale
    k = _rope(h[:, 1792:2304], tab)
    v = h[:, 2304:2816]
    return ua, gub, vn, glu, q, k, v


def _sgu(gub, vn, ws_ref, sb_ref):
    tm = vn.shape[0]
    head_of_lane = lax.broadcasted_iota(jnp.int32, (CHUNK, D_SGU), 1) // (D_SGU // SGU_HEADS)
    outs = []
    for c in range(tm // CHUNK):
        vc = vn[c * CHUNK:(c + 1) * CHUNK]
        z = sb_ref[...]
        for g in range(SGU_HEADS):
            z = z + _mm(ws_ref[g], jnp.where(head_of_lane == g, vc, 0.0))
        outs.append(gub[c * CHUNK:(c + 1) * CHUNK] * z)
    return outs[0] if len(outs) == 1 else jnp.concatenate(outs, axis=0)


def _conv_post(acc, cb, g, b):
    return _silu(_ln(acc + cb, g, b))


def _to_planes(rows, tmp_ref, planes_ref):
    n_planes, per_plane = planes_ref.shape[0], planes_ref.shape[1]
    for h in range(tmp_ref.shape[0]):
        tmp_ref[h] = rows[:, h * LANES:(h + 1) * LANES]
    for i in range(n_planes):
        for h in range(tmp_ref.shape[0]):
            planes_ref[i, :, h * LANES:(h + 1) * LANES] = tmp_ref[h, pl.ds(i, per_plane, stride=n_planes), :]


def _from_planes(planes_ref, tmp_ref):
    n_planes, per_plane = planes_ref.shape[0], planes_ref.shape[1]
    for i in range(n_planes):
        for h in range(tmp_ref.shape[0]):
            tmp_ref[h, pl.ds(i, per_plane, stride=n_planes), :] = planes_ref[i, :, h * LANES:(h + 1) * LANES]
    return jnp.concatenate([tmp_ref[h] for h in range(tmp_ref.shape[0])], axis=1)


def _proj_prompt_kernel(x_ref, w_ref, tab_ref, sg_ref, sbn_ref, ws_ref, sb_ref, cw_ref, cb_ref, cg_ref, cbn_ref,
                        up_ref, yb_ref, yc_ref, glu_ref, q_ref, k_ref, v_ref, kb_ref, vx_ref,
                        ext_ref, ua_ref, *, tm, tiles_per_seq, rb):
    i = pl.program_id(0)
    ua, gub, vn, glu, q, k, v = _proj_split(x_ref[...], w_ref, tab_ref[...], sg_ref[...], sbn_ref[...],
                                            ATT_SCALE * LOG2E)
    _to_planes(ua, ua_ref, up_ref)
    glu_ref[...] = glu
    q_ref[...] = q.astype(BF16)
    k_ref[...] = k.T
    for h in range(ATT_HEADS):
        v_ref[pl.ds(h, tm, stride=ATT_HEADS), :] = v[:, h * ATT_VD:(h + 1) * ATT_VD]
    kb_ref[...] = k.astype(BF16)
    vb = v.astype(BF16)
    ones = jnp.ones((tm, ATT_VD), BF16)
    vx_ref[...] = jnp.concatenate(
        [piece for h in range(ATT_HEADS) for piece in (vb[:, h * ATT_VD:(h + 1) * ATT_VD], ones)], axis=1)
    yb_ref[...] = _sgu(gub, vn, ws_ref, sb_ref)

    @pl.when(i % tiles_per_seq == 0)
    def _():
        ext_ref[0:HALO, :] = jnp.zeros((HALO, D_CONV), F32)

    ext_ref[HALO:HALO + tm, :] = glu
    for r in range(tm // rb):
        base = r * rb
        acc = jnp.zeros((rb, D_CONV), F32)
        for b in range(SUBLANES):
            zlen = rb if b == 0 else rb + SUBLANES
            z = jnp.zeros((zlen, D_CONV), F32)
            for a in range(HALO // SUBLANES + 1):
                w = SUBLANES * a + b - HALO_OFF
                if 0 <= w < CONV_W:
                    z = z + ext_ref[pl.ds(base + SUBLANES * a, zlen), :] * cw_ref[w:w + 1, :]
            acc = acc + z[b:b + rb]
        yc_ref[base:base + rb, :] = _conv_post(acc, cb_ref[...], cg_ref[...], cbn_ref[...])
    ext_ref[0:HALO, :] = ext_ref[tm:tm + HALO, :]


def _proj_prompt(x, w_in, tab, sgu_g, sgu_b, ws, sb, cw, cb, cg, cbn, *, seq, tm, chunk):
    n = x.shape[0]
    tiles_per_seq = seq // tm
    row = lambda width: pl.BlockSpec((tm, width), lambda i: (i, 0))
    f = lambda width, dt=F32: jax.ShapeDtypeStruct((n, width), dt)
    planes_spec = pl.BlockSpec((chunk, tm // chunk, D_S5), lambda i: (0, i, 0))
    planes_shape = jax.ShapeDtypeStruct((chunk, n // chunk, D_S5), F32)
    kern = functools.partial(_proj_prompt_kernel, tm=tm, tiles_per_seq=tiles_per_seq, rb=64)
    return pl.pallas_call(
        kern,
        grid=(n // tm,),
        in_specs=[row(D_MODEL), _layer_spec(w_in),
                  pl.BlockSpec((tm, 3 * LANES), lambda i: (i % tiles_per_seq, 0)),
                  _const_spec((1, D_SGU)), _const_spec((1, D_SGU)),
                  _const_spec((SGU_HEADS, CHUNK, CHUNK)), _const_spec((CHUNK, D_SGU)),
                  _const_spec((CONV_W, D_CONV)), _const_spec((1, D_CONV)), _const_spec((1, D_CONV)),
                  _const_spec((1, D_CONV))],
        out_specs=[planes_spec, row(D_SGU), row(D_CONV), row(D_CONV), row(D_ATT),
                   pl.BlockSpec((None, D_ATT, tm), lambda i: (i // tiles_per_seq, 0, i % tiles_per_seq)),
                   pl.BlockSpec((ATT_HEADS * tm, ATT_VD), lambda i: (i, 0)),
                   row(D_ATT), row(2 * D_ATT)],
        out_shape=[planes_shape, f(D_SGU), f(D_CONV), f(D_CONV), f(D_ATT, BF16),
                   jax.ShapeDtypeStruct((n // seq, D_ATT, seq), F32),
                   jax.ShapeDtypeStruct((ATT_HEADS * n, ATT_VD), F32),
                   f(D_ATT, BF16), f(2 * D_ATT, BF16)],
        scratch_shapes=[pltpu.VMEM((tm + HALO, D_CONV), F32), pltpu.VMEM((D_S5 // LANES, tm, LANES), F32)],
        compiler_params=_cparams(("arbitrary",)),
    )(x, w_in.stack, tab, sgu_g, sgu_b, ws, sb, cw, cb, cg, cbn)


def _proj_sample_kernel(x_ref, w_ref, tab_ref, sg_ref, sbn_ref, ws_ref, sb_ref, cw_ref, cb_ref, cg_ref, cbn_ref,
                        pfx_ref,
                        up_ref, yb_ref, yc_ref, glu_ref, q_ref, k_ref, v_ref, vn_ref,
                        ext_ref, ua_ref, *, nb, tn):
    ua, gub, vn, glu, q, k, v = _proj_split(x_ref[...], w_ref, tab_ref[...], sg_ref[...], sbn_ref[...], ATT_SCALE)
    _to_planes(ua, ua_ref, up_ref)
    glu_ref[...] = glu
    q_ref[...] = q
    k_ref[...] = k
    v_ref[...] = v
    vn_ref[...] = vn
    yb_ref[...] = _sgu(gub, vn, ws_ref, sb_ref)

    ext_ref[:, 0:HALO, :] = pfx_ref[...]
    ext_ref[:, HALO:HALO + tn, :] = glu.reshape(nb, tn, D_CONV)
    acc = jnp.zeros((nb, tn, D_CONV), F32)
    for w in range(CONV_W):
        acc = acc + ext_ref[:, pl.ds(HALO_OFF + w, tn), :] * cw_ref[w:w + 1, :]
    yc_ref[...] = _conv_post(acc.reshape(nb * tn, D_CONV), cb_ref[...], cg_ref[...], cbn_ref[...])


def _proj_sample(x, w_in, tab, sgu_g, sgu_b, ws, sb, cw, cb, cg, cbn, pfx, *, nb, tn):
    n = x.shape[0]
    full = lambda width: _const_spec((n, width))
    f = lambda width: jax.ShapeDtypeStruct((n, width), F32)
    kern = functools.partial(_proj_sample_kernel, nb=nb, tn=tn)
    return pl.pallas_call(
        kern,
        grid=(1,),
        in_specs=[full(D_MODEL), _layer_spec(w_in), full(3 * LANES),
                  _const_spec((1, D_SGU)), _const_spec((1, D_SGU)),
                  _const_spec((SGU_HEADS, CHUNK, CHUNK)), _const_spec((CHUNK, D_SGU)),
                  _const_spec((CONV_W, D_CONV)), _const_spec((1, D_CONV)), _const_spec((1, D_CONV)),
                  _const_spec((1, D_CONV)), _const_spec((nb, HALO, D_CONV))],
        out_specs=[_const_spec((tn, nb, D_S5)), full(D_SGU), full(D_CONV), full(D_CONV), full(D_ATT), full(D_ATT),
                   full(D_ATT), full(D_SGU)],
        out_shape=[jax.ShapeDtypeStruct((tn, nb, D_S5), F32), f(D_SGU), f(D_CONV), f(D_CONV), f(D_ATT), f(D_ATT),
                   f(D_ATT), f(D_SGU)],
        scratch_shapes=[pltpu.VMEM((nb, HALO + tn, D_CONV), F32), pltpu.VMEM((D_S5 // LANES, n, LANES), F32)],
        compiler_params=_cparams(("arbitrary",)),
    )(x, w_in.stack, tab, sgu_g, sgu_b, ws, sb, cw, cb, cg, cbn, pfx)


def _s5_mats(lam_re, lam_im, log_dt, b_re, b_im, c_re, c_im, L):
    G, P, C = S5_GROUPS, S5_STATE, S5_GROUP
    lr, li = lam_re.astype(F32), lam_im.astype(F32)
    dt = jnp.exp(log_dt.astype(F32))[:, None]
    steps = jnp.arange(L + 1, dtype=F32)[:, None, None]
    mag = jnp.exp((lr * dt)[None] * steps)
    pr = mag * jnp.cos((li * dt)[None] * steps)
    pi = mag * jnp.sin((li * dt)[None] * steps)
    nr, ni = pr[1] - 1.0, pi[1]
    den = lr * lr + li * li
    fr = (nr * lr + ni * li) / den
    fi = (ni * lr - nr * li) / den
    bre, bim = b_re.astype(F32), b_im.astype(F32)
    br = fr[..., None] * bre - fi[..., None] * bim
    bi = fr[..., None] * bim + fi[..., None] * bre
    cr, ci = c_re.astype(F32), c_im.astype(F32)
    kr = (jnp.einsum('gcp,tgp,gpd->tgdc', cr, pr[:L], br) - jnp.einsum('gcp,tgp,gpd->tgdc', cr, pi[:L], bi)
          - jnp.einsum('gcp,tgp,gpd->tgdc', ci, pr[:L], bi) - jnp.einsum('gcp,tgp,gpd->tgdc', ci, pi[:L], br))

    def block_diag(compact):
        _, _, r, c = compact.shape
        repeat = jnp.tile(jnp.eye(c, dtype=F32), (1, G))
        tiled = jnp.einsum('lrc,cq->lrq', compact.reshape(L, G * r, c), repeat, precision=lax.Precision.HIGHEST)
        row_g = lax.broadcasted_iota(jnp.int32, (G * r, G * c), 0) // r
        col_g = lax.broadcasted_iota(jnp.int32, (G * r, G * c), 1) // c
        return jnp.where(row_g == col_g, tiled, 0.0).astype(BF16)

    kbd = block_diag(kr)
    rev = L - 1 - jnp.arange(L)
    brt, bit = br.transpose(0, 2, 1), bi.transpose(0, 2, 1)
    m1re = block_diag(pr[rev][:, :, None, :] * brt[None] - pi[rev][:, :, None, :] * bit[None])
    m1im = block_diag(pr[rev][:, :, None, :] * bit[None] + pi[rev][:, :, None, :] * brt[None])
    crt, cit = cr.transpose(0, 2, 1), ci.transpose(0, 2, 1)
    nr, ni = pr[1:L + 1][..., None], pi[1:L + 1][..., None]
    m2re = block_diag(crt[None] * nr - cit[None] * ni)
    m2im = block_diag(-(crt[None] * ni + cit[None] * nr))
    return (kbd, m1re, m1im, m2re, m2im), pr.reshape(L + 1, 1, G * P), pi.reshape(L + 1, 1, G * P)


def _s5_local_kernel(up_ref, m1re_ref, m1im_ref, slre_ref, slim_ref):
    @pl.when(pl.program_id(0) == 0)
    def _():
        slre_ref[...] = jnp.zeros_like(slre_ref)
        slim_ref[...] = jnp.zeros_like(slim_ref)

    u = up_ref[...].astype(BF16)
    slre_ref[...] += jnp.dot(u, m1re_ref[...], preferred_element_type=F32)
    slim_ref[...] += jnp.dot(u, m1im_ref[...], preferred_element_type=F32)


def _s5_scan_kernel(slre_ref, slim_ref, s0re_ref, s0im_ref, are_ref, aim_ref,
                    stre_ref, stim_ref, fre_ref, fim_ref, *, nb, cpb):
    ar = are_ref[...]
    ai = aim_ref[...]

    def body(c, carry):
        out = []
        for b in range(nb):
            sr, si = carry[b]
            row = pl.ds(b * cpb + c, 1)
            stre_ref[row, :] = sr
            stim_ref[row, :] = si
            out.append((ar * sr - ai * si + slre_ref[row, :], ar * si + ai * sr + slim_ref[row, :]))
        return tuple(out)

    init = tuple((s0re_ref[b:b + 1, :], s0im_ref[b:b + 1, :]) for b in range(nb))
    fin = lax.fori_loop(0, cpb, body, init)
    for b in range(nb):
        fre_ref[b:b + 1, :] = fin[b][0]
        fim_ref[b:b + 1, :] = fin[b][1]


def _s5_out_kernel(up_ref, kbd_ref, stre_ref, stim_ref, m2re_ref, m2im_ref, d_ref, o_ref):
    i = pl.program_id(0)
    o_ref[...] = (_mm(stre_ref[...], m2re_ref[...]) + _mm(stim_ref[...], m2im_ref[...])
                  + d_ref[...] * up_ref[i])

    def body(tau, carry):
        o_ref[...] += _mm(up_ref[i - tau], kbd_ref[tau])
        return carry

    lax.fori_loop(0, i + 1, body, 0)
    o_ref[...] = _gelu(o_ref[...])


def _s5_branch(up, s0_re, s0_im, tables, a_re, a_im, d_skip, *, nb, t):
    kbd, m1re, m1im, m2re, m2im = tables
    L, ch, dc = up.shape
    l_tab = kbd.shape[0]
    m1_off = l_tab - L
    G, P = S5_GROUPS, S5_STATE
    gp = G * P
    cpb = t // L
    acc_spec = _const_spec((ch, gp))
    acc_shape = jax.ShapeDtypeStruct((ch, gp), F32)
    slre, slim = pl.pallas_call(
        _s5_local_kernel,
        grid=(L,),
        in_specs=[pl.BlockSpec((None, ch, dc), lambda j: (j, 0, 0)),
                  pl.BlockSpec((None, dc, gp), lambda j: (j + m1_off, 0, 0)),
                  pl.BlockSpec((None, dc, gp), lambda j: (j + m1_off, 0, 0))],
        out_specs=[acc_spec, acc_spec],
        out_shape=[acc_shape, acc_shape],
        compiler_params=_cparams(("arbitrary",)),
    )(up, m1re, m1im)
    fin_shape = jax.ShapeDtypeStruct((nb, gp), F32)
    stre, stim, fre, fim = pl.pallas_call(
        functools.partial(_s5_scan_kernel, nb=nb, cpb=cpb),
        out_shape=[acc_shape, acc_shape, fin_shape, fin_shape],
        compiler_params=pltpu.CompilerParams(vmem_limit_bytes=VMEM_LIMIT),
    )(slre, slim, s0_re.reshape(nb, gp).astype(F32), s0_im.reshape(nb, gp).astype(F32), a_re, a_im)
    yp = pl.pallas_call(
        _s5_out_kernel,
        grid=(L,),
        in_specs=[_const_spec((L, ch, dc)), _const_spec((l_tab, dc, dc)), acc_spec, acc_spec,
                  pl.BlockSpec((None, gp, dc), lambda i: (i, 0, 0)),
                  pl.BlockSpec((None, gp, dc), lambda i: (i, 0, 0)),
                  _const_spec((1, dc))],
        out_specs=pl.BlockSpec((None, ch, dc), lambda i: (i, 0, 0)),
        out_shape=jax.ShapeDtypeStruct((L, ch, dc), F32),
        compiler_params=_cparams(("parallel",)),
    )(up, kbd, stre, stim, m2re, m2im, d_skip.astype(F32).reshape(1, dc))
    return yp, fre.reshape(nb, G, P), fim.reshape(nb, G, P)


def _diff_finish(o1, o2, lam, gs):
    o = o1 - lam * o2
    return o * lax.rsqrt(jnp.mean(o * o, axis=-1, keepdims=True) + RMS_EPS) * gs


def _attn_prompt_kernel(q_ref, k_ref, v_ref, lam_ref, gs_ref, o_ref, q2_ref, m_ref, acc_ref, *, tq, rb, unroll):
    i = pl.program_id(2)
    r2 = 2 * tq
    q = q_ref[...]
    lane = lax.broadcasted_iota(jnp.int32, (tq, ATT_VD), 1)
    zero = jnp.zeros_like(q)
    q2_ref[0:tq, :] = jnp.where(lane < ATT_HD, q, zero)
    q2_ref[tq:r2, :] = jnp.where(lane >= ATT_HD, q, zero)
    m_ref[...] = jnp.full((r2, LANES), NEG_INF, F32)
    acc_ref[...] = jnp.zeros((r2, 2 * ATT_VD), F32)

    def step(j, diagonal):
        off = pl.multiple_of(j * tq, tq)
        kt = k_ref[pl.ds(off, tq), :]
        vt = v_ref[pl.ds(off, tq), :]
        for r in range(r2 // rb):
            rows = slice(r * rb, (r + 1) * rb)
            s = _mm_nt(q2_ref[rows, :], kt)
            if diagonal:
                row = (lax.broadcasted_iota(jnp.int32, (rb, tq), 0) + r * rb) % tq
                col = lax.broadcasted_iota(jnp.int32, (rb, tq), 1)
                s = jnp.where(col <= row, s, NEG_INF)
            m_prev = m_ref[rows, :]
            m_new = jnp.maximum(m_prev, jnp.max(s, axis=-1, keepdims=True))
            alpha = jnp.exp2(m_prev - m_new)
            p = jnp.exp2(s - jnp.concatenate([m_new] * (tq // LANES), axis=1))
            acc_ref[rows, :] = jnp.concatenate([alpha, alpha], axis=1) * acc_ref[rows, :] + _mm(p, vt)
            m_ref[rows, :] = m_new

    def group(g, carry):
        for u in range(unroll):
            step(g * unroll + u, False)
        return carry

    def single(j, carry):
        step(j, False)
        return carry

    n_grp = i // unroll
    lax.fori_loop(0, n_grp, group, 0)
    lax.fori_loop(n_grp * unroll, i, single, 0)
    step(i, True)
    acc = acc_ref[...]
    o = acc[:, 0:ATT_VD] / acc[:, ATT_VD:]
    o_ref[...] = _diff_finish(o[0:tq], o[tq:r2], lam_ref[...], gs_ref[...])


def _attn_prompt(qb, kb, vx, lamv, gsv, *, nb, t, tq):
    n = nb * t
    nq = t // tq
    return pl.pallas_call(
        functools.partial(_attn_prompt_kernel, tq=tq, rb=min(128, tq), unroll=4),
        grid=(nb, ATT_HEADS, nq),
        in_specs=[pl.BlockSpec((tq, ATT_VD), lambda b, h, i: (b * nq + i, h)),
                  pl.BlockSpec((t, ATT_VD), lambda b, h, i: (b, h)),
                  pl.BlockSpec((t, 2 * ATT_VD), lambda b, h, i: (b, h)),
                  _const_spec((1, ATT_VD)), _const_spec((1, ATT_VD))],
        out_specs=pl.BlockSpec((tq, ATT_VD), lambda b, h, i: (b * nq + i, h)),
        out_shape=jax.ShapeDtypeStruct((n, D_ATT), F32),
        scratch_shapes=[pltpu.VMEM((2 * tq, ATT_VD), BF16), pltpu.VMEM((2 * tq, LANES), F32),
                        pltpu.VMEM((2 * tq, 2 * ATT_VD), F32)],
        compiler_params=_cparams(("parallel", "parallel", "arbitrary")),
    )(qb, kb, vx, lamv, gsv)


def _attn_sample_kernel(pt_ref, q_ref, kn_ref, vn_ref, lam_ref, gs_ref, *rest, tn, pg):
    k_refs = rest[0:pg]
    v_refs = rest[pg:2 * pg]
    o_ref = rest[2 * pg]
    qbd_ref, m_ref, l_ref, acc_ref = rest[2 * pg + 1:]
    p_idx = pl.program_id(1)
    rows_h = 2 * tn
    n_rows = ATT_HEADS * rows_h

    def update(s, v_pages):
        n_rep = s.shape[1] // LANES
        m_prev = m_ref[...]
        m_new = jnp.maximum(m_prev, jnp.max(s, axis=-1, keepdims=True))
        p = jnp.exp(s - jnp.concatenate([m_new] * n_rep, axis=1))
        alpha = jnp.exp(m_prev - m_new)
        l_ref[...] = alpha * l_ref[...] + jnp.sum(p, axis=-1, keepdims=True)
        for h in range(ATT_HEADS):
            rows = slice(h * rows_h, (h + 1) * rows_h)
            v_h = [r[pl.ds(h, PAGE_SIZE, stride=ATT_HEADS), :].astype(BF16) for r in v_pages]
            v_h = v_h[0] if len(v_h) == 1 else jnp.concatenate(v_h, axis=0)
            acc_ref[rows, :] = alpha[rows] * acc_ref[rows, :] + _mm(p[rows], v_h)
        m_ref[...] = m_new

    @pl.when(p_idx == 0)
    def _():
        q = q_ref[...]
        lane_grp = lax.broadcasted_iota(jnp.int32, (tn, D_ATT), 1) // ATT_HD
        pieces = [jnp.where(lane_grp == g, q, 0.0) for g in range(2 * ATT_HEADS)]
        qbd_ref[...] = jnp.concatenate(pieces, axis=0).astype(BF16)
        m_ref[...] = jnp.full((n_rows, LANES), NEG_INF, F32)
        l_ref[...] = jnp.zeros((n_rows, LANES), F32)
        acc_ref[...] = jnp.zeros((n_rows, ATT_VD), F32)
        s = _mm(qbd_ref[...], kn_ref[...])
        t_of_row = lax.broadcasted_iota(jnp.int32, (n_rows, PAGE_SIZE), 0) % tn
        col = lax.broadcasted_iota(jnp.int32, (n_rows, PAGE_SIZE), 1)
        update(jnp.where(col <= t_of_row, s, NEG_INF), [vn_ref])

    kcat = jnp.concatenate([r[...].astype(BF16) for r in k_refs], axis=1)
    update(_mm(qbd_ref[...], kcat), list(v_refs))

    @pl.when(p_idx == pl.num_programs(1) - 1)
    def _():
        o = acc_ref[...] / l_ref[...]
        outs = []
        for h in range(ATT_HEADS):
            o1 = o[h * rows_h:h * rows_h + tn]
            o2 = o[h * rows_h + tn:(h + 1) * rows_h]
            outs.append(_diff_finish(o1, o2, lam_ref[...], gs_ref[...]))
        o_ref[...] = jnp.concatenate(outs, axis=1)


def _attn_sample(q, k_new, v_new, cache_k, cache_v, page_table, layer, lamv, gsv, *, nb, tn, pg):
    n_pages = page_table.shape[1]
    depth, n_pool = cache_k.shape[0], cache_k.shape[1]
    rows_v = PAGE_SIZE * ATT_HEADS
    ck = jnp.transpose(cache_k, (0, 1, 3, 4, 5, 2)).reshape(depth, n_pool, D_ATT, PAGE_SIZE)
    cv = cache_v.reshape(depth, n_pool, rows_v, ATT_VD)
    kn = jnp.pad(k_new.reshape(nb, tn, D_ATT).transpose(0, 2, 1), ((0, 0), (0, 0), (0, PAGE_SIZE - tn)))
    vn = jnp.pad(v_new.reshape(nb, tn * ATT_HEADS, ATT_VD), ((0, 0), (0, rows_v - tn * ATT_HEADS), (0, 0)))
    n_rows = 2 * ATT_HEADS * tn

    def page_spec(r, rows, cols):
        return pl.BlockSpec((None, None, rows, cols), lambda b, p, pt: (layer, pt[b, p * pg + r], 0, 0))

    grid_spec = pltpu.PrefetchScalarGridSpec(
        num_scalar_prefetch=1,
        grid=(nb, n_pages // pg),
        in_specs=[pl.BlockSpec((tn, D_ATT), lambda b, p, pt: (b, 0)),
                  pl.BlockSpec((None, D_ATT, PAGE_SIZE), lambda b, p, pt: (b, 0, 0)),
                  pl.BlockSpec((None, rows_v, ATT_VD), lambda b, p, pt: (b, 0, 0)),
                  pl.BlockSpec((1, ATT_VD), lambda b, p, pt: (0, 0)),
                  pl.BlockSpec((1, ATT_VD), lambda b, p, pt: (0, 0))]
                 + [page_spec(r, D_ATT, PAGE_SIZE) for r in range(pg)]
                 + [page_spec(r, rows_v, ATT_VD) for r in range(pg)],
        out_specs=pl.BlockSpec((tn, D_ATT), lambda b, p, pt: (b, 0)),
        scratch_shapes=[pltpu.VMEM((n_rows, D_ATT), BF16), pltpu.VMEM((n_rows, LANES), F32),
                        pltpu.VMEM((n_rows, LANES), F32), pltpu.VMEM((n_rows, ATT_VD), F32)],
    )
    return pl.pallas_call(
        functools.partial(_attn_sample_kernel, tn=tn, pg=pg),
        grid_spec=grid_spec,
        out_shape=jax.ShapeDtypeStruct((nb * tn, D_ATT), F32),
        compiler_params=_cparams(("parallel", "arbitrary")),
    )(page_table, q, kn, vn, lamv, gsv, *([ck] * pg), *([cv] * pg))


def _merge_kernel(x_ref, ya_ref, yb_ref, yc_ref, yd_ref, wglu_ref, wgate_ref, bgate_ref,
                  wa_ref, wb_ref, wc_ref, wd_ref, wo_ref, g_ref, b_ref, o_ref, ya_rows_ref, *, alpha):
    x = x_ref[...]
    xb = x.astype(BF16)
    hg = _mm(_from_planes(ya_ref, ya_rows_ref), wglu_ref[...])
    ya = hg[:, :D_S5] * _sigmoid(hg[:, D_S5:])
    merged = None
    for idx, (y, w_ref) in enumerate(((ya, wa_ref), (yb_ref[...], wb_ref), (yc_ref[...], wc_ref),
                                      (yd_ref[...], wd_ref))):
        cols = slice(idx * D_MODEL, (idx + 1) * D_MODEL)
        gate = _sigmoid(_mm(xb, wgate_ref[:, cols]) + bgate_ref[:, cols])
        term = gate * _mm(y, w_ref[...])
        merged = term if merged is None else merged + term
    out = _mm(merged, wo_ref[...])
    o_ref[...] = _ln(alpha * x + out, g_ref[...], b_ref[...])


def _merge(x, ya, yb, yc, yd, w, *, tm, alpha):
    n = x.shape[0]
    chunk = ya.shape[0]
    row = lambda width: pl.BlockSpec((tm, width), lambda i: (i, 0))
    return pl.pallas_call(
        functools.partial(_merge_kernel, alpha=alpha),
        grid=(n // tm,),
        in_specs=[row(D_MODEL), pl.BlockSpec((chunk, tm // chunk, D_S5), lambda i: (0, i, 0)),
                  row(D_SGU), row(D_CONV), row(D_ATT),
                  _layer_spec(w['w_glu']), _layer_spec(w['w_gate']),
                  _const_spec((1, N_BRANCH * D_MODEL)),
                  _layer_spec(w['w_br_a']), _layer_spec(w['w_br_b']), _layer_spec(w['w_br_c']),
                  _layer_spec(w['w_br_d']), _layer_spec(w['w_o']),
                  _const_spec((1, D_MODEL)), _const_spec((1, D_MODEL))],
        out_specs=row(D_MODEL),
        out_shape=jax.ShapeDtypeStruct((n, D_MODEL), F32),
        scratch_shapes=[pltpu.VMEM((D_S5 // LANES, tm, LANES), F32)],
        compiler_params=_cparams(("parallel",)),
    )(x, ya, yb, yc, yd, w['w_glu'].stack, w['w_gate'].stack, w['b_gate'], w['w_br_a'].stack, w['w_br_b'].stack,
      w['w_br_c'].stack, w['w_br_d'].stack, w['w_o'].stack, w['ln1_g'], w['ln1_b'])


def _ffn_kernel(x_ref, wg_ref, wu_ref, wd_ref, g_ref, b_ref, o_ref, *, alpha):
    x = x_ref[...]
    xb = x.astype(BF16)
    hid = _silu(_mm(xb, wg_ref[...])) * _mm(xb, wu_ref[...])
    o_ref[...] = _ln(alpha * x + _mm(hid, wd_ref[...]), g_ref[...], b_ref[...])


def _ffn(x, wg, wu, wd, g, b, *, tm, alpha):
    n = x.shape[0]
    dff = wg.stack.shape[-1]
    row = pl.BlockSpec((tm, D_MODEL), lambda i: (i, 0))
    return pl.pallas_call(
        functools.partial(_ffn_kernel, alpha=alpha),
        grid=(n // tm,),
        in_specs=[row, _layer_spec(wg), _layer_spec(wu), _layer_spec(wd),
                  _const_spec((1, D_MODEL)), _const_spec((1, D_MODEL))],
        out_specs=row,
        out_shape=jax.ShapeDtypeStruct((n, D_MODEL), F32),
        compiler_params=_cparams(("parallel",)),
    )(x, wg.stack, wu.stack, wd.stack, g, b)


def _split_bf16(a):
    hi = a.astype(BF16)
    lo = (a - hi.astype(F32)).astype(BF16)
    return hi, lo


SEL_I1, SEL_I2, SEL_W1, SEL_W2 = 0, 1, 2, 3
MOE_TM = 512


def _moe_route_kernel(x_ref, wr_hi_ref, wr_lo_ref, br_ref, sel_ref):
    x = x_ref[...]
    tm = x.shape[0]
    lane = lax.broadcasted_iota(jnp.int32, (tm, LANES), 1)
    x_hi, x_lo = _split_bf16(x)
    logits = (jnp.dot(x_hi, wr_hi_ref[...], preferred_element_type=F32)
              + jnp.dot(x_hi, wr_lo_ref[...], preferred_element_type=F32)
              + jnp.dot(x_lo, wr_hi_ref[...], preferred_element_type=F32)) + br_ref[...]
    logits = jnp.where(lane < N_EXPERTS, logits, -jnp.inf)
    lane_f = lane.astype(F32)
    v1 = jnp.max(logits, axis=-1, keepdims=True)
    i1 = jnp.min(jnp.where(logits == v1, lane_f, float(LANES)), axis=-1, keepdims=True)
    rest = jnp.where(lane_f == i1, -jnp.inf, logits)
    v2 = jnp.max(rest, axis=-1, keepdims=True)
    i2 = jnp.min(jnp.where(rest == v2, lane_f, float(LANES)), axis=-1, keepdims=True)
    e2 = jnp.exp(v2 - v1)
    w1 = 1.0 / (1.0 + e2)
    w2 = e2 / (1.0 + e2)
    sel_ref[...] = (jnp.where(lane == SEL_I1, i1, 0.0) + jnp.where(lane == SEL_I2, i2, 0.0)
                    + jnp.where(lane == SEL_W1, w1, 0.0) + jnp.where(lane == SEL_W2, w2, 0.0))


def _moe_expert_kernel(item_e, item_tile, item_flag, starts, idx_ref, idx_next_ref, idx_prev_ref, x_hbm,
                       wg_ref, wu_ref, wd_ref, y_hbm, xbuf, obuf, gsem, ssem, *, n_tiles, n_tok):
    i = pl.program_id(0)
    tm = xbuf.shape[1]
    flag = item_flag[i]
    valid = (flag & 1) == 1
    first = (flag & 2) == 2
    last = (flag & 4) == 4
    t = item_tile[i]
    e = item_e[i]
    slot = t % 2
    other = 1 - slot

    def for_rows(body, unrolled):
        if unrolled:
            for r in range(tm):
                body(r)
        else:
            def step(r, carry):
                body(r)
                return carry
            lax.fori_loop(0, tm, step, 0, unroll=8)

    def gather_rows(idx, dst_slot, unrolled=False):
        def body(r):
            a = idx[0, r]
            tok = jnp.where(a >= n_tok, a - n_tok, a)
            pltpu.make_async_copy(x_hbm.at[pl.ds(tok, 1)], xbuf.at[dst_slot, pl.ds(r, 1)], gsem.at[dst_slot]).start()
        for_rows(body, unrolled)

    def scatter_rows(idx, src_slot, unrolled=False):
        def body(r):
            dst = idx[0, r]
            pltpu.make_async_copy(obuf.at[src_slot, pl.ds(r, 1)], y_hbm.at[pl.ds(dst, 1)], ssem.at[src_slot]).start()
        for_rows(body, unrolled)

    def wait_rows(buf, sem, s):
        pltpu.make_async_copy(buf.at[s], buf.at[s], sem.at[s]).wait()

    def compute():
        xb = xbuf[slot].astype(BF16)
        hid = _silu(_mm(xb, wg_ref[...])) * _mm(xb, wu_ref[...])
        y = _mm(hid, wd_ref[...])
        pos = t * tm + lax.broadcasted_iota(jnp.int32, (tm, 1), 0)
        mine = (pos >= starts[e]) & (pos < starts[e + 1])
        obuf[slot] += jnp.where(mine, y, 0.0)

    @pl.when(valid & first)
    def _():
        @pl.when(t == 0)
        def _():
            gather_rows(idx_ref, 0)

        wait_rows(xbuf, gsem, slot)

        @pl.when(t >= 2)
        def _():
            wait_rows(obuf, ssem, slot)

        obuf[slot] = jnp.zeros((tm, D_MODEL), F32)

    has_next = t + 1 < n_tiles
    has_prev = t >= 1
    interior = valid & last & has_next & has_prev

    @pl.when(interior)
    def _():
        gather_rows(idx_next_ref, other, unrolled=True)
        scatter_rows(idx_prev_ref, other, unrolled=True)
        compute()

    @pl.when(valid & jnp.logical_not(interior))
    def _():
        @pl.when(last & has_next)
        def _():
            gather_rows(idx_next_ref, other)

        @pl.when(last & has_prev)
        def _():
            scatter_rows(idx_prev_ref, other)

        compute()

    @pl.when(i == pl.num_programs(0) - 1)
    def _():
        scatter_rows(idx_ref, (n_tiles - 1) % 2)
        for tt in range(max(0, n_tiles - 2), n_tiles):
            wait_rows(obuf, ssem, tt % 2)


def _moe_combine_kernel(x_ref, y1_ref, y2_ref, sel_ref, g_ref, b_ref, o_ref, *, alpha):
    sel = sel_ref[...]
    lane = lax.broadcasted_iota(jnp.int32, sel.shape, 1)
    w1 = jnp.sum(jnp.where(lane == SEL_W1, sel, 0.0), axis=-1, keepdims=True)
    w2 = jnp.sum(jnp.where(lane == SEL_W2, sel, 0.0), axis=-1, keepdims=True)
    y = w1 * y1_ref[...] + w2 * y2_ref[...]
    o_ref[...] = _ln(alpha * x_ref[...] + y, g_ref[...], b_ref[...])


def _moe(x, wr_hi, wr_lo, br, wg, wu, wd, g, b, *, tm, alpha):
    n = x.shape[0]
    dfe = wg.stack.shape[-1]
    n_asg = TOP_K * n
    tmx = min(MOE_TM, n_asg)
    assert n_asg % tmx == 0
    n_tiles = n_asg // tmx
    n_items = n_tiles + N_EXPERTS - 1
    row = pl.BlockSpec((tm, D_MODEL), lambda i: (i, 0))

    sel = pl.pallas_call(
        _moe_route_kernel,
        grid=(n // tm,),
        in_specs=[row, _const_spec((D_MODEL, LANES)), _const_spec((D_MODEL, LANES)), _const_spec((1, LANES))],
        out_specs=pl.BlockSpec((tm, LANES), lambda i: (i, 0)),
        out_shape=jax.ShapeDtypeStruct((n, LANES), F32),
        compiler_params=_cparams(("parallel",)),
    )(x, wr_hi, wr_lo, br)

    experts = sel[:, SEL_I1:SEL_I2 + 1].astype(jnp.int32).T.reshape(n_asg)
    keys = jnp.sort(experts * n_asg + jnp.arange(n_asg, dtype=jnp.int32))
    idx = (keys % n_asg).reshape(n_tiles, 1, tmx)
    bounds = jnp.arange(N_EXPERTS + 1, dtype=jnp.int32) * n_asg
    starts = jnp.sum((keys[None, :] < bounds[:, None]).astype(jnp.int32), axis=1)
    counts = starts[1:] - starts[:-1]
    first_tile = starts[:-1] // tmx
    last_tile = (starts[1:] - 1) // tmx
    per_e = jnp.where(counts > 0, last_tile - first_tile + 1, 0)
    ends = jnp.cumsum(per_e)
    total = ends[-1]
    it = jnp.arange(n_items, dtype=jnp.int32)
    e_raw = jnp.sum((ends[None, :] <= it[:, None]).astype(jnp.int32), axis=1)
    e_last = jnp.max(jnp.where(counts > 0, jnp.arange(N_EXPERTS, dtype=jnp.int32), 0))
    valid = it < total
    item_e = jnp.where(valid, jnp.minimum(e_raw, N_EXPERTS - 1), e_last).astype(jnp.int32)
    item_tile = jnp.where(valid, first_tile[item_e] + it - (ends[item_e] - per_e[item_e]), n_tiles - 1)
    item_tile = item_tile.astype(jnp.int32)
    prev_tile = jnp.concatenate([jnp.full((1,), -1, jnp.int32), item_tile[:-1]])
    next_tile = jnp.concatenate([item_tile[1:], jnp.full((1,), -1, jnp.int32)])
    next_valid = jnp.concatenate([valid[1:], jnp.zeros((1,), bool)])
    is_first = item_tile != prev_tile
    is_last = (item_tile != next_tile) | ~next_valid
    item_flag = (valid.astype(jnp.int32) + 2 * is_first.astype(jnp.int32) + 4 * is_last.astype(jnp.int32))

    smem_idx = lambda f: pl.BlockSpec((None, 1, tmx), f, memory_space=pltpu.SMEM)
    grid_spec = pltpu.PrefetchScalarGridSpec(
        num_scalar_prefetch=4,
        grid=(n_items,),
        in_specs=[smem_idx(lambda i, ie, itl, ifl, st: (itl[i], 0, 0)),
                  smem_idx(lambda i, ie, itl, ifl, st: (jnp.minimum(itl[i] + 1, n_tiles - 1), 0, 0)),
                  smem_idx(lambda i, ie, itl, ifl, st: (jnp.maximum(itl[i] - 1, 0), 0, 0)),
                  pl.BlockSpec(memory_space=pl.ANY),
                  pl.BlockSpec((None, None, D_MODEL, dfe), lambda i, ie, itl, ifl, st: (wg.index, ie[i], 0, 0)),
                  pl.BlockSpec((None, None, D_MODEL, dfe), lambda i, ie, itl, ifl, st: (wu.index, ie[i], 0, 0)),
                  pl.BlockSpec((None, None, dfe, D_MODEL), lambda i, ie, itl, ifl, st: (wd.index, ie[i], 0, 0))],
        out_specs=pl.BlockSpec(memory_space=pl.ANY),
        scratch_shapes=[pltpu.VMEM((2, tmx, D_MODEL), F32), pltpu.VMEM((2, tmx, D_MODEL), F32),
                        pltpu.SemaphoreType.DMA((2,)), pltpu.SemaphoreType.DMA((2,))],
    )
    y = pl.pallas_call(
        functools.partial(_moe_expert_kernel, n_tiles=n_tiles, n_tok=n),
        grid_spec=grid_spec,
        out_shape=jax.ShapeDtypeStruct((n_asg, D_MODEL), F32),
        compiler_params=_cparams(("arbitrary",)),
    )(item_e, item_tile, item_flag, starts, idx, idx, idx, x, wg.stack, wu.stack, wd.stack)

    return pl.pallas_call(
        functools.partial(_moe_combine_kernel, alpha=alpha),
        grid=(n // tm,),
        in_specs=[row, row, pl.BlockSpec((tm, D_MODEL), lambda i: (i + n // tm, 0)),
                  pl.BlockSpec((tm, LANES), lambda i: (i, 0)), _const_spec((1, D_MODEL)), _const_spec((1, D_MODEL))],
        out_specs=row,
        out_shape=jax.ShapeDtypeStruct((n, D_MODEL), F32),
        compiler_params=_cparams(("parallel",)),
    )(x, y, y, sel, g, b)


def _rope_table(pos):
    half = ROT_DIM // 2
    inv = ROPE_THETA ** (-jnp.arange(half, dtype=F32) * (2.0 / ROT_DIM))
    ang = pos.astype(F32)[:, None] * inv[None, :]
    cos, sin = jnp.cos(ang), jnp.sin(ang)
    n = pos.shape[0]
    one = jnp.ones((n, ATT_HD - ROT_DIM), F32)
    zero = jnp.zeros((n, ATT_HD - ROT_DIM), F32)
    z8 = jnp.zeros((n, half), F32)
    c = jnp.concatenate([cos, cos, one], axis=1)
    s_hi = jnp.concatenate([-sin, z8, zero], axis=1)
    s_lo = jnp.concatenate([z8, sin, zero], axis=1)
    rep = LANES // ATT_HD
    return jnp.concatenate([jnp.tile(c, (1, rep)), jnp.tile(s_hi, (1, rep)), jnp.tile(s_lo, (1, rep))], axis=1)


def _sgu_tables(w_s, b_s, tn):
    tril = jnp.tril(jnp.ones((CHUNK, CHUNK), dtype=bool))
    ws = jnp.where(tril, w_s, 0)
    hd = D_SGU // SGU_HEADS
    bias_p = jnp.repeat(b_s.T, hd, axis=1)
    rep = CHUNK // tn
    eye = jnp.eye(rep, dtype=w_s.dtype)
    ws_s = jnp.einsum('ab,gts->gatbs', eye, ws[:, :tn, :tn]).reshape(SGU_HEADS, CHUNK, CHUNK)
    bias_s = jnp.tile(bias_p[:tn], (rep, 1))
    return ws.astype(BF16), bias_p.astype(F32), ws_s.astype(BF16), bias_s.astype(F32)


def kernel(x_prompt, x_sample, cache_k, cache_v, state_ssm_re, state_ssm_im, state_conv, page_table, ln_in_g, ln_in_b, w_in, w_gate, b_gate, s5_lam_re, s5_lam_im, s5_log_dt, s5_b_re, s5_b_im, s5_c_re, s5_c_im, s5_d, s5_w_glu, sgu_ln_g, sgu_ln_b, sgu_w_s, sgu_b_s, conv_w, conv_b, conv_ln_g, conv_ln_b, att_lam_q1, att_lam_k1, att_lam_q2, att_lam_k2, att_subln_g, w_br_a, w_br_b, w_br_c, w_br_d, w_o, ln1_g, ln1_b, ffn_wg, ffn_wu, ffn_wd, moe_router, moe_router_b, moe_wg, moe_wu, moe_wd, ln2_g, ln2_b):
    bp, tp, d = x_prompt.shape
    bs, ts, _ = x_sample.shape
    depth = w_in.shape[0]
    n_pages = page_table.shape[1]
    past = n_pages * PAGE_SIZE
    alpha = (2 * depth) ** 0.25
    n_p, n_s = bp * tp, bs * ts
    tm_p = min(512, tp)
    tq = min(512, tp)
    l_p = 16
    pg = min(8, n_pages)
    assert tp % tm_p == 0 and tp % CHUNK == 0 and tp % tq == 0 and tp % l_p == 0 and tp >= CONV_W - 1
    assert n_s % CHUNK == 0 and CHUNK % ts == 0 and ts % SUBLANES == 0 and n_pages % pg == 0 and ts <= l_p

    row1 = lambda a: a.reshape(1, -1).astype(F32)
    tab_p = _rope_table(jnp.arange(tp, dtype=jnp.int32))
    tab_s = jnp.tile(_rope_table(past + jnp.arange(ts, dtype=jnp.int32)), (bs, 1))
    zero_state = jnp.zeros((bp, S5_GROUPS, S5_STATE), F32)

    xp = _layernorm(x_prompt.reshape(n_p, d), ln_in_g, ln_in_b, tm_p)
    xs = _layernorm(x_sample.reshape(n_s, d), ln_in_g, ln_in_b, n_s)

    bf = lambda a: a.astype(BF16)
    w_in_b, w_glu_b, w_gate_b, w_o_b = bf(w_in), bf(s5_w_glu), bf(w_gate), bf(w_o)
    w_br_b4 = (bf(w_br_a), bf(w_br_b), bf(w_br_c), bf(w_br_d))
    ffn_b = (bf(ffn_wg), bf(ffn_wu), bf(ffn_wd))
    moe_b = (bf(moe_wg), bf(moe_wu), bf(moe_wd))

    outs = [[] for _ in range(11)]
    for l in range(depth):
        lam_init = 0.8 - 0.6 * math.exp(-0.3 * l)
        lam = (jnp.exp(jnp.sum(att_lam_q1[l].astype(F32) * att_lam_k1[l].astype(F32)))
               - jnp.exp(jnp.sum(att_lam_q2[l].astype(F32) * att_lam_k2[l].astype(F32))) + lam_init)
        lamv = jnp.full((1, ATT_VD), lam, F32)
        gsv = row1(att_subln_g[l]) * (1.0 - lam_init)
        w_in_l = _Layer(w_in_b, l)
        ws_p, sb_p, ws_s, sb_s = _sgu_tables(sgu_w_s[l], sgu_b_s[l], ts)
        conv_args = (conv_w[l].astype(F32), row1(conv_b[l]), row1(conv_ln_g[l]), row1(conv_ln_b[l]))
        sgu_ln = (row1(sgu_ln_g[l]), row1(sgu_ln_b[l]))
        s5_par = (s5_lam_re[l], s5_lam_im[l], s5_log_dt[l], s5_b_re[l], s5_b_im[l], s5_c_re[l], s5_c_im[l])
        mw = {'w_glu': _Layer(w_glu_b, l), 'w_gate': _Layer(w_gate_b, l), 'b_gate': row1(b_gate[l]),
              'w_br_a': _Layer(w_br_b4[0], l), 'w_br_b': _Layer(w_br_b4[1], l),
              'w_br_c': _Layer(w_br_b4[2], l), 'w_br_d': _Layer(w_br_b4[3], l), 'w_o': _Layer(w_o_b, l),
              'ln1_g': row1(ln1_g[l]), 'ln1_b': row1(ln1_b[l])}
        ln2 = (row1(ln2_g[l]), row1(ln2_b[l]))
        i = l // 2
        if l % 2 == 0:
            fw = tuple(_Layer(a, i) for a in ffn_b)
            ffn = lambda x, tm: _ffn(x, *fw, *ln2, tm=tm, alpha=alpha)
        else:
            wr = jnp.pad(moe_router[i].astype(F32), ((0, 0), (0, LANES - N_EXPERTS)))
            wr_hi, wr_lo = _split_bf16(wr)
            br = jnp.pad(moe_router_b[i].astype(F32), (0, LANES - N_EXPERTS)).reshape(1, LANES)
            fw = tuple(_Layer(a, i) for a in moe_b)
            ffn = lambda x, tm: _moe(x, wr_hi, wr_lo, br, *fw, *ln2, tm=tm, alpha=alpha)

        ua, yb, yc, glu, qb, k, v, kb, vx = _proj_prompt(xp, w_in_l, tab_p, *sgu_ln, ws_p, sb_p, *conv_args,
                                                         seq=tp, tm=tm_p, chunk=l_p)
        s5_tab, pw_re, pw_im = _s5_mats(*s5_par, l_p)
        ya, sr_p, si_p = _s5_branch(ua, zero_state, zero_state, s5_tab, pw_re[l_p], pw_im[l_p], s5_d[l],
                                    nb=bp, t=tp)
        yd = _attn_prompt(qb, kb, vx, lamv, gsv, nb=bp, t=tp, tq=tq)
        xp = _merge(xp, ya, yb, yc, yd, mw, tm=tm_p, alpha=alpha)
        xp = ffn(xp, tm_p)
        outs[0].append(k.reshape(bp, ATT_HEADS, 2, ATT_HD, tp).transpose(0, 4, 1, 2, 3))
        outs[1].append(v.reshape(bp, tp, ATT_HEADS, ATT_VD))
        outs[2].append(sr_p)
        outs[3].append(si_p)
        outs[4].append(glu.reshape(bp, tp, D_CONV)[:, tp - (CONV_W - 1):])

        pfx = jnp.pad(state_conv[l].astype(F32), ((0, 0), (HALO_OFF, 0), (0, 0)))
        ua, yb, yc, glu, q, k, v, vn = _proj_sample(xs, w_in_l, tab_s, *sgu_ln, ws_s, sb_s, *conv_args, pfx,
                                                    nb=bs, tn=ts)
        ya, sr_s, si_s = _s5_branch(ua, state_ssm_re[l], state_ssm_im[l], s5_tab, pw_re[ts], pw_im[ts], s5_d[l],
                                    nb=bs, t=ts)
        yd = _attn_sample(q, k, v, cache_k, cache_v, page_table, l, lamv, gsv, nb=bs, tn=ts, pg=pg)
        xs = _merge(xs, ya, yb, yc, yd, mw, tm=n_s, alpha=alpha)
        xs = ffn(xs, n_s)
        outs[5].append(k.reshape(bs, ts, ATT_HEADS, 2, ATT_HD))
        outs[6].append(v.reshape(bs, ts, ATT_HEADS, ATT_VD))
        outs[7].append(sr_s)
        outs[8].append(si_s)
        full = jnp.concatenate([state_conv[l].astype(F32), glu.reshape(bs, ts, D_CONV)], axis=1)
        outs[9].append(full[:, ts:])
        outs[10].append(vn.reshape(bs, ts, D_SGU))

    return (xp.reshape(bp, tp, d), xs.reshape(bs, ts, d), *[jnp.stack(o) for o in outs])
```

```python
import functools
import math
from typing import NamedTuple

import jax
import jax.numpy as jnp
from jax import lax
from jax.experimental import pallas as pl
from jax.experimental.pallas import tpu as pltpu

F32 = jnp.float32
BF16 = jnp.bfloat16

D_MODEL = 1024
PAGE_SIZE = 128
S5_GROUP = 16
S5_GROUPS = 16
D_S5 = 256
S5_STATE = 64
D_SGU = 256
SGU_HEADS = 4
CHUNK = 128
D_CONV = 256
CONV_W = 31
ATT_HEADS = 4
ATT_HD = 64
ATT_VD = 128
D_ATT = 512
ROT_DIM = 16
ROPE_THETA = 500000.0
ATT_SCALE = ATT_HD ** -0.5
LOG2E = 1.4426950408889634
NEG_INF = -1e30
D_IN = 2816
N_BRANCH = 4
N_EXPERTS = 8
TOP_K = 2
LN_EPS = 1e-5
RMS_EPS = 1e-5

LANES = 128
SUBLANES = 8
HALO = 32
HALO_OFF = HALO - (CONV_W - 1)
VMEM_LIMIT = 56 * 1024 * 1024


def _cparams(sem):
    return pltpu.CompilerParams(dimension_semantics=sem, vmem_limit_bytes=VMEM_LIMIT)


def _const_spec(shape):
    nd = len(shape)
    return pl.BlockSpec(shape, lambda *_: (0,) * nd)


class _Layer(NamedTuple):
    stack: jax.Array
    index: int


def _layer_spec(w):
    shape = w.stack.shape[1:]
    return pl.BlockSpec((None,) + shape, lambda *_: (w.index,) + (0,) * len(shape))


def _ln(x, g, b):
    mu = jnp.mean(x, axis=-1, keepdims=True)
    xc = x - mu
    var = jnp.mean(xc * xc, axis=-1, keepdims=True)
    return xc * lax.rsqrt(var + LN_EPS) * g + b


def _sigmoid(x):
    return 1.0 / (1.0 + jnp.exp(-x))


def _silu(x):
    return x * _sigmoid(x)


def _gelu(x):
    return jax.nn.gelu(x, approximate=True)


def _mm(a, b):
    return jnp.dot(a.astype(BF16), b.astype(BF16), preferred_element_type=F32)


def _mm_nt(a, b):
    return lax.dot_general(a.astype(BF16), b.astype(BF16), (((1,), (1,)), ((), ())),
                           preferred_element_type=F32)


def _rope(x, tab):
    n_rep = x.shape[1] // LANES
    c = jnp.concatenate([tab[:, 0:LANES]] * n_rep, axis=1)
    s_hi = jnp.concatenate([tab[:, LANES:2 * LANES]] * n_rep, axis=1)
    s_lo = jnp.concatenate([tab[:, 2 * LANES:3 * LANES]] * n_rep, axis=1)
    half = ROT_DIM // 2
    x_up = pltpu.roll(x, x.shape[1] - half, 1)
    x_dn = pltpu.roll(x, half, 1)
    return x * c + x_up * s_hi + x_dn * s_lo


def _ln_kernel(x_ref, g_ref, b_ref, o_ref):
    o_ref[...] = _ln(x_ref[...], g_ref[...], b_ref[...])


def _layernorm(x, g, b, tm):
    n, d = x.shape
    return pl.pallas_call(
        _ln_kernel,
        grid=(n // tm,),
        in_specs=[pl.BlockSpec((tm, d), lambda i: (i, 0)), _const_spec((1, d)), _const_spec((1, d))],
        out_specs=pl.BlockSpec((tm, d), lambda i: (i, 0)),
        out_shape=jax.ShapeDtypeStruct((n, d), F32),
        compiler_params=_cparams(("parallel",)),
    )(x, g.reshape(1, d), b.reshape(1, d))


def _proj_split(x, w_ref, tab, sgu_g, sgu_b, q_scale):
    h = _mm(x, w_ref[...])
    ua = h[:, 0:256]
    gub = _gelu(h[:, 256:512])
    vn = _ln(_gelu(h[:, 512:768]), sgu_g, sgu_b)
    glu = h[:, 768:1024] * _sigmoid(h[:, 1024:1280])
    q = _rope(h[:, 1280:1792], tab) * q_scale
    k = _rope(h[:, 1792:2304], tab)
    v = h[:, 2304:2816]
    return ua, gub, vn, glu, q, k, v


def _sgu(gub, vn, ws_ref, sb_ref):
    tm = vn.shape[0]
    head_of_lane = lax.broadcasted_iota(jnp.int32, (CHUNK, D_SGU), 1) // (D_SGU // SGU_HEADS)
    outs = []
    for c in range(tm // CHUNK):
        vc = vn[c * CHUNK:(c + 1) * CHUNK]
        z = sb_ref[...]
        for g in range(SGU_HEADS):
            z = z + _mm(ws_ref[g], jnp.where(head_of_lane == g, vc, 0.0))
        outs.append(gub[c * CHUNK:(c + 1) * CHUNK] * z)
    return outs[0] if len(outs) == 1 else jnp.concatenate(outs, axis=0)


def _conv_post(acc, cb, g, b):
    return _silu(_ln(acc + cb, g, b))


def _to_planes(rows, tmp_ref, planes_ref):
    n_planes, per_plane = planes_ref.shape[0], planes_ref.shape[1]
    for h in range(tmp_ref.shape[0]):
        tmp_ref[h] = rows[:, h * LANES:(h + 1) * LANES]
    for i in range(n_planes):
        for h in range(tmp_ref.shape[0]):
            planes_ref[i, :, h * LANES:(h + 1) * LANES] = tmp_ref[h, pl.ds(i, per_plane, stride=n_planes), :]


def _from_planes(planes_ref, tmp_ref):
    n_planes, per_plane = planes_ref.shape[0], planes_ref.shape[1]
    for i in range(n_planes):
        for h in range(tmp_ref.shape[0]):
            tmp_ref[h, pl.ds(i, per_plane, stride=n_planes), :] = planes_ref[i, :, h * LANES:(h + 1) * LANES]
    return jnp.concatenate([tmp_ref[h] for h in range(tmp_ref.shape[0])], axis=1)


def _proj_prompt_kernel(x_ref, w_ref, tab_ref, sg_ref, sbn_ref, ws_ref, sb_ref, cw_ref, cb_ref, cg_ref, cbn_ref,
                        up_ref, yb_ref, yc_ref, glu_ref, q_ref, k_ref, v_ref, kb_ref, vx_ref,
                        ext_ref, ua_ref, *, tm, tiles_per_seq, rb):
    i = pl.program_id(0)
    ua, gub, vn, glu, q, k, v = _proj_split(x_ref[...], w_ref, tab_ref[...], sg_ref[...], sbn_ref[...],
                                            ATT_SCALE * LOG2E)
    _to_planes(ua, ua_ref, up_ref)
    glu_ref[...] = glu
    q_ref[...] = q.astype(BF16)
    k_ref[...] = k.T
    for h in range(ATT_HEADS):
        v_ref[pl.ds(h, tm, stride=ATT_HEADS), :] = v[:, h * ATT_VD:(h + 1) * ATT_VD]
    kb_ref[...] = k.astype(BF16)
    vb = v.astype(BF16)
    ones = jnp.ones((tm, ATT_VD), BF16)
    vx_ref[...] = jnp.concatenate(
        [piece for h in range(ATT_HEADS) for piece in (vb[:, h * ATT_VD:(h + 1) * ATT_VD], ones)], axis=1)
    yb_ref[...] = _sgu(gub, vn, ws_ref, sb_ref)

    @pl.when(i % tiles_per_seq == 0)
    def _():
        ext_ref[0:HALO, :] = jnp.zeros((HALO, D_CONV), F32)

    ext_ref[HALO:HALO + tm, :] = glu
    for r in range(tm // rb):
        base = r * rb
        acc = jnp.zeros((rb, D_CONV), F32)
        for b in range(SUBLANES):
            zlen = rb if b == 0 else rb + SUBLANES
            z = jnp.zeros((zlen, D_CONV), F32)
            for a in range(HALO // SUBLANES + 1):
                w = SUBLANES * a + b - HALO_OFF
                if 0 <= w < CONV_W:
                    z = z + ext_ref[pl.ds(base + SUBLANES * a, zlen), :] * cw_ref[w:w + 1, :]
            acc = acc + z[b:b + rb]
        yc_ref[base:base + rb, :] = _conv_post(acc, cb_ref[...], cg_ref[...], cbn_ref[...])
    ext_ref[0:HALO, :] = ext_ref[tm:tm + HALO, :]


def _proj_prompt(x, w_in, tab, sgu_g, sgu_b, ws, sb, cw, cb, cg, cbn, *, seq, tm, chunk):
    n = x.shape[0]
    tiles_per_seq = seq // tm
    row = lambda width: pl.BlockSpec((tm, width), lambda i: (i, 0))
    f = lambda width, dt=F32: jax.ShapeDtypeStruct((n, width), dt)
    planes_spec = pl.BlockSpec((chunk, tm // chunk, D_S5), lambda i: (0, i, 0))
    planes_shape = jax.ShapeDtypeStruct((chunk, n // chunk, D_S5), F32)
    kern = functools.partial(_proj_prompt_kernel, tm=tm, tiles_per_seq=tiles_per_seq, rb=64)
    return pl.pallas_call(
        kern,
        grid=(n // tm,),
        in_specs=[row(D_MODEL), _layer_spec(w_in),
                  pl.BlockSpec((tm, 3 * LANES), lambda i: (i % tiles_per_seq, 0)),
                  _const_spec((1, D_SGU)), _const_spec((1, D_SGU)),
                  _const_spec((SGU_HEADS, CHUNK, CHUNK)), _const_spec((CHUNK, D_SGU)),
                  _const_spec((CONV_W, D_CONV)), _const_spec((1, D_CONV)), _const_spec((1, D_CONV)),
                  _const_spec((1, D_CONV))],
        out_specs=[planes_spec, row(D_SGU), row(D_CONV), row(D_CONV), row(D_ATT),
                   pl.BlockSpec((None, D_ATT, tm), lambda i: (i // tiles_per_seq, 0, i % tiles_per_seq)),
                   pl.BlockSpec((ATT_HEADS * tm, ATT_VD), lambda i: (i, 0)),
                   row(D_ATT), row(2 * D_ATT)],
        out_shape=[planes_shape, f(D_SGU), f(D_CONV), f(D_CONV), f(D_ATT, BF16),
                   jax.ShapeDtypeStruct((n // seq, D_ATT, seq), F32),
                   jax.ShapeDtypeStruct((ATT_HEADS * n, ATT_VD), F32),
                   f(D_ATT, BF16), f(2 * D_ATT, BF16)],
        scratch_shapes=[pltpu.VMEM((tm + HALO, D_CONV), F32), pltpu.VMEM((D_S5 // LANES, tm, LANES), F32)],
        compiler_params=_cparams(("arbitrary",)),
    )(x, w_in.stack, tab, sgu_g, sgu_b, ws, sb, cw, cb, cg, cbn)


def _proj_sample_kernel(x_ref, w_ref, tab_ref, sg_ref, sbn_ref, ws_ref, sb_ref, cw_ref, cb_ref, cg_ref, cbn_ref,
                        pfx_ref,
                        up_ref, yb_ref, yc_ref, glu_ref, q_ref, k_ref, v_ref, vn_ref,
                        ext_ref, ua_ref, *, nb, tn):
    ua, gub, vn, glu, q, k, v = _proj_split(x_ref[...], w_ref, tab_ref[...], sg_ref[...], sbn_ref[...], ATT_SCALE)
    _to_planes(ua, ua_ref, up_ref)
    glu_ref[...] = glu
    q_ref[...] = q
    k_ref[...] = k
    v_ref[...] = v
    vn_ref[...] = vn
    yb_ref[...] = _sgu(gub, vn, ws_ref, sb_ref)

    ext_ref[:, 0:HALO, :] = pfx_ref[...]
    ext_ref[:, HALO:HALO + tn, :] = glu.reshape(nb, tn, D_CONV)
    acc = jnp.zeros((nb, tn, D_CONV), F32)
    for w in range(CONV_W):
        acc = acc + ext_ref[:, pl.ds(HALO_OFF + w, tn), :] * cw_ref[w:w + 1, :]
    yc_ref[...] = _conv_post(acc.reshape(nb * tn, D_CONV), cb_ref[...], cg_ref[...], cbn_ref[...])


def _proj_sample(x, w_in, tab, sgu_g, sgu_b, ws, sb, cw, cb, cg, cbn, pfx, *, nb, tn):
    n = x.shape[0]
    full = lambda width: _const_spec((n, width))
    f = lambda width: jax.ShapeDtypeStruct((n, width), F32)
    kern = functools.partial(_proj_sample_kernel, nb=nb, tn=tn)
    return pl.pallas_call(
        kern,
        grid=(1,),
        in_specs=[full(D_MODEL), _layer_spec(w_in), full(3 * LANES),
                  _const_spec((1, D_SGU)), _const_spec((1, D_SGU)),
                  _const_spec((SGU_HEADS, CHUNK, CHUNK)), _const_spec((CHUNK, D_SGU)),
                  _const_spec((CONV_W, D_CONV)), _const_spec((1, D_CONV)), _const_spec((1, D_CONV)),
                  _const_spec((1, D_CONV)), _const_spec((nb, HALO, D_CONV))],
        out_specs=[_const_spec((tn, nb, D_S5)), full(D_SGU), full(D_CONV), full(D_CONV), full(D_ATT), full(D_ATT),
                   full(D_ATT), full(D_SGU)],
        out_shape=[jax.ShapeDtypeStruct((tn, nb, D_S5), F32), f(D_SGU), f(D_CONV), f(D_CONV), f(D_ATT), f(D_ATT),
                   f(D_ATT), f(D_SGU)],
        scratch_shapes=[pltpu.VMEM((nb, HALO + tn, D_CONV), F32), pltpu.VMEM((D_S5 // LANES, n, LANES), F32)],
        compiler_params=_cparams(("arbitrary",)),
    )(x, w_in.stack, tab, sgu_g, sgu_b, ws, sb, cw, cb, cg, cbn, pfx)


def _s5_mats(lam_re, lam_im, log_dt, b_re, b_im, c_re, c_im, L):
    G, P, C = S5_GROUPS, S5_STATE, S5_GROUP
    lr, li = lam_re.astype(F32), lam_im.astype(F32)
    dt = jnp.exp(log_dt.astype(F32))[:, None]
    steps = jnp.arange(L + 1, dtype=F32)[:, None, None]
    mag = jnp.exp((lr * dt)[None] * steps)
    pr = mag * jnp.cos((li * dt)[None] * steps)
    pi = mag * jnp.sin((li * dt)[None] * steps)
    nr, ni = pr[1] - 1.0, pi[1]
    den = lr * lr + li * li
    fr = (nr * lr + ni * li) / den
    fi = (ni * lr - nr * li) / den
    bre, bim = b_re.astype(F32), b_im.astype(F32)
    br = fr[..., None] * bre - fi[..., None] * bim
    bi = fr[..., None] * bim + fi[..., None] * bre
    cr, ci = c_re.astype(F32), c_im.astype(F32)
    kr = (jnp.einsum('gcp,tgp,gpd->tgdc', cr, pr[:L], br) - jnp.einsum('gcp,tgp,gpd->tgdc', cr, pi[:L], bi)
          - jnp.einsum('gcp,tgp,gpd->tgdc', ci, pr[:L], bi) - jnp.einsum('gcp,tgp,gpd->tgdc', ci, pi[:L], br))

    def block_diag(compact):
        _, _, r, c = compact.shape
        repeat = jnp.tile(jnp.eye(c, dtype=F32), (1, G))
        tiled = jnp.einsum('lrc,cq->lrq', compact.reshape(L, G * r, c), repeat, precision=lax.Precision.HIGHEST)
        row_g = lax.broadcasted_iota(jnp.int32, (G * r, G * c), 0) // r
        col_g = lax.broadcasted_iota(jnp.int32, (G * r, G * c), 1) // c
        return jnp.where(row_g == col_g, tiled, 0.0).astype(BF16)

    kbd = block_diag(kr)
    rev = L - 1 - jnp.arange(L)
    brt, bit = br.transpose(0, 2, 1), bi.transpose(0, 2, 1)
    m1re = block_diag(pr[rev][:, :, None, :] * brt[None] - pi[rev][:, :, None, :] * bit[None])
    m1im = block_diag(pr[rev][:, :, None, :] * bit[None] + pi[rev][:, :, None, :] * brt[None])
    crt, cit = cr.transpose(0, 2, 1), ci.transpose(0, 2, 1)
    nr, ni = pr[1:L + 1][..., None], pi[1:L + 1][..., None]
    m2re = block_diag(crt[None] * nr - cit[None] * ni)
    m2im = block_diag(-(crt[None] * ni + cit[None] * nr))
    return (kbd, m1re, m1im, m2re, m2im), pr.reshape(L + 1, 1, G * P), pi.reshape(L + 1, 1, G * P)


def _s5_local_kernel(up_ref, m1re_ref, m1im_ref, slre_ref, slim_ref):
    @pl.when(pl.program_id(0) == 0)
    def _():
        slre_ref[...] = jnp.zeros_like(slre_ref)
        slim_ref[...] = jnp.zeros_like(slim_ref)

    u = up_ref[...].astype(BF16)
    slre_ref[...] += jnp.dot(u, m1re_ref[...], preferred_element_type=F32)
    slim_ref[...] += jnp.dot(u, m1im_ref[...], preferred_element_type=F32)


def _s5_scan_kernel(slre_ref, slim_ref, s0re_ref, s0im_ref, are_ref, aim_ref,
                    stre_ref, stim_ref, fre_ref, fim_ref, *, nb, cpb):
    ar = are_ref[...]
    ai = aim_ref[...]

    def body(c, carry):
        out = []
        for b in range(nb):
            sr, si = carry[b]
            row = pl.ds(b * cpb + c, 1)
            stre_ref[row, :] = sr
            stim_ref[row, :] = si
            out.append((ar * sr - ai * si + slre_ref[row, :], ar * si + ai * sr + slim_ref[row, :]))
        return tuple(out)

    init = tuple((s0re_ref[b:b + 1, :], s0im_ref[b:b + 1, :]) for b in range(nb))
    fin = lax.fori_loop(0, cpb, body, init)
    for b in range(nb):
        fre_ref[b:b + 1, :] = fin[b][0]
        fim_ref[b:b + 1, :] = fin[b][1]


def _s5_out_kernel(up_ref, kbd_ref, stre_ref, stim_ref, m2re_ref, m2im_ref, d_ref, o_ref):
    i = pl.program_id(0)
    o_ref[...] = (_mm(stre_ref[...], m2re_ref[...]) + _mm(stim_ref[...], m2im_ref[...])
                  + d_ref[...] * up_ref[i])

    def body(tau, carry):
        o_ref[...] += _mm(up_ref[i - tau], kbd_ref[tau])
        return carry

    lax.fori_loop(0, i + 1, body, 0)
    o_ref[...] = _gelu(o_ref[...])


def _s5_branch(up, s0_re, s0_im, tables, a_re, a_im, d_skip, *, nb, t):
    kbd, m1re, m1im, m2re, m2im = tables
    L, ch, dc = up.shape
    l_tab = kbd.shape[0]
    m1_off = l_tab - L
    G, P = S5_GROUPS, S5_STATE
    gp = G * P
    cpb = t // L
    acc_spec = _const_spec((ch, gp))
    acc_shape = jax.ShapeDtypeStruct((ch, gp), F32)
    slre, slim = pl.pallas_call(
        _s5_local_kernel,
        grid=(L,),
        in_specs=[pl.BlockSpec((None, ch, dc), lambda j: (j, 0, 0)),
                  pl.BlockSpec((None, dc, gp), lambda j: (j + m1_off, 0, 0)),
                  pl.BlockSpec((None, dc, gp), lambda j: (j + m1_off, 0, 0))],
        out_specs=[acc_spec, acc_spec],
        out_shape=[acc_shape, acc_shape],
        compiler_params=_cparams(("arbitrary",)),
    )(up, m1re, m1im)
    fin_shape = jax.ShapeDtypeStruct((nb, gp), F32)
    stre, stim, fre, fim = pl.pallas_call(
        functools.partial(_s5_scan_kernel, nb=nb, cpb=cpb),
        out_shape=[acc_shape, acc_shape, fin_shape, fin_shape],
        compiler_params=pltpu.CompilerParams(vmem_limit_bytes=VMEM_LIMIT),
    )(slre, slim, s0_re.reshape(nb, gp).astype(F32), s0_im.reshape(nb, gp).astype(F32), a_re, a_im)
    yp = pl.pallas_call(
        _s5_out_kernel,
        grid=(L,),
        in_specs=[_const_spec((L, ch, dc)), _const_spec((l_tab, dc, dc)), acc_spec, acc_spec,
                  pl.BlockSpec((None, gp, dc), lambda i: (i, 0, 0)),
                  pl.BlockSpec((None, gp, dc), lambda i: (i, 0, 0)),
                  _const_spec((1, dc))],
        out_specs=pl.BlockSpec((None, ch, dc), lambda i: (i, 0, 0)),
        out_shape=jax.ShapeDtypeStruct((L, ch, dc), F32),
        compiler_params=_cparams(("parallel",)),
    )(up, kbd, stre, stim, m2re, m2im, d_skip.astype(F32).reshape(1, dc))
    return yp, fre.reshape(nb, G, P), fim.reshape(nb, G, P)


def _diff_finish(o1, o2, lam, gs):
    o = o1 - lam * o2
    return o * lax.rsqrt(jnp.mean(o * o, axis=-1, keepdims=True) + RMS_EPS) * gs


def _attn_prompt_kernel(q_ref, k_ref, v_ref, lam_ref, gs_ref, o_ref, q2_ref, m_ref, acc_ref, *, tq, rb, unroll):
    i = pl.program_id(2)
    r2 = 2 * tq
    q = q_ref[...]
    lane = lax.broadcasted_iota(jnp.int32, (tq, ATT_VD), 1)
    zero = jnp.zeros_like(q)
    q2_ref[0:tq, :] = jnp.where(lane < ATT_HD, q, zero)
    q2_ref[tq:r2, :] = jnp.where(lane >= ATT_HD, q, zero)
    m_ref[...] = jnp.full((r2, LANES), NEG_INF, F32)
    acc_ref[...] = jnp.zeros((r2, 2 * ATT_VD), F32)

    def step(j, diagonal):
        off = pl.multiple_of(j * tq, tq)
        kt = k_ref[pl.ds(off, tq), :]
        vt = v_ref[pl.ds(off, tq), :]
        for r in range(r2 // rb):
            rows = slice(r * rb, (r + 1) * rb)
            s = _mm_nt(q2_ref[rows, :], kt)
            if diagonal:
                row = (lax.broadcasted_iota(jnp.int32, (rb, tq), 0) + r * rb) % tq
                col = lax.broadcasted_iota(jnp.int32, (rb, tq), 1)
                s = jnp.where(col <= row, s, NEG_INF)
            m_prev = m_ref[rows, :]
            m_new = jnp.maximum(m_prev, jnp.max(s, axis=-1, keepdims=True))
            alpha = jnp.exp2(m_prev - m_new)
            p = jnp.exp2(s - jnp.concatenate([m_new] * (tq // LANES), axis=1))
            acc_ref[rows, :] = jnp.concatenate([alpha, alpha], axis=1) * acc_ref[rows, :] + _mm(p, vt)
            m_ref[rows, :] = m_new

    def group(g, carry):
        for u in range(unroll):
            step(g * unroll + u, False)
        return carry

    def single(j, carry):
        step(j, False)
        return carry

    def pair(g, carry):
        step(g, False)
        step(g + 1, False)
        return carry

    n_grp = i // unroll
    lax.fori_loop(0, n_grp, group, 0)
    rest = n_grp * unroll
    n_pair = (i - rest) // 2
    lax.fori_loop(0, n_pair, lambda g, c: pair(rest + 2 * g, c), 0)
    lax.fori_loop(rest + 2 * n_pair, i, single, 0)
    step(i, True)
    acc = acc_ref[...]
    o = acc[:, 0:ATT_VD] / acc[:, ATT_VD:]
    o_ref[...] = _diff_finish(o[0:tq], o[tq:r2], lam_ref[...], gs_ref[...])


def _attn_prompt(qb, kb, vx, lamv, gsv, *, nb, t, tq):
    n = nb * t
    nq = t // tq
    return pl.pallas_call(
        functools.partial(_attn_prompt_kernel, tq=tq, rb=min(128, tq), unroll=4),
        grid=(nb, ATT_HEADS, nq),
        in_specs=[pl.BlockSpec((tq, ATT_VD), lambda b, h, i: (b * nq + i, h)),
                  pl.BlockSpec((t, ATT_VD), lambda b, h, i: (b, h)),
                  pl.BlockSpec((t, 2 * ATT_VD), lambda b, h, i: (b, h)),
                  _const_spec((1, ATT_VD)), _const_spec((1, ATT_VD))],
        out_specs=pl.BlockSpec((tq, ATT_VD), lambda b, h, i: (b * nq + i, h)),
        out_shape=jax.ShapeDtypeStruct((n, D_ATT), F32),
        scratch_shapes=[pltpu.VMEM((2 * tq, ATT_VD), BF16), pltpu.VMEM((2 * tq, LANES), F32),
                        pltpu.VMEM((2 * tq, 2 * ATT_VD), F32)],
        compiler_params=_cparams(("parallel", "parallel", "arbitrary")),
    )(qb, kb, vx, lamv, gsv)


def _attn_sample_kernel(pt_ref, q_ref, kn_ref, vn_ref, lam_ref, gs_ref, *rest, tn, pg):
    k_refs = rest[0:pg]
    v_refs = rest[pg:2 * pg]
    o_ref = rest[2 * pg]
    qbd_ref, m_ref, l_ref, acc_ref = rest[2 * pg + 1:]
    p_idx = pl.program_id(1)
    rows_h = 2 * tn
    n_rows = ATT_HEADS * rows_h

    def update(s, v_pages):
        n_rep = s.shape[1] // LANES
        m_prev = m_ref[...]
        m_new = jnp.maximum(m_prev, jnp.max(s, axis=-1, keepdims=True))
        p = jnp.exp(s - jnp.concatenate([m_new] * n_rep, axis=1))
        alpha = jnp.exp(m_prev - m_new)
        l_ref[...] = alpha * l_ref[...] + jnp.sum(p, axis=-1, keepdims=True)
        for h in range(ATT_HEADS):
            rows = slice(h * rows_h, (h + 1) * rows_h)
            v_h = [r[pl.ds(h, PAGE_SIZE, stride=ATT_HEADS), :].astype(BF16) for r in v_pages]
            v_h = v_h[0] if len(v_h) == 1 else jnp.concatenate(v_h, axis=0)
            acc_ref[rows, :] = alpha[rows] * acc_ref[rows, :] + _mm(p[rows], v_h)
        m_ref[...] = m_new

    @pl.when(p_idx == 0)
    def _():
        q = q_ref[...]
        lane_grp = lax.broadcasted_iota(jnp.int32, (tn, D_ATT), 1) // ATT_HD
        pieces = [jnp.where(lane_grp == g, q, 0.0) for g in range(2 * ATT_HEADS)]
        qbd_ref[...] = jnp.concatenate(pieces, axis=0).astype(BF16)
        m_ref[...] = jnp.full((n_rows, LANES), NEG_INF, F32)
        l_ref[...] = jnp.zeros((n_rows, LANES), F32)
        acc_ref[...] = jnp.zeros((n_rows, ATT_VD), F32)
        s = _mm(qbd_ref[...], kn_ref[...])
        t_of_row = lax.broadcasted_iota(jnp.int32, (n_rows, PAGE_SIZE), 0) % tn
        col = lax.broadcasted_iota(jnp.int32, (n_rows, PAGE_SIZE), 1)
        update(jnp.where(col <= t_of_row, s, NEG_INF), [vn_ref])

    kcat = jnp.concatenate([r[...].astype(BF16) for r in k_refs], axis=1)
    update(_mm(qbd_ref[...], kcat), list(v_refs))

    @pl.when(p_idx == pl.num_programs(1) - 1)
    def _():
        o = acc_ref[...] / l_ref[...]
        outs = []
        for h in range(ATT_HEADS):
            o1 = o[h * rows_h:h * rows_h + tn]
            o2 = o[h * rows_h + tn:(h + 1) * rows_h]
            outs.append(_diff_finish(o1, o2, lam_ref[...], gs_ref[...]))
        o_ref[...] = jnp.concatenate(outs, axis=1)


def _attn_sample(q, k_new, v_new, cache_k, cache_v, page_table, layer, lamv, gsv, *, nb, tn, pg):
    n_pages = page_table.shape[1]
    depth, n_pool = cache_k.shape[0], cache_k.shape[1]
    rows_v = PAGE_SIZE * ATT_HEADS
    ck = jnp.transpose(cache_k, (0, 1, 3, 4, 5, 2)).reshape(depth, n_pool, D_ATT, PAGE_SIZE)
    cv = cache_v.reshape(depth, n_pool, rows_v, ATT_VD)
    kn = jnp.pad(k_new.reshape(nb, tn, D_ATT).transpose(0, 2, 1), ((0, 0), (0, 0), (0, PAGE_SIZE - tn)))
    vn = jnp.pad(v_new.reshape(nb, tn * ATT_HEADS, ATT_VD), ((0, 0), (0, rows_v - tn * ATT_HEADS), (0, 0)))
    n_rows = 2 * ATT_HEADS * tn

    def page_spec(r, rows, cols):
        return pl.BlockSpec((None, None, rows, cols), lambda b, p, pt: (layer, pt[b, p * pg + r], 0, 0))

    grid_spec = pltpu.PrefetchScalarGridSpec(
        num_scalar_prefetch=1,
        grid=(nb, n_pages // pg),
        in_specs=[pl.BlockSpec((tn, D_ATT), lambda b, p, pt: (b, 0)),
                  pl.BlockSpec((None, D_ATT, PAGE_SIZE), lambda b, p, pt: (b, 0, 0)),
                  pl.BlockSpec((None, rows_v, ATT_VD), lambda b, p, pt: (b, 0, 0)),
                  pl.BlockSpec((1, ATT_VD), lambda b, p, pt: (0, 0)),
                  pl.BlockSpec((1, ATT_VD), lambda b, p, pt: (0, 0))]
                 + [page_spec(r, D_ATT, PAGE_SIZE) for r in range(pg)]
                 + [page_spec(r, rows_v, ATT_VD) for r in range(pg)],
        out_specs=pl.BlockSpec((tn, D_ATT), lambda b, p, pt: (b, 0)),
        scratch_shapes=[pltpu.VMEM((n_rows, D_ATT), BF16), pltpu.VMEM((n_rows, LANES), F32),
                        pltpu.VMEM((n_rows, LANES), F32), pltpu.VMEM((n_rows, ATT_VD), F32)],
    )
    return pl.pallas_call(
        functools.partial(_attn_sample_kernel, tn=tn, pg=pg),
        grid_spec=grid_spec,
        out_shape=jax.ShapeDtypeStruct((nb * tn, D_ATT), F32),
        compiler_params=_cparams(("parallel", "arbitrary")),
    )(page_table, q, kn, vn, lamv, gsv, *([ck] * pg), *([cv] * pg))


def _merge_kernel(x_ref, ya_ref, yb_ref, yc_ref, yd_ref, wglu_ref, wgate_ref, bgate_ref,
                  wa_ref, wb_ref, wc_ref, wd_ref, wo_ref, g_ref, b_ref, o_ref, ya_rows_ref, *, alpha):
    x = x_ref[...]
    xb = x.astype(BF16)
    hg = _mm(_from_planes(ya_ref, ya_rows_ref), wglu_ref[...])
    ya = hg[:, :D_S5] * _sigmoid(hg[:, D_S5:])
    merged = None
    for idx, (y, w_ref) in enumerate(((ya, wa_ref), (yb_ref[...], wb_ref), (yc_ref[...], wc_ref),
                                      (yd_ref[...], wd_ref))):
        cols = slice(idx * D_MODEL, (idx + 1) * D_MODEL)
        gate = _sigmoid(_mm(xb, wgate_ref[:, cols]) + bgate_ref[:, cols])
        term = gate * _mm(y, w_ref[...])
        merged = term if merged is None else merged + term
    out = _mm(merged, wo_ref[...])
    o_ref[...] = _ln(alpha * x + out, g_ref[...], b_ref[...])


def _merge(x, ya, yb, yc, yd, w, *, tm, alpha):
    n = x.shape[0]
    chunk = ya.shape[0]
    row = lambda width: pl.BlockSpec((tm, width), lambda i: (i, 0))
    return pl.pallas_call(
        functools.partial(_merge_kernel, alpha=alpha),
        grid=(n // tm,),
        in_specs=[row(D_MODEL), pl.BlockSpec((chunk, tm // chunk, D_S5), lambda i: (0, i, 0)),
                  row(D_SGU), row(D_CONV), row(D_ATT),
                  _layer_spec(w['w_glu']), _layer_spec(w['w_gate']),
                  _const_spec((1, N_BRANCH * D_MODEL)),
                  _layer_spec(w['w_br_a']), _layer_spec(w['w_br_b']), _layer_spec(w['w_br_c']),
                  _layer_spec(w['w_br_d']), _layer_spec(w['w_o']),
                  _const_spec((1, D_MODEL)), _const_spec((1, D_MODEL))],
        out_specs=row(D_MODEL),
        out_shape=jax.ShapeDtypeStruct((n, D_MODEL), F32),
        scratch_shapes=[pltpu.VMEM((D_S5 // LANES, tm, LANES), F32)],
        compiler_params=_cparams(("parallel",)),
    )(x, ya, yb, yc, yd, w['w_glu'].stack, w['w_gate'].stack, w['b_gate'], w['w_br_a'].stack, w['w_br_b'].stack,
      w['w_br_c'].stack, w['w_br_d'].stack, w['w_o'].stack, w['ln1_g'], w['ln1_b'])


def _ffn_kernel(x_ref, wg_ref, wu_ref, wd_ref, g_ref, b_ref, o_ref, *, alpha):
    x = x_ref[...]
    xb = x.astype(BF16)
    hid = _silu(_mm(xb, wg_ref[...])) * _mm(xb, wu_ref[...])
    o_ref[...] = _ln(alpha * x + _mm(hid, wd_ref[...]), g_ref[...], b_ref[...])


def _ffn(x, wg, wu, wd, g, b, *, tm, alpha):
    n = x.shape[0]
    dff = wg.stack.shape[-1]
    row = pl.BlockSpec((tm, D_MODEL), lambda i: (i, 0))
    return pl.pallas_call(
        functools.partial(_ffn_kernel, alpha=alpha),
        grid=(n // tm,),
        in_specs=[row, _layer_spec(wg), _layer_spec(wu), _layer_spec(wd),
                  _const_spec((1, D_MODEL)), _const_spec((1, D_MODEL))],
        out_specs=row,
        out_shape=jax.ShapeDtypeStruct((n, D_MODEL), F32),
        compiler_params=_cparams(("parallel",)),
    )(x, wg.stack, wu.stack, wd.stack, g, b)


def _split_bf16(a):
    hi = a.astype(BF16)
    lo = (a - hi.astype(F32)).astype(BF16)
    return hi, lo


SEL_I1, SEL_I2, SEL_W1, SEL_W2 = 0, 1, 2, 3
MOE_TM = 512


def _moe_route_kernel(x_ref, wr_hi_ref, wr_lo_ref, br_ref, sel_ref):
    x = x_ref[...]
    tm = x.shape[0]
    lane = lax.broadcasted_iota(jnp.int32, (tm, LANES), 1)
    x_hi, x_lo = _split_bf16(x)
    logits = (jnp.dot(x_hi, wr_hi_ref[...], preferred_element_type=F32)
              + jnp.dot(x_hi, wr_lo_ref[...], preferred_element_type=F32)
              + jnp.dot(x_lo, wr_hi_ref[...], preferred_element_type=F32)) + br_ref[...]
    logits = jnp.where(lane < N_EXPERTS, logits, -jnp.inf)
    lane_f = lane.astype(F32)
    v1 = jnp.max(logits, axis=-1, keepdims=True)
    i1 = jnp.min(jnp.where(logits == v1, lane_f, float(LANES)), axis=-1, keepdims=True)
    rest = jnp.where(lane_f == i1, -jnp.inf, logits)
    v2 = jnp.max(rest, axis=-1, keepdims=True)
    i2 = jnp.min(jnp.where(rest == v2, lane_f, float(LANES)), axis=-1, keepdims=True)
    e2 = jnp.exp(v2 - v1)
    w1 = 1.0 / (1.0 + e2)
    w2 = e2 / (1.0 + e2)
    sel_ref[...] = (jnp.where(lane == SEL_I1, i1, 0.0) + jnp.where(lane == SEL_I2, i2, 0.0)
                    + jnp.where(lane == SEL_W1, w1, 0.0) + jnp.where(lane == SEL_W2, w2, 0.0))


def _moe_expert_kernel(item_e, item_tile, item_flag, starts, idx_ref, idx_next_ref, idx_prev_ref, x_hbm,
                       wg_ref, wu_ref, wd_ref, y_hbm, xbuf, obuf, gsem, ssem, *, n_tiles, n_tok):
    i = pl.program_id(0)
    tm = xbuf.shape[1]
    flag = item_flag[i]
    valid = (flag & 1) == 1
    first = (flag & 2) == 2
    last = (flag & 4) == 4
    t = item_tile[i]
    e = item_e[i]
    slot = t % 2
    other = 1 - slot

    def for_rows(body, unrolled):
        if unrolled:
            for r in range(tm):
                body(r)
        else:
            def step(r, carry):
                body(r)
                return carry
            lax.fori_loop(0, tm, step, 0, unroll=8)

    def gather_rows(idx, dst_slot, unrolled=False):
        def body(r):
            a = idx[0, r]
            tok = jnp.where(a >= n_tok, a - n_tok, a)
            pltpu.make_async_copy(x_hbm.at[pl.ds(tok, 1)], xbuf.at[dst_slot, pl.ds(r, 1)], gsem.at[dst_slot]).start()
        for_rows(body, unrolled)

    def scatter_rows(idx, src_slot, unrolled=False):
        def body(r):
            dst = idx[0, r]
            pltpu.make_async_copy(obuf.at[src_slot, pl.ds(r, 1)], y_hbm.at[pl.ds(dst, 1)], ssem.at[src_slot]).start()
        for_rows(body, unrolled)

    def wait_rows(buf, sem, s):
        pltpu.make_async_copy(buf.at[s], buf.at[s], sem.at[s]).wait()

    def compute():
        xb = xbuf[slot].astype(BF16)
        hid = _silu(_mm(xb, wg_ref[...])) * _mm(xb, wu_ref[...])
        y = _mm(hid, wd_ref[...])
        pos = t * tm + lax.broadcasted_iota(jnp.int32, (tm, 1), 0)
        mine = (pos >= starts[e]) & (pos < starts[e + 1])
        obuf[slot] += jnp.where(mine, y, 0.0)

    @pl.when(valid & first)
    def _():
        @pl.when(t == 0)
        def _():
            gather_rows(idx_ref, 0)

        wait_rows(xbuf, gsem, slot)

        @pl.when(t >= 2)
        def _():
            wait_rows(obuf, ssem, slot)

        obuf[slot] = jnp.zeros((tm, D_MODEL), F32)

    has_next = t + 1 < n_tiles
    has_prev = t >= 1
    interior = valid & last & has_next & has_prev

    @pl.when(interior)
    def _():
        gather_rows(idx_next_ref, other, unrolled=True)
        scatter_rows(idx_prev_ref, other, unrolled=True)
        compute()

    @pl.when(valid & jnp.logical_not(interior))
    def _():
        @pl.when(last & has_next)
        def _():
            gather_rows(idx_next_ref, other)

        @pl.when(last & has_prev)
        def _():
            scatter_rows(idx_prev_ref, other)

        compute()

    @pl.when(i == pl.num_programs(0) - 1)
    def _():
        scatter_rows(idx_ref, (n_tiles - 1) % 2)
        for tt in range(max(0, n_tiles - 2), n_tiles):
            wait_rows(obuf, ssem, tt % 2)


def _moe_combine_kernel(x_ref, y1_ref, y2_ref, sel_ref, g_ref, b_ref, o_ref, *, alpha):
    sel = sel_ref[...]
    lane = lax.broadcasted_iota(jnp.int32, sel.shape, 1)
    w1 = jnp.sum(jnp.where(lane == SEL_W1, sel, 0.0), axis=-1, keepdims=True)
    w2 = jnp.sum(jnp.where(lane == SEL_W2, sel, 0.0), axis=-1, keepdims=True)
    y = w1 * y1_ref[...] + w2 * y2_ref[...]
    o_ref[...] = _ln(alpha * x_ref[...] + y, g_ref[...], b_ref[...])


def _moe(x, wr_hi, wr_lo, br, wg, wu, wd, g, b, *, tm, alpha):
    n = x.shape[0]
    dfe = wg.stack.shape[-1]
    n_asg = TOP_K * n
    tmx = min(MOE_TM, n_asg)
    assert n_asg % tmx == 0
    n_tiles = n_asg // tmx
    n_items = n_tiles + N_EXPERTS - 1
    row = pl.BlockSpec((tm, D_MODEL), lambda i: (i, 0))

    sel = pl.pallas_call(
        _moe_route_kernel,
        grid=(n // tm,),
        in_specs=[row, _const_spec((D_MODEL, LANES)), _const_spec((D_MODEL, LANES)), _const_spec((1, LANES))],
        out_specs=pl.BlockSpec((tm, LANES), lambda i: (i, 0)),
        out_shape=jax.ShapeDtypeStruct((n, LANES), F32),
        compiler_params=_cparams(("parallel",)),
    )(x, wr_hi, wr_lo, br)

    experts = sel[:, SEL_I1:SEL_I2 + 1].astype(jnp.int32).T.reshape(n_asg)
    keys = jnp.sort(experts * n_asg + jnp.arange(n_asg, dtype=jnp.int32))
    idx = (keys % n_asg).reshape(n_tiles, 1, tmx)
    bounds = jnp.arange(N_EXPERTS + 1, dtype=jnp.int32) * n_asg
    starts = jnp.sum((keys[None, :] < bounds[:, None]).astype(jnp.int32), axis=1)
    counts = starts[1:] - starts[:-1]
    first_tile = starts[:-1] // tmx
    last_tile = (starts[1:] - 1) // tmx
    per_e = jnp.where(counts > 0, last_tile - first_tile + 1, 0)
    ends = jnp.cumsum(per_e)
    total = ends[-1]
    it = jnp.arange(n_items, dtype=jnp.int32)
    e_raw = jnp.sum((ends[None, :] <= it[:, None]).astype(jnp.int32), axis=1)
    e_last = jnp.max(jnp.where(counts > 0, jnp.arange(N_EXPERTS, dtype=jnp.int32), 0))
    valid = it < total
    item_e = jnp.where(valid, jnp.minimum(e_raw, N_EXPERTS - 1), e_last).astype(jnp.int32)
    item_tile = jnp.where(valid, first_tile[item_e] + it - (ends[item_e] - per_e[item_e]), n_tiles - 1)
    item_tile = item_tile.astype(jnp.int32)
    prev_tile = jnp.concatenate([jnp.full((1,), -1, jnp.int32), item_tile[:-1]])
    next_tile = jnp.concatenate([item_tile[1:], jnp.full((1,), -1, jnp.int32)])
    next_valid = jnp.concatenate([valid[1:], jnp.zeros((1,), bool)])
    is_first = item_tile != prev_tile
    is_last = (item_tile != next_tile) | ~next_valid
    item_flag = (valid.astype(jnp.int32) + 2 * is_first.astype(jnp.int32) + 4 * is_last.astype(jnp.int32))

    smem_idx = lambda f: pl.BlockSpec((None, 1, tmx), f, memory_space=pltpu.SMEM)
    grid_spec = pltpu.PrefetchScalarGridSpec(
        num_scalar_prefetch=4,
        grid=(n_items,),
        in_specs=[smem_idx(lambda i, ie, itl, ifl, st: (itl[i], 0, 0)),
                  smem_idx(lambda i, ie, itl, ifl, st: (jnp.minimum(itl[i] + 1, n_tiles - 1), 0, 0)),
                  smem_idx(lambda i, ie, itl, ifl, st: (jnp.maximum(itl[i] - 1, 0), 0, 0)),
                  pl.BlockSpec(memory_space=pl.ANY),
                  pl.BlockSpec((None, None, D_MODEL, dfe), lambda i, ie, itl, ifl, st: (wg.index, ie[i], 0, 0)),
                  pl.BlockSpec((None, None, D_MODEL, dfe), lambda i, ie, itl, ifl, st: (wu.index, ie[i], 0, 0)),
                  pl.BlockSpec((None, None, dfe, D_MODEL), lambda i, ie, itl, ifl, st: (wd.index, ie[i], 0, 0))],
        out_specs=pl.BlockSpec(memory_space=pl.ANY),
        scratch_shapes=[pltpu.VMEM((2, tmx, D_MODEL), F32), pltpu.VMEM((2, tmx, D_MODEL), F32),
                        pltpu.SemaphoreType.DMA((2,)), pltpu.SemaphoreType.DMA((2,))],
    )
    y = pl.pallas_call(
        functools.partial(_moe_expert_kernel, n_tiles=n_tiles, n_tok=n),
        grid_spec=grid_spec,
        out_shape=jax.ShapeDtypeStruct((n_asg, D_MODEL), F32),
        compiler_params=_cparams(("arbitrary",)),
    )(item_e, item_tile, item_flag, starts, idx, idx, idx, x, wg.stack, wu.stack, wd.stack)

    return pl.pallas_call(
        functools.partial(_moe_combine_kernel, alpha=alpha),
        grid=(n // tm,),
        in_specs=[row, row, pl.BlockSpec((tm, D_MODEL), lambda i: (i + n // tm, 0)),
                  pl.BlockSpec((tm, LANES), lambda i: (i, 0)), _const_spec((1, D_MODEL)), _const_spec((1, D_MODEL))],
        out_specs=row,
        out_shape=jax.ShapeDtypeStruct((n, D_MODEL), F32),
        compiler_params=_cparams(("parallel",)),
    )(x, y, y, sel, g, b)


def _rope_table(pos):
    half = ROT_DIM // 2
    inv = ROPE_THETA ** (-jnp.arange(half, dtype=F32) * (2.0 / ROT_DIM))
    ang = pos.astype(F32)[:, None] * inv[None, :]
    cos, sin = jnp.cos(ang), jnp.sin(ang)
    n = pos.shape[0]
    one = jnp.ones((n, ATT_HD - ROT_DIM), F32)
    zero = jnp.zeros((n, ATT_HD - ROT_DIM), F32)
    z8 = jnp.zeros((n, half), F32)
    c = jnp.concatenate([cos, cos, one], axis=1)
    s_hi = jnp.concatenate([-sin, z8, zero], axis=1)
    s_lo = jnp.concatenate([z8, sin, zero], axis=1)
    rep = LANES // ATT_HD
    return jnp.concatenate([jnp.tile(c, (1, rep)), jnp.tile(s_hi, (1, rep)), jnp.tile(s_lo, (1, rep))], axis=1)


def _sgu_tables(w_s, b_s, tn):
    tril = jnp.tril(jnp.ones((CHUNK, CHUNK), dtype=bool))
    ws = jnp.where(tril, w_s, 0)
    hd = D_SGU // SGU_HEADS
    bias_p = jnp.repeat(b_s.T, hd, axis=1)
    rep = CHUNK // tn
    eye = jnp.eye(rep, dtype=w_s.dtype)
    ws_s = jnp.einsum('ab,gts->gatbs', eye, ws[:, :tn, :tn]).reshape(SGU_HEADS, CHUNK, CHUNK)
    bias_s = jnp.tile(bias_p[:tn], (rep, 1))
    return ws.astype(BF16), bias_p.astype(F32), ws_s.astype(BF16), bias_s.astype(F32)


def kernel(x_prompt, x_sample, cache_k, cache_v, state_ssm_re, state_ssm_im, state_conv, page_table, ln_in_g, ln_in_b, w_in, w_gate, b_gate, s5_lam_re, s5_lam_im, s5_log_dt, s5_b_re, s5_b_im, s5_c_re, s5_c_im, s5_d, s5_w_glu, sgu_ln_g, sgu_ln_b, sgu_w_s, sgu_b_s, conv_w, conv_b, conv_ln_g, conv_ln_b, att_lam_q1, att_lam_k1, att_lam_q2, att_lam_k2, att_subln_g, w_br_a, w_br_b, w_br_c, w_br_d, w_o, ln1_g, ln1_b, ffn_wg, ffn_wu, ffn_wd, moe_router, moe_router_b, moe_wg, moe_wu, moe_wd, ln2_g, ln2_b):
    bp, tp, d = x_prompt.shape
    bs, ts, _ = x_sample.shape
    depth = w_in.shape[0]
    n_pages = page_table.shape[1]
    past = n_pages * PAGE_SIZE
    alpha = (2 * depth) ** 0.25
    n_p, n_s = bp * tp, bs * ts
    tm_p = min(512, tp)
    tq = min(512, tp)
    l_p = 16
    pg = min(8, n_pages)
    assert tp % tm_p == 0 and tp % CHUNK == 0 and tp % tq == 0 and tp % l_p == 0 and tp >= CONV_W - 1
    assert n_s % CHUNK == 0 and CHUNK % ts == 0 and ts % SUBLANES == 0 and n_pages % pg == 0 and ts <= l_p

    row1 = lambda a: a.reshape(1, -1).astype(F32)
    tab_p = _rope_table(jnp.arange(tp, dtype=jnp.int32))
    tab_s = jnp.tile(_rope_table(past + jnp.arange(ts, dtype=jnp.int32)), (bs, 1))
    zero_state = jnp.zeros((bp, S5_GROUPS, S5_STATE), F32)

    xp = _layernorm(x_prompt.reshape(n_p, d), ln_in_g, ln_in_b, tm_p)
    xs = _layernorm(x_sample.reshape(n_s, d), ln_in_g, ln_in_b, n_s)

    bf = lambda a: a.astype(BF16)
    w_in_b, w_glu_b, w_gate_b, w_o_b = bf(w_in), bf(s5_w_glu), bf(w_gate), bf(w_o)
    w_br_b4 = (bf(w_br_a), bf(w_br_b), bf(w_br_c), bf(w_br_d))
    ffn_b = (bf(ffn_wg), bf(ffn_wu), bf(ffn_wd))
    moe_b = (bf(moe_wg), bf(moe_wu), bf(moe_wd))

    outs = [[] for _ in range(11)]
    for l in range(depth):
        lam_init = 0.8 - 0.6 * math.exp(-0.3 * l)
        lam = (jnp.exp(jnp.sum(att_lam_q1[l].astype(F32) * att_lam_k1[l].astype(F32)))
               - jnp.exp(jnp.sum(att_lam_q2[l].astype(F32) * att_lam_k2[l].astype(F32))) + lam_init)
        lamv = jnp.full((1, ATT_VD), lam, F32)
        gsv = row1(att_subln_g[l]) * (1.0 - lam_init)
        w_in_l = _Layer(w_in_b, l)
        ws_p, sb_p, ws_s, sb_s = _sgu_tables(sgu_w_s[l], sgu_b_s[l], ts)
        conv_args = (conv_w[l].astype(F32), row1(conv_b[l]), row1(conv_ln_g[l]), row1(conv_ln_b[l]))
        sgu_ln = (row1(sgu_ln_g[l]), row1(sgu_ln_b[l]))
        s5_par = (s5_lam_re[l], s5_lam_im[l], s5_log_dt[l], s5_b_re[l], s5_b_im[l], s5_c_re[l], s5_c_im[l])
        mw = {'w_glu': _Layer(w_glu_b, l), 'w_gate': _Layer(w_gate_b, l), 'b_gate': row1(b_gate[l]),
              'w_br_a': _Layer(w_br_b4[0], l), 'w_br_b': _Layer(w_br_b4[1], l),
              'w_br_c': _Layer(w_br_b4[2], l), 'w_br_d': _Layer(w_br_b4[3], l), 'w_o': _Layer(w_o_b, l),
              'ln1_g': row1(ln1_g[l]), 'ln1_b': row1(ln1_b[l])}
        ln2 = (row1(ln2_g[l]), row1(ln2_b[l]))
        i = l // 2
        if l % 2 == 0:
            fw = tuple(_Layer(a, i) for a in ffn_b)
            ffn = lambda x, tm: _ffn(x, *fw, *ln2, tm=tm, alpha=alpha)
        else:
            wr = jnp.pad(moe_router[i].astype(F32), ((0, 0), (0, LANES - N_EXPERTS)))
            wr_hi, wr_lo = _split_bf16(wr)
            br = jnp.pad(moe_router_b[i].astype(F32), (0, LANES - N_EXPERTS)).reshape(1, LANES)
            fw = tuple(_Layer(a, i) for a in moe_b)
            ffn = lambda x, tm: _moe(x, wr_hi, wr_lo, br, *fw, *ln2, tm=tm, alpha=alpha)

        ua, yb, yc, glu, qb, k, v, kb, vx = _proj_prompt(xp, w_in_l, tab_p, *sgu_ln, ws_p, sb_p, *conv_args,
                                                         seq=tp, tm=tm_p, chunk=l_p)
        s5_tab, pw_re, pw_im = _s5_mats(*s5_par, l_p)
        ya, sr_p, si_p = _s5_branch(ua, zero_state, zero_state, s5_tab, pw_re[l_p], pw_im[l_p], s5_d[l],
                                    nb=bp, t=tp)
        yd = _attn_prompt(qb, kb, vx, lamv, gsv, nb=bp, t=tp, tq=tq)
        xp = _merge(xp, ya, yb, yc, yd, mw, tm=tm_p, alpha=alpha)
        xp = ffn(xp, tm_p)
        outs[0].append(k.reshape(bp, ATT_HEADS, 2, ATT_HD, tp).transpose(0, 4, 1, 2, 3))
        outs[1].append(v.reshape(bp, tp, ATT_HEADS, ATT_VD))
        outs[2].append(sr_p)
        outs[3].append(si_p)
        outs[4].append(glu.reshape(bp, tp, D_CONV)[:, tp - (CONV_W - 1):])

        pfx = jnp.pad(state_conv[l].astype(F32), ((0, 0), (HALO_OFF, 0), (0, 0)))
        ua, yb, yc, glu, q, k, v, vn = _proj_sample(xs, w_in_l, tab_s, *sgu_ln, ws_s, sb_s, *conv_args, pfx,
                                                    nb=bs, tn=ts)
        ya, sr_s, si_s = _s5_branch(ua, state_ssm_re[l], state_ssm_im[l], s5_tab, pw_re[ts], pw_im[ts], s5_d[l],
                                    nb=bs, t=ts)
        yd = _attn_sample(q, k, v, cache_k, cache_v, page_table, l, lamv, gsv, nb=bs, tn=ts, pg=pg)
        xs = _merge(xs, ya, yb, yc, yd, mw, tm=n_s, alpha=alpha)
        xs = ffn(xs, n_s)
        outs[5].append(k.reshape(bs, ts, ATT_HEADS, 2, ATT_HD))
        outs[6].append(v.reshape(bs, ts, ATT_HEADS, ATT_VD))
        outs[7].append(sr_s)
        outs[8].append(si_s)
        full = jnp.concatenate([state_conv[l].astype(F32), glu.reshape(bs, ts, D_CONV)], axis=1)
        outs[9].append(full[:, ts:])
        outs[10].append(vn.reshape(bs, ts, D_SGU))

    return (xp.reshape(bp, tp, d), xs.reshape(bs, ts, d), *[jnp.stack(o) for o in outs])
```
